```python
import math
import jax, jax.numpy as jnp
from jax import lax
import numpy as np

D_MODEL = 4096
BATCH = 4
SEQ = 4096
DEPTH = 1

ATTN_HEADS = 12
HEAD_DIM = 128
ATTN_WIDTH = ATTN_HEADS * 2 * HEAD_DIM
SSM_WIDTH = D_MODEL - ATTN_WIDTH
SSM_GROUP = 16
SSM_GROUPS = SSM_WIDTH // SSM_GROUP
SSM_STATE = 64
DT_MIN = 1e-3
DT_MAX = 1e-1
IN_WIDTH = 3 * ATTN_WIDTH + SSM_WIDTH
D_FF = 11008
CONV_WIDTH = 3
Q_BLOCK = 128
EPS = 1e-6

kernel_name = "hymba_diffattn_s5_convffn"


def rms_norm(x, g):
    xf = x.astype(jnp.float32)
    y = xf * lax.rsqrt(jnp.mean(xf * xf, axis=-1, keepdims=True) + EPS)
    return (y * g.astype(jnp.float32)).astype(x.dtype)


def diff_attention(q, k, v, lam, q_gain, k_gain, sub_gain, lam_init):
    b, s = q.shape[:2]
    nb = s // Q_BLOCK
    q = rms_norm(q, q_gain)
    k = rms_norm(k, k_gain)
    scale = HEAD_DIM ** -0.5
    qb = q.reshape(b, nb, Q_BLOCK, ATTN_HEADS, 2, HEAD_DIM).transpose(1, 0, 3, 4, 2, 5)
    kt = k.transpose(0, 2, 3, 1, 4)
    vt = v.transpose(0, 2, 1, 3)
    k_pos = jnp.arange(s)

    def block(args):
        q_blk, i = args
        scores = jnp.einsum('bhmqd,bhmkd->bhmqk', q_blk, kt).astype(jnp.float32) * scale
        q_pos = i * Q_BLOCK + jnp.arange(Q_BLOCK)
        mask = k_pos[None, :] <= q_pos[:, None]
        scores = jnp.where(mask, scores, -jnp.inf)
        p = jax.nn.softmax(scores, axis=-1)
        w = p[:, :, 0] - lam * p[:, :, 1]
        return jnp.einsum('bhqk,bhke->bhqe', w.astype(v.dtype), vt)

    out = lax.map(block, (qb, jnp.arange(nb)))
    out = out.transpose(1, 0, 3, 2, 4).reshape(b, s, ATTN_HEADS, 2 * HEAD_DIM)
    out = rms_norm(out, sub_gain) * (1.0 - lam_init)
    return out.reshape(b, s, ATTN_WIDTH)


def s5_ssm(u, log_dt, a_re, a_im, b_re, b_im, c_re, c_im, d_skip, w_glu, b_glu, out_gain):
    f32 = jnp.float32
    bsz, s = u.shape[:2]
    ug = u.reshape(bsz, s, SSM_GROUPS, SSM_GROUP).astype(f32)
    dt = jnp.exp(log_dt.astype(f32))[:, None]
    ar, ai = a_re.astype(f32), a_im.astype(f32)
    mag = jnp.exp(ar * dt)
    lb_re = mag * jnp.cos(ai * dt)
    lb_im = mag * jnp.sin(ai * dt)
    pr, pi_ = lb_re - 1.0, lb_im
    den = ar * ar + ai * ai
    z_re = ((pr * ar + pi_ * ai) / den)[..., None]
    z_im = ((pi_ * ar - pr * ai) / den)[..., None]
    br, bi = b_re.astype(f32), b_im.astype(f32)
    bb_re = z_re * br - z_im * bi
    bb_im = z_re * bi + z_im * br
    bu_re = jnp.einsum('gnc,bsgc->bsgn', bb_re, ug)
    bu_im = jnp.einsum('gnc,bsgc->bsgn', bb_im, ug)
    la_re = jnp.broadcast_to(lb_re, bu_re.shape)
    la_im = jnp.broadcast_to(lb_im, bu_im.shape)

    def combine(e1, e2):
        a1r, a1i, b1r, b1i = e1
        a2r, a2i, b2r, b2i = e2
        return (a2r * a1r - a2i * a1i,
                a2r * a1i + a2i * a1r,
                a2r * b1r - a2i * b1i + b2r,
                a2r * b1i + a2i * b1r + b2i)

    _, _, h_re, h_im = lax.associative_scan(combine, (la_re, la_im, bu_re, bu_im), axis=1)
    y = (jnp.einsum('gcn,bsgn->bsgc', c_re.astype(f32), h_re)
         - jnp.einsum('gcn,bsgn->bsgc', c_im.astype(f32), h_im)
         + d_skip.astype(f32) * ug)
    y = jax.nn.gelu(y.reshape(bsz, s, SSM_WIDTH))
    y = y * jax.nn.sigmoid(y @ w_glu.astype(f32) + b_glu.astype(f32))
    return rms_norm(y.astype(u.dtype), out_gain)


def conv_glu_ffn(h, w_gate, w_up, conv_w, conv_b, w_down):
    s = h.shape[1]
    g = h @ w_gate
    up = h @ w_up
    gp = jnp.pad(g, ((0, 0), (CONV_WIDTH - 1, 0), (0, 0)))
    gc = conv_b + conv_w[0] * gp[:, 0:s]
    for j in range(1, CONV_WIDTH):
        gc = gc + conv_w[j] * gp[:, j:j + s]
    return (jax.nn.silu(gc) * up) @ w_down


def setup_inputs(seed: int = 0) -> dict:
    key = jax.random.key(seed)
    ks = jax.random.split(key, 32)
    L, D, F = DEPTH, D_MODEL, D_FF
    G, N, C = SSM_GROUPS, SSM_STATE, SSM_GROUP
    nrm = lambda k, shape, s: jax.random.normal(k, shape, jnp.float32) * s
    n_idx = jnp.arange(N, dtype=jnp.float32)
    return {
        "x": nrm(ks[0], (BATCH, SEQ, D), 1.0),
        "attn_norm": 1.0 + nrm(ks[1], (L, D), 0.02),
        "w_in": nrm(ks[2], (L, D, IN_WIDTH), D ** -0.5),
        "q_gain": 1.0 + nrm(ks[3], (L, HEAD_DIM), 0.02),
        "k_gain": 1.0 + nrm(ks[4], (L, HEAD_DIM), 0.02),
        "lam_q1": nrm(ks[5], (L, HEAD_DIM), 0.1),
        "lam_k1": nrm(ks[6], (L, HEAD_DIM), 0.1),
        "lam_q2": nrm(ks[7], (L, HEAD_DIM), 0.1),
        "lam_k2": nrm(ks[8], (L, HEAD_DIM), 0.1),
        "sub_gain": 1.0 + nrm(ks[9], (L, 2 * HEAD_DIM), 0.02),
        "ssm_log_dt": jax.random.uniform(ks[10], (L, G), jnp.float32, math.log(DT_MIN), math.log(DT_MAX)),
        "ssm_a_re": -0.5 + nrm(ks[11], (L, G, N), 0.01),
        "ssm_a_im": jnp.pi * n_idx + nrm(ks[12], (L, G, N), 0.01),
        "ssm_b_re": nrm(ks[13], (L, G, N, C), (2 * C) ** -0.5),
        "ssm_b_im": nrm(ks[14], (L, G, N, C), (2 * C) ** -0.5),
        "ssm_c_re": nrm(ks[15], (L, G, C, N), (2 * N) ** -0.5),
        "ssm_c_im": nrm(ks[16], (L, G, C, N), (2 * N) ** -0.5),
        "ssm_d": nrm(ks[17], (L, G, C), 1.0),
        "ssm_w_glu": nrm(ks[18], (L, SSM_WIDTH, SSM_WIDTH), SSM_WIDTH ** -0.5),
        "ssm_b_glu": nrm(ks[19], (L, SSM_WIDTH), 0.01),
        "ssm_out_gain": 1.0 + nrm(ks[20], (L, SSM_WIDTH), 0.02),
        "w_out": nrm(ks[21], (L, D, D), D ** -0.5),
        "ffn_norm": 1.0 + nrm(ks[22], (L, D), 0.02),
        "w_gate": nrm(ks[23], (L, D, F), D ** -0.5),
        "w_up": nrm(ks[24], (L, D, F), D ** -0.5),
        "conv_w": nrm(ks[25], (L, CONV_WIDTH, F), CONV_WIDTH ** -0.5),
        "conv_b": nrm(ks[26], (L, F), 0.01),
        "w_down": nrm(ks[27], (L, F, D), F ** -0.5),
    }


def reference(x, attn_norm, w_in, q_gain, k_gain, lam_q1, lam_k1, lam_q2, lam_k2, sub_gain,
              ssm_log_dt, ssm_a_re, ssm_a_im, ssm_b_re, ssm_b_im, ssm_c_re, ssm_c_im, ssm_d,
              ssm_w_glu, ssm_b_glu, ssm_out_gain, w_out, ffn_norm, w_gate, w_up, conv_w, conv_b,
              w_down):
    b, s = x.shape[:2]
    for l in range(DEPTH):
        lam_init = 0.8 - 0.6 * math.exp(-0.3 * l)
        h = rms_norm(x, attn_norm[l])
        proj = h @ w_in[l]
        q = proj[..., :ATTN_WIDTH].reshape(b, s, ATTN_HEADS, 2, HEAD_DIM)
        k = proj[..., ATTN_WIDTH:2 * ATTN_WIDTH].reshape(b, s, ATTN_HEADS, 2, HEAD_DIM)
        v = proj[..., 2 * ATTN_WIDTH:3 * ATTN_WIDTH].reshape(b, s, ATTN_HEADS, 2 * HEAD_DIM)
        u = proj[..., 3 * ATTN_WIDTH:]
        lam = (jnp.exp(jnp.sum(lam_q1[l].astype(jnp.float32) * lam_k1[l].astype(jnp.float32)))
               - jnp.exp(jnp.sum(lam_q2[l].astype(jnp.float32) * lam_k2[l].astype(jnp.float32)))
               + lam_init)
        attn_out = diff_attention(q, k, v, lam, q_gain[l], k_gain[l], sub_gain[l], lam_init)
        ssm_out = s5_ssm(u, ssm_log_dt[l], ssm_a_re[l], ssm_a_im[l], ssm_b_re[l], ssm_b_im[l],
                         ssm_c_re[l], ssm_c_im[l], ssm_d[l], ssm_w_glu[l], ssm_b_glu[l],
                         ssm_out_gain[l])
        mixed = jnp.concatenate([attn_out, ssm_out.astype(attn_out.dtype)], axis=-1)
        x = x + mixed @ w_out[l]
        h2 = rms_norm(x, ffn_norm[l])
        x = x + conv_glu_ffn(h2, w_gate[l], w_up[l], conv_w[l], conv_b[l], w_down[l])
    return x
```

```python
import functools
import math

import jax
import jax.numpy as jnp
from jax import lax
from jax.experimental import pallas as pl
from jax.experimental.pallas import tpu as pltpu

D_MODEL = 4096
ATTN_HEADS = 12
HEAD_DIM = 128
ATTN_WIDTH = ATTN_HEADS * 2 * HEAD_DIM
SSM_WIDTH = D_MODEL - ATTN_WIDTH
SSM_GROUP = 16
SSM_GROUPS = SSM_WIDTH // SSM_GROUP
SSM_STATE = 64
IN_WIDTH = 3 * ATTN_WIDTH + SSM_WIDTH
D_FF = 11008
CONV_WIDTH = 3
EPS = 1e-6

V7X_LANES = 128
V7X_SUBLANES = 8
V7X_BF16_ROWS_PER_TILE = 2 * V7X_SUBLANES
V7X_VMEM_BYTES = 64 * 1024 * 1024
V7X_VMEM_RESERVED_BYTES = 6 * 1024 * 1024

F_TILE = 512
D_FF_PAD = -(-D_FF // F_TILE) * F_TILE
SSM_CLUSTER = V7X_LANES // SSM_GROUP
SSM_CLUSTERS = SSM_GROUPS // SSM_CLUSTER
SSM_CL_STATE = SSM_CLUSTER * SSM_STATE
SSM_CHUNK = 128


def _vmem_limit(*nbytes):
    want = int(sum(nbytes)) + 12 * 1024 * 1024
    return min(want, V7X_VMEM_BYTES - V7X_VMEM_RESERVED_BYTES)


def _nbytes(shape, dtype):
    return math.prod(shape) * jnp.dtype(dtype).itemsize


def _rms_rows(x, gain):
    ms = jnp.mean(x * x, axis=-1, keepdims=True)
    return x * lax.rsqrt(ms + EPS) * gain


def _norm_into(x_ref, gain_ref, h_ref, *, rows, row_chunk, h_offset):
    def body(c, carry):
        r = pl.multiple_of(c * row_chunk, row_chunk)
        xb = x_ref[pl.ds(r, row_chunk), :]
        h_ref[pl.ds(h_offset + r, row_chunk), :] = _rms_rows(xb, gain_ref[...]).astype(h_ref.dtype)
        return carry
    lax.fori_loop(0, rows // row_chunk, body, 0)


def _in_proj_kernel(x_ref, g_ref, w_ref, cg_ref, o_ref, h_ref, *, tm, tn, n_norm_tiles):
    j = pl.program_id(1)

    @pl.when(j == 0)
    def _():
        _norm_into(x_ref, g_ref, h_ref, rows=tm, row_chunk=32, h_offset=0)

    acc = jnp.dot(h_ref[...], w_ref[...], preferred_element_type=jnp.float32)

    @pl.when(j < n_norm_tiles)
    def _():
        for s in range(tn // HEAD_DIM):
            sl = slice(s * HEAD_DIM, (s + 1) * HEAD_DIM)
            o_ref[:, sl] = _rms_rows(acc[:, sl], cg_ref[:, sl]).astype(o_ref.dtype)

    @pl.when(j >= n_norm_tiles)
    def _():
        o_ref[...] = acc.astype(o_ref.dtype)


def _in_proj(x2d, gain, w_bf16, col_gain, *, tm=512, tn=1024):
    m, k = x2d.shape
    n = w_bf16.shape[1]
    n_norm_tiles = 2 * ATTN_WIDTH // tn
    assert 2 * ATTN_WIDTH % tn == 0 and m % tm == 0 and n % tn == 0
    kern = functools.partial(_in_proj_kernel, tm=tm, tn=tn, n_norm_tiles=n_norm_tiles)
    vmem = _vmem_limit(2 * _nbytes((tm, k), jnp.float32), _nbytes((tm, k), jnp.bfloat16),
                       2 * _nbytes((k, tn), jnp.bfloat16), 2 * _nbytes((tm, tn), jnp.bfloat16),
                       _nbytes((tm, tn), jnp.float32))
    return pl.pallas_call(
        kern,
        out_shape=jax.ShapeDtypeStruct((m, n), jnp.bfloat16),
        grid=(m // tm, n // tn),
        in_specs=[
            pl.BlockSpec((tm, k), lambda i, j: (i, 0)),
            pl.BlockSpec((1, k), lambda i, j: (0, 0)),
            pl.BlockSpec((k, tn), lambda i, j: (0, j)),
            pl.BlockSpec((1, tn), lambda i, j: (0, j)),
        ],
        out_specs=pl.BlockSpec((tm, tn), lambda i, j: (i, j)),
        scratch_shapes=[pltpu.VMEM((tm, k), jnp.bfloat16)],
        compiler_params=pltpu.CompilerParams(
            dimension_semantics=("arbitrary", "arbitrary"), vmem_limit_bytes=vmem),
        name="in_proj",
    )(x2d, gain, w_bf16, col_gain)


def _attn_kernel(lamv_ref, q_ref, k_ref, v_ref, sg_ref, o_ref, m_ref, l_ref, acc_ref,
                 *, blk, lam_init):
    qi = pl.program_id(2)
    m_ref[...] = jnp.full(m_ref.shape, -jnp.inf, jnp.float32)
    l_ref[...] = jnp.zeros(l_ref.shape, jnp.float32)
    acc_ref[...] = jnp.zeros(acc_ref.shape, jnp.float32)

    def kv_step(j, masked):
        ks = pl.ds(pl.multiple_of(j * blk, blk), blk)
        vb = v_ref[ks, :]
        for mi in range(2):
            hs = slice(mi * HEAD_DIM, (mi + 1) * HEAD_DIM)
            s = lax.dot_general(q_ref[:, hs], k_ref[ks, hs], (((1,), (1,)), ((), ())),
                                preferred_element_type=jnp.float32)
            if masked:
                row = lax.broadcasted_iota(jnp.int32, s.shape, 0)
                col = lax.broadcasted_iota(jnp.int32, s.shape, 1)
                s = jnp.where(col <= row, s, -jnp.inf)
            m_old = m_ref[mi]
            m_new = jnp.maximum(m_old, jnp.max(s, axis=-1, keepdims=True))
            alpha = jnp.exp(m_old - m_new)
            p = jnp.exp(s - m_new)
            l_ref[mi] = alpha * l_ref[mi] + jnp.sum(p, axis=-1, keepdims=True)
            acc_ref[mi] = alpha * acc_ref[mi] + jnp.dot(
                p.astype(vb.dtype), vb, preferred_element_type=jnp.float32)
            m_ref[mi] = m_new

    def full_step(j, carry):
        kv_step(j, False)
        return carry

    lax.fori_loop(0, qi, full_step, 0)
    kv_step(qi, True)

    lamv = lamv_ref[...]
    lam = (jnp.exp(jnp.sum(lamv[0:1] * lamv[1:2], axis=-1, keepdims=True))
           - jnp.exp(jnp.sum(lamv[2:3] * lamv[3:4], axis=-1, keepdims=True)) + lam_init)
    o = acc_ref[0] / l_ref[0] - lam * (acc_ref[1] / l_ref[1])
    o_ref[...] = (_rms_rows(o, sg_ref[...]) * (1.0 - lam_init)).astype(o_ref.dtype)


def _attention(proj, lamv, sub_gain, *, batch, seq, lam_init, blk=512):
    nq = seq // blk
    hw = 2 * HEAD_DIM
    kern = functools.partial(_attn_kernel, blk=blk, lam_init=lam_init)
    vmem = _vmem_limit(4 * _nbytes((seq, hw), jnp.bfloat16), 4 * _nbytes((blk, hw), jnp.bfloat16),
                       2 * _nbytes((blk, hw), jnp.float32), 4 * _nbytes((blk, V7X_LANES), jnp.float32),
                       6 * _nbytes((blk, blk), jnp.float32))
    return pl.pallas_call(
        kern,
        out_shape=jax.ShapeDtypeStruct((batch * seq, D_MODEL), jnp.bfloat16),
        grid=(batch, ATTN_HEADS, nq),
        in_specs=[
            pl.BlockSpec((4, HEAD_DIM), lambda b, h, i: (0, 0)),
            pl.BlockSpec((blk, hw), lambda b, h, i: (b * nq + i, h)),
            pl.BlockSpec((seq, hw), lambda b, h, i: (b, ATTN_HEADS + h)),
            pl.BlockSpec((seq, hw), lambda b, h, i: (b, 2 * ATTN_HEADS + h)),
            pl.BlockSpec((1, hw), lambda b, h, i: (0, 0)),
        ],
        out_specs=pl.BlockSpec((blk, hw), lambda b, h, i: (b * nq + i, h)),
        scratch_shapes=[
            pltpu.VMEM((2, blk, 1), jnp.float32),
            pltpu.VMEM((2, blk, 1), jnp.float32),
            pltpu.VMEM((2, blk, hw), jnp.float32),
        ],
        compiler_params=pltpu.CompilerParams(
            dimension_semantics=("arbitrary", "arbitrary", "arbitrary"), vmem_limit_bytes=vmem),
        name="diff_attn",
    )(lamv, proj, proj, proj, sub_gain)


def _cmul(ar, ai, br, bi):
    return ar * br - ai * bi, ar * bi + ai * br


def _ssm_kernel(u_ref, mixed_ref, bmat_ref, cmat_ref, pinv_ref, pfwd_ref, pcar_ref, d_ref,
                wglu_ref, bglu_ref, og_ref, o_ref, h_ref, y_ref, *, chunk):
    del mixed_ref
    c = pl.program_id(1)

    @pl.when(c == 0)
    def _():
        h_ref[...] = jnp.zeros(h_ref.shape, jnp.float32)

    n = SSM_CL_STATE
    row = lax.broadcasted_iota(jnp.int32, (chunk, chunk), 0)
    col = lax.broadcasted_iota(jnp.int32, (chunk, chunk), 1)
    tri = (col <= row).astype(jnp.bfloat16)

    for cl in range(SSM_CLUSTERS):
        ls = slice(cl * V7X_LANES, (cl + 1) * V7X_LANES)
        bu = jnp.dot(u_ref[:, ls], bmat_ref[cl], preferred_element_type=jnp.float32)
        xr, xi = _cmul(pinv_ref[cl, :, :n], pinv_ref[cl, :, n:], bu[:, :n], bu[:, n:])
        x = jnp.concatenate([xr, xi], axis=1).astype(jnp.bfloat16)
        cs = jnp.dot(tri, x, preferred_element_type=jnp.float32)
        hp = h_ref[cl:cl + 1, :]
        cr, ci = _cmul(pcar_ref[cl:cl + 1, :n], pcar_ref[cl:cl + 1, n:], hp[:, :n], hp[:, n:])
        hr, hi = _cmul(pfwd_ref[cl, :, :n], pfwd_ref[cl, :, n:], cs[:, :n] + cr, cs[:, n:] + ci)
        h = jnp.concatenate([hr, hi], axis=1)
        h_ref[cl:cl + 1, :] = h[chunk - 1:chunk, :]
        y_ref[:, ls] = jnp.dot(h.astype(jnp.bfloat16), cmat_ref[cl],
                               preferred_element_type=jnp.float32)

    y = y_ref[...] + d_ref[...] * u_ref[...].astype(jnp.float32)
    y = jax.nn.gelu(y, approximate=True)
    z = jnp.dot(y.astype(jnp.bfloat16), wglu_ref[...], preferred_element_type=jnp.float32)
    z = z + bglu_ref[...]
    y = y * (1.0 / (1.0 + jnp.exp(-z)))
    o_ref[...] = _rms_rows(y, og_ref[...]).astype(o_ref.dtype)


def _ssm_tables(log_dt, a_re, a_im, b_re, b_im, c_re, c_im, chunk):
    f32 = jnp.float32
    g, n, c = SSM_GROUPS, SSM_STATE, SSM_GROUP
    dt = jnp.exp(log_dt.astype(f32))[:, None]
    ar, ai = a_re.astype(f32), a_im.astype(f32)
    mag = jnp.exp(ar * dt)
    lb_re = mag * jnp.cos(ai * dt)
    lb_im = mag * jnp.sin(ai * dt)
    pr, pi_ = lb_re - 1.0, lb_im
    den = ar * ar + ai * ai
    z_re = ((pr * ar + pi_ * ai) / den)[..., None]
    z_im = ((pi_ * ar - pr * ai) / den)[..., None]
    br, bi = b_re.astype(f32), b_im.astype(f32)
    bb_re = z_re * br - z_im * bi
    bb_im = z_re * bi + z_im * br

    def powers(k):
        kk = k.astype(f32)[:, None, None]
        pm = jnp.exp(kk * (ar * dt)[None])
        ph = kk * (ai * dt)[None]
        return pm * jnp.cos(ph), pm * jnp.sin(ph)

    def cl_cols(re, im):
        k = re.shape[0]
        re = re.reshape(k, SSM_CLUSTERS, SSM_CL_STATE)
        im = im.reshape(k, SSM_CLUSTERS, SSM_CL_STATE)
        return jnp.concatenate([re, im], axis=-1).transpose(1, 0, 2)

    c0 = chunk // 2
    t = jnp.arange(chunk)
    pinv = cl_cols(*powers(c0 - t))
    pfwd = cl_cols(*powers(t - c0))
    pcar = cl_cols(*powers(jnp.array([c0 + 1])))[:, 0, :]

    eye = jnp.eye(SSM_CLUSTER, dtype=f32)
    def bpart(bb):
        b4 = bb.reshape(SSM_CLUSTERS, SSM_CLUSTER, n, c)
        return jnp.einsum('kgnc,gh->kgchn', b4, eye).reshape(SSM_CLUSTERS, V7X_LANES, SSM_CL_STATE)
    bmat = jnp.concatenate([bpart(bb_re), bpart(bb_im)], axis=-1).astype(jnp.bfloat16)
    def cpart(cc):
        c4 = cc.astype(f32).reshape(SSM_CLUSTERS, SSM_CLUSTER, c, n)
        return jnp.einsum('kgcn,gh->kgnhc', c4, eye).reshape(SSM_CLUSTERS, SSM_CL_STATE, V7X_LANES)
    cmat = jnp.concatenate([cpart(c_re), -cpart(c_im)], axis=1).astype(jnp.bfloat16)
    return bmat, cmat, pinv, pfwd, pcar


def _ssm(proj, mixed, tables, d_skip, w_glu, b_glu, out_gain, *, batch, seq, chunk=SSM_CHUNK):
    bmat, cmat, pinv, pfwd, pcar = tables
    nc = seq // chunk
    u_col = 3 * ATTN_WIDTH // SSM_WIDTH
    o_col = ATTN_WIDTH // SSM_WIDTH
    kern = functools.partial(_ssm_kernel, chunk=chunk)
    const3 = lambda b, c: (0, 0, 0)
    const2 = lambda b, c: (0, 0)
    vmem = _vmem_limit(2 * (bmat.nbytes + cmat.nbytes + pinv.nbytes + pfwd.nbytes + pcar.nbytes),
                       4 * _nbytes((SSM_WIDTH, SSM_WIDTH), jnp.bfloat16),
                       8 * _nbytes((chunk, SSM_WIDTH), jnp.float32))
    return pl.pallas_call(
        kern,
        out_shape=jax.ShapeDtypeStruct(mixed.shape, mixed.dtype),
        grid=(batch, nc),
        in_specs=[
            pl.BlockSpec((chunk, SSM_WIDTH), lambda b, c: (b * nc + c, u_col)),
            pl.BlockSpec(memory_space=pl.ANY),
            pl.BlockSpec(bmat.shape, const3),
            pl.BlockSpec(cmat.shape, const3),
            pl.BlockSpec(pinv.shape, const3),
            pl.BlockSpec(pfwd.shape, const3),
            pl.BlockSpec(pcar.shape, const2),
            pl.BlockSpec((1, SSM_WIDTH), const2),
            pl.BlockSpec((SSM_WIDTH, SSM_WIDTH), const2),
            pl.BlockSpec((1, SSM_WIDTH), const2),
            pl.BlockSpec((1, SSM_WIDTH), const2),
        ],
        out_specs=pl.BlockSpec((chunk, SSM_WIDTH), lambda b, c: (b * nc + c, o_col)),
        scratch_shapes=[
            pltpu.VMEM((SSM_CLUSTERS, 2 * SSM_CL_STATE), jnp.float32),
            pltpu.VMEM((chunk, SSM_WIDTH), jnp.float32),
        ],
        input_output_aliases={1: 0},
        compiler_params=pltpu.CompilerParams(
            dimension_semantics=("arbitrary", "arbitrary"), vmem_limit_bytes=vmem),
        name="s5_ssm",
    )(proj, mixed, bmat, cmat, pinv, pfwd, pcar, d_skip, w_glu, b_glu, out_gain)


def _res_matmul_kernel(a_ref, w_ref, r_ref, o_ref):
    o_ref[...] = r_ref[...] + jnp.dot(a_ref[...], w_ref[...], preferred_element_type=jnp.float32)


def _res_matmul(a, w, res, *, tm, tn, name):
    m, k = a.shape
    n = w.shape[1]
    assert m % tm == 0 and n % tn == 0
    vmem = _vmem_limit(2 * _nbytes((tm, k), a.dtype), 2 * _nbytes((k, tn), w.dtype),
                       5 * _nbytes((tm, tn), jnp.float32))
    return pl.pallas_call(
        _res_matmul_kernel,
        out_shape=jax.ShapeDtypeStruct((m, n), jnp.float32),
        grid=(m // tm, n // tn),
        in_specs=[
            pl.BlockSpec((tm, k), lambda i, j: (i, 0)),
            pl.BlockSpec((k, tn), lambda i, j: (0, j)),
            pl.BlockSpec((tm, tn), lambda i, j: (i, j)),
        ],
        out_specs=pl.BlockSpec((tm, tn), lambda i, j: (i, j)),
        compiler_params=pltpu.CompilerParams(
            dimension_semantics=("parallel", "parallel"), vmem_limit_bytes=vmem),
        name=name,
    )(a, w, res)


def _ffn_up_kernel(x_ref, halo_ref, g_ref, wg_ref, wu_ref, cw_ref, cb_ref, o_ref, h_ref, gs_ref,
                   *, tm, blocks_per_seq):
    i = pl.program_id(0)
    j = pl.program_id(1)
    halo = V7X_BF16_ROWS_PER_TILE

    @pl.when(j == 0)
    def _():
        hh = _rms_rows(halo_ref[...], g_ref[...])
        hh = jnp.where(i % blocks_per_seq == 0, 0.0, hh)
        h_ref[0:halo, :] = hh.astype(h_ref.dtype)
        _norm_into(x_ref, g_ref, h_ref, rows=tm, row_chunk=32, h_offset=halo)

    gs_ref[...] = jnp.dot(h_ref[...], wg_ref[...], preferred_element_type=jnp.float32)
    up = jnp.dot(h_ref[halo:, :], wu_ref[...], preferred_element_type=jnp.float32)
    gc = (cb_ref[...]
          + cw_ref[0:1, :] * gs_ref[pl.ds(halo - 2, tm), :]
          + cw_ref[1:2, :] * gs_ref[pl.ds(halo - 1, tm), :]
          + cw_ref[2:3, :] * gs_ref[pl.ds(halo, tm), :])
    act = gc * (1.0 / (1.0 + jnp.exp(-gc))) * up
    o_ref[...] = act.astype(o_ref.dtype)


def _ffn_up(x1, gain, wg, wu, conv_w, conv_b, *, seq, tm=512, tf=F_TILE):
    m, k = x1.shape
    f = wg.shape[1]
    halo = V7X_BF16_ROWS_PER_TILE
    assert m % tm == 0 and seq % tm == 0 and f % tf == 0
    kern = functools.partial(_ffn_up_kernel, tm=tm, blocks_per_seq=seq // tm)
    vmem = _vmem_limit(2 * _nbytes((tm, k), jnp.float32), _nbytes((tm + halo, k), jnp.bfloat16),
                       4 * _nbytes((k, tf), jnp.bfloat16), 2 * _nbytes((tm, tf), jnp.bfloat16),
                       4 * _nbytes((tm + halo, tf), jnp.float32))
    return pl.pallas_call(
        kern,
        out_shape=jax.ShapeDtypeStruct((m, f), jnp.bfloat16),
        grid=(m // tm, f // tf),
        in_specs=[
            pl.BlockSpec((tm, k), lambda i, j: (i, 0)),
            pl.BlockSpec((halo, k), lambda i, j: (jnp.maximum(i * (tm // halo) - 1, 0), 0)),
            pl.BlockSpec((1, k), lambda i, j: (0, 0)),
            pl.BlockSpec((k, tf), lambda i, j: (0, j)),
            pl.BlockSpec((k, tf), lambda i, j: (0, j)),
            pl.BlockSpec((CONV_WIDTH, tf), lambda i, j: (0, j)),
            pl.BlockSpec((1, tf), lambda i, j: (0, j)),
        ],
        out_specs=pl.BlockSpec((tm, tf), lambda i, j: (i, j)),
        scratch_shapes=[
            pltpu.VMEM((tm + halo, k), jnp.bfloat16),
            pltpu.VMEM((tm + halo, tf), jnp.float32),
        ],
        compiler_params=pltpu.CompilerParams(
            dimension_semantics=("arbitrary", "arbitrary"), vmem_limit_bytes=vmem),
        name="ffn_up",
    )(x1, x1, gain, wg, wu, conv_w, conv_b)


def _pad_cols(a, n):
    return jnp.pad(a, ((0, 0), (0, n - a.shape[1])))


def kernel(x, attn_norm, w_in, q_gain, k_gain, lam_q1, lam_k1, lam_q2, lam_k2, sub_gain, ssm_log_dt, ssm_a_re, ssm_a_im, ssm_b_re, ssm_b_im, ssm_c_re, ssm_c_im, ssm_d, ssm_w_glu, ssm_b_glu, ssm_out_gain, w_out, ffn_norm, w_gate, w_up, conv_w, conv_b, w_down):
    b, s, d = x.shape
    depth = w_in.shape[0]
    bf16, f32 = jnp.bfloat16, jnp.float32
    xf = x.reshape(b * s, d)
    for l in range(depth):
        lam_init = 0.8 - 0.6 * math.exp(-0.3 * l)
        scale = HEAD_DIM ** -0.5
        col_gain = jnp.concatenate([
            jnp.tile(q_gain[l].astype(f32) * scale, ATTN_WIDTH // HEAD_DIM),
            jnp.tile(k_gain[l].astype(f32), ATTN_WIDTH // HEAD_DIM),
            jnp.ones((IN_WIDTH - 2 * ATTN_WIDTH,), f32)])[None, :]
        proj = _in_proj(xf, attn_norm[l].astype(f32)[None, :], w_in[l].astype(bf16), col_gain)

        lamv = jnp.stack([lam_q1[l], lam_k1[l], lam_q2[l], lam_k2[l]]).astype(f32)
        mixed = _attention(proj, lamv, sub_gain[l].astype(f32)[None, :],
                           batch=b, seq=s, lam_init=lam_init)

        tables = _ssm_tables(ssm_log_dt[l], ssm_a_re[l], ssm_a_im[l], ssm_b_re[l], ssm_b_im[l],
                             ssm_c_re[l], ssm_c_im[l], SSM_CHUNK)
        mixed = _ssm(proj, mixed, tables, ssm_d[l].astype(f32).reshape(1, SSM_WIDTH),
                     ssm_w_glu[l].astype(bf16), ssm_b_glu[l].astype(f32)[None, :],
                     ssm_out_gain[l].astype(f32)[None, :], batch=b, seq=s)

        x1 = _res_matmul(mixed, w_out[l].astype(bf16), xf, tm=512, tn=1024, name="out_proj")

        wg = _pad_cols(w_gate[l].astype(bf16), D_FF_PAD)
        wu = _pad_cols(w_up[l].astype(bf16), D_FF_PAD)
        cw = _pad_cols(conv_w[l].astype(f32), D_FF_PAD)
        cb = _pad_cols(conv_b[l].astype(f32)[None, :], D_FF_PAD)
        wd = jnp.pad(w_down[l].astype(bf16), ((0, D_FF_PAD - D_FF), (0, 0)))
        act = _ffn_up(x1, ffn_norm[l].astype(f32)[None, :], wg, wu, cw, cb, seq=s)
        xf = _res_matmul(act, wd, x1, tm=512, tn=512, name="ffn_down")
    return xf.reshape(b, s, d)
```

```python
import functools
import math

import jax
import jax.numpy as jnp
from jax import lax
from jax.experimental import pallas as pl
from jax.experimental.pallas import tpu as pltpu

D_MODEL = 4096
ATTN_HEADS = 12
HEAD_DIM = 128
ATTN_WIDTH = ATTN_HEADS * 2 * HEAD_DIM
SSM_WIDTH = D_MODEL - ATTN_WIDTH
SSM_GROUP = 16
SSM_GROUPS = SSM_WIDTH // SSM_GROUP
SSM_STATE = 64
IN_WIDTH = 3 * ATTN_WIDTH + SSM_WIDTH
D_FF = 11008
CONV_WIDTH = 3
EPS = 1e-6

V7X_LANES = 128
V7X_SUBLANES = 8
V7X_BF16_ROWS_PER_TILE = 2 * V7X_SUBLANES
V7X_VMEM_BYTES = 64 * 1024 * 1024
V7X_VMEM_RESERVED_BYTES = 6 * 1024 * 1024

F_TILE = 512
SSM_CLUSTER = V7X_LANES // SSM_GROUP
SSM_CLUSTERS = SSM_GROUPS // SSM_CLUSTER
SSM_CL_STATE = SSM_CLUSTER * SSM_STATE
SSM_CHUNK = 128


def _vmem_limit(*nbytes):
    want = int(sum(nbytes)) + 12 * 1024 * 1024
    return min(want, V7X_VMEM_BYTES - V7X_VMEM_RESERVED_BYTES)


def _nbytes(shape, dtype):
    return math.prod(shape) * jnp.dtype(dtype).itemsize


def _rms_rows(x, gain):
    ms = jnp.mean(x * x, axis=-1, keepdims=True)
    return x * lax.rsqrt(ms + EPS) * gain


def _norm_into(x_ref, gain_ref, h_ref, *, rows, row_chunk, h_offset):
    def body(c, carry):
        r = pl.multiple_of(c * row_chunk, row_chunk)
        xb = x_ref[pl.ds(r, row_chunk), :]
        h_ref[pl.ds(h_offset + r, row_chunk), :] = _rms_rows(xb, gain_ref[...]).astype(h_ref.dtype)
        return carry
    lax.fori_loop(0, rows // row_chunk, body, 0)


def _in_proj_kernel(x_ref, g_ref, w_ref, cg_ref, o_ref, h_ref, *, tm, tn, n_norm_tiles):
    j = pl.program_id(1)

    @pl.when(j == 0)
    def _():
        _norm_into(x_ref, g_ref, h_ref, rows=tm, row_chunk=32, h_offset=0)

    acc = jnp.dot(h_ref[...], w_ref[...], preferred_element_type=jnp.float32)

    @pl.when(j < n_norm_tiles)
    def _():
        for s in range(tn // HEAD_DIM):
            sl = slice(s * HEAD_DIM, (s + 1) * HEAD_DIM)
            o_ref[:, sl] = _rms_rows(acc[:, sl], cg_ref[:, sl]).astype(o_ref.dtype)

    @pl.when(j >= n_norm_tiles)
    def _():
        o_ref[...] = acc.astype(o_ref.dtype)


def _in_proj(x2d, gain, w_bf16, col_gain, *, tm=512, tn=1024):
    m, k = x2d.shape
    n = w_bf16.shape[1]
    n_norm_tiles = 2 * ATTN_WIDTH // tn
    assert 2 * ATTN_WIDTH % tn == 0 and m % tm == 0 and n % tn == 0
    kern = functools.partial(_in_proj_kernel, tm=tm, tn=tn, n_norm_tiles=n_norm_tiles)
    vmem = _vmem_limit(2 * _nbytes((tm, k), jnp.float32), _nbytes((tm, k), jnp.bfloat16),
                       2 * _nbytes((k, tn), jnp.bfloat16), 2 * _nbytes((tm, tn), jnp.bfloat16),
                       _nbytes((tm, tn), jnp.float32))
    return pl.pallas_call(
        kern,
        out_shape=jax.ShapeDtypeStruct((m, n), jnp.bfloat16),
        grid=(m // tm, n // tn),
        in_specs=[
            pl.BlockSpec((tm, k), lambda i, j: (i, 0)),
            pl.BlockSpec((1, k), lambda i, j: (0, 0)),
            pl.BlockSpec((k, tn), lambda i, j: (0, j)),
            pl.BlockSpec((1, tn), lambda i, j: (0, j)),
        ],
        out_specs=pl.BlockSpec((tm, tn), lambda i, j: (i, j)),
        scratch_shapes=[pltpu.VMEM((tm, k), jnp.bfloat16)],
        compiler_params=pltpu.CompilerParams(
            dimension_semantics=("arbitrary", "arbitrary"), vmem_limit_bytes=vmem),
        name="in_proj",
    )(x2d, gain, w_bf16, col_gain)


_UNSHIFTED_SOFTMAX_MAX_LOG2 = 60.0


def _attn_scores(q_ref, k_ref, ks, mi, masked):
    hs = slice(mi * HEAD_DIM, (mi + 1) * HEAD_DIM)
    s = lax.dot_general(q_ref[:, hs], k_ref[ks, hs], (((1,), (1,)), ((), ())),
                        preferred_element_type=jnp.float32)
    if masked:
        row = lax.broadcasted_iota(jnp.int32, s.shape, 0)
        col = lax.broadcasted_iota(jnp.int32, s.shape, 1)
        s = jnp.where(col <= row, s, -jnp.inf)
    return s


def _attn_kernel(small_ref, lamv_ref, q_ref, k_ref, v_ref, sg_ref, o_ref,
                 m_ref, l_ref, acc_ref, p_ref, *, blk, lam_init):
    qi = pl.program_id(2)
    l_ref[...] = jnp.zeros(l_ref.shape, jnp.float32)
    acc_ref[...] = jnp.zeros(acc_ref.shape, jnp.float32)
    lanes = V7X_LANES

    def kv_slice(j):
        return pl.ds(pl.multiple_of(j * blk, blk), blk)

    def unshifted_step(j, masked):
        ks = kv_slice(j)
        for mi in range(2):
            p = jnp.exp2(_attn_scores(q_ref, k_ref, ks, mi, masked))
            ps = p[:, 0:lanes]
            for c in range(1, blk // lanes):
                ps = ps + p[:, c * lanes:(c + 1) * lanes]
            l_ref[mi] += ps
            p_ref[mi * blk:(mi + 1) * blk, :] = p.astype(p_ref.dtype)
        acc_ref[...] += jnp.dot(p_ref[...], v_ref[ks, :], preferred_element_type=jnp.float32)

    def shifted_step(j, masked):
        ks = kv_slice(j)
        vb = v_ref[ks, :]
        for mi in range(2):
            s = _attn_scores(q_ref, k_ref, ks, mi, masked)
            rows = slice(mi * blk, (mi + 1) * blk)
            m_old = m_ref[mi]
            m_new = jnp.maximum(m_old, jnp.max(s, axis=-1, keepdims=True))
            alpha = jnp.exp2(m_old - m_new)
            p = jnp.exp2(s - m_new)
            l_ref[mi, :, 0:1] = alpha * l_ref[mi, :, 0:1] + jnp.sum(p, axis=-1, keepdims=True)
            acc_ref[rows, :] = alpha * acc_ref[rows, :] + jnp.dot(
                p.astype(vb.dtype), vb, preferred_element_type=jnp.float32)
            m_ref[mi] = m_new

    def run(step):
        def body(j, carry):
            step(j, False)
            return carry
        lax.fori_loop(0, qi, body, 0)
        step(qi, True)

    unshifted = small_ref[0] == 1

    @pl.when(unshifted)
    def _():
        run(unshifted_step)

    @pl.when(jnp.logical_not(unshifted))
    def _():
        m_ref[...] = jnp.full(m_ref.shape, -jnp.inf, jnp.float32)
        run(shifted_step)

    lamv = lamv_ref[...]
    lam = (jnp.exp(jnp.sum(lamv[0:1] * lamv[1:2], axis=-1, keepdims=True))
           - jnp.exp(jnp.sum(lamv[2:3] * lamv[3:4], axis=-1, keepdims=True)) + lam_init)
    l1 = jnp.sum(l_ref[0], axis=-1, keepdims=True)
    l2 = jnp.sum(l_ref[1], axis=-1, keepdims=True)
    o = acc_ref[0:blk, :] / l1 - lam * (acc_ref[blk:2 * blk, :] / l2)
    o_ref[...] = (_rms_rows(o, sg_ref[...]) * (1.0 - lam_init)).astype(o_ref.dtype)


def _attention(proj, small_logits, lamv, sub_gain, *, batch, seq, lam_init, blk=512):
    nq = seq // blk
    hw = 2 * HEAD_DIM
    kern = functools.partial(_attn_kernel, blk=blk, lam_init=lam_init)
    vmem = _vmem_limit(4 * _nbytes((seq, hw), jnp.bfloat16), 4 * _nbytes((blk, hw), jnp.bfloat16),
                       2 * _nbytes((blk, hw), jnp.float32), 4 * _nbytes((blk, V7X_LANES), jnp.float32),
                       6 * _nbytes((blk, blk), jnp.float32))
    return pl.pallas_call(
        kern,
        out_shape=jax.ShapeDtypeStruct((batch * seq, D_MODEL), jnp.bfloat16),
        grid=(batch, ATTN_HEADS, nq),
        in_specs=[
            pl.BlockSpec(memory_space=pltpu.SMEM),
            pl.BlockSpec((4, HEAD_DIM), lambda b, h, i: (0, 0)),
            pl.BlockSpec((blk, hw), lambda b, h, i: (b * nq + i, h)),
            pl.BlockSpec((seq, hw), lambda b, h, i: (b, ATTN_HEADS + h)),
            pl.BlockSpec((seq, hw), lambda b, h, i: (b, 2 * ATTN_HEADS + h)),
            pl.BlockSpec((1, hw), lambda b, h, i: (0, 0)),
        ],
        out_specs=pl.BlockSpec((blk, hw), lambda b, h, i: (b * nq + i, h)),
        scratch_shapes=[
            pltpu.VMEM((2, blk, 1), jnp.float32),
            pltpu.VMEM((2, blk, V7X_LANES), jnp.float32),
            pltpu.VMEM((2 * blk, hw), jnp.float32),
            pltpu.VMEM((2 * blk, blk), jnp.bfloat16),
        ],
        compiler_params=pltpu.CompilerParams(
            dimension_semantics=("arbitrary", "arbitrary", "arbitrary"), vmem_limit_bytes=vmem),
        name="diff_attn",
    )(small_logits, lamv, proj, proj, proj, sub_gain)


def _cmul(ar, ai, br, bi):
    return ar * br - ai * bi, ar * bi + ai * br


def _ssm_kernel(u_ref, mixed_ref, bmat_ref, cmat_ref, pinv_ref, pfwd_ref, pcar_ref, d_ref,
                wglu_ref, bglu_ref, og_ref, o_ref, h_ref, y_ref, *, chunk):
    del mixed_ref
    c = pl.program_id(1)

    @pl.when(c == 0)
    def _():
        h_ref[...] = jnp.zeros(h_ref.shape, jnp.float32)

    n = SSM_CL_STATE
    row = lax.broadcasted_iota(jnp.int32, (chunk, chunk), 0)
    col = lax.broadcasted_iota(jnp.int32, (chunk, chunk), 1)
    tri = (col <= row).astype(jnp.bfloat16)

    for cl in range(SSM_CLUSTERS):
        ls = slice(cl * V7X_LANES, (cl + 1) * V7X_LANES)
        bu = jnp.dot(u_ref[:, ls], bmat_ref[cl], preferred_element_type=jnp.float32)
        xr, xi = _cmul(pinv_ref[cl, :, :n], pinv_ref[cl, :, n:], bu[:, :n], bu[:, n:])
        x = jnp.concatenate([xr, xi], axis=1).astype(jnp.bfloat16)
        cs = jnp.dot(tri, x, preferred_element_type=jnp.float32)
        hp = h_ref[cl:cl + 1, :]
        cr, ci = _cmul(pcar_ref[cl:cl + 1, :n], pcar_ref[cl:cl + 1, n:], hp[:, :n], hp[:, n:])
        hr, hi = _cmul(pfwd_ref[cl, :, :n], pfwd_ref[cl, :, n:], cs[:, :n] + cr, cs[:, n:] + ci)
        h = jnp.concatenate([hr, hi], axis=1)
        h_ref[cl:cl + 1, :] = h[chunk - 1:chunk, :]
        y_ref[:, ls] = jnp.dot(h.astype(jnp.bfloat16), cmat_ref[cl],
                               preferred_element_type=jnp.float32)

    y = y_ref[...] + d_ref[...] * u_ref[...].astype(jnp.float32)
    y = jax.nn.gelu(y, approximate=True)
    z = jnp.dot(y.astype(jnp.bfloat16), wglu_ref[...], preferred_element_type=jnp.float32)
    z = z + bglu_ref[...]
    y = y * (1.0 / (1.0 + jnp.exp(-z)))
    o_ref[...] = _rms_rows(y, og_ref[...]).astype(o_ref.dtype)


def _ssm_tables(log_dt, a_re, a_im, b_re, b_im, c_re, c_im, chunk):
    f32 = jnp.float32
    g, n, c = SSM_GROUPS, SSM_STATE, SSM_GROUP
    dt = jnp.exp(log_dt.astype(f32))[:, None]
    ar, ai = a_re.astype(f32), a_im.astype(f32)
    mag = jnp.exp(ar * dt)
    lb_re = mag * jnp.cos(ai * dt)
    lb_im = mag * jnp.sin(ai * dt)
    pr, pi_ = lb_re - 1.0, lb_im
    den = ar * ar + ai * ai
    z_re = ((pr * ar + pi_ * ai) / den)[..., None]
    z_im = ((pi_ * ar - pr * ai) / den)[..., None]
    br, bi = b_re.astype(f32), b_im.astype(f32)
    bb_re = z_re * br - z_im * bi
    bb_im = z_re * bi + z_im * br

    def powers(k):
        kk = k.astype(f32)[:, None, None]
        pm = jnp.exp(kk * (ar * dt)[None])
        ph = kk * (ai * dt)[None]
        return pm * jnp.cos(ph), pm * jnp.sin(ph)

    def cl_cols(re, im):
        k = re.shape[0]
        re = re.reshape(k, SSM_CLUSTERS, SSM_CL_STATE)
        im = im.reshape(k, SSM_CLUSTERS, SSM_CL_STATE)
        return jnp.concatenate([re, im], axis=-1).transpose(1, 0, 2)

    c0 = chunk // 2
    t = jnp.arange(chunk)
    pinv = cl_cols(*powers(c0 - t))
    pfwd = cl_cols(*powers(t - c0))
    pcar = cl_cols(*powers(jnp.array([c0 + 1])))[:, 0, :]

    eye = jnp.eye(SSM_CLUSTER, dtype=f32)
    def bpart(bb):
        b4 = bb.reshape(SSM_CLUSTERS, SSM_CLUSTER, n, c)
        return jnp.einsum('kgnc,gh->kgchn', b4, eye).reshape(SSM_CLUSTERS, V7X_LANES, SSM_CL_STATE)
    bmat = jnp.concatenate([bpart(bb_re), bpart(bb_im)], axis=-1).astype(jnp.bfloat16)
    def cpart(cc):
        c4 = cc.astype(f32).reshape(SSM_CLUSTERS, SSM_CLUSTER, c, n)
        return jnp.einsum('kgcn,gh->kgnhc', c4, eye).reshape(SSM_CLUSTERS, SSM_CL_STATE, V7X_LANES)
    cmat = jnp.concatenate([cpart(c_re), -cpart(c_im)], axis=1).astype(jnp.bfloat16)
    return bmat, cmat, pinv, pfwd, pcar


def _ssm(proj, mixed, tables, d_skip, w_glu, b_glu, out_gain, *, batch, seq, chunk=SSM_CHUNK):
    bmat, cmat, pinv, pfwd, pcar = tables
    nc = seq // chunk
    u_col = 3 * ATTN_WIDTH // SSM_WIDTH
    o_col = ATTN_WIDTH // SSM_WIDTH
    kern = functools.partial(_ssm_kernel, chunk=chunk)
    const3 = lambda b, c: (0, 0, 0)
    const2 = lambda b, c: (0, 0)
    vmem = _vmem_limit(2 * (bmat.nbytes + cmat.nbytes + pinv.nbytes + pfwd.nbytes + pcar.nbytes),
                       4 * _nbytes((SSM_WIDTH, SSM_WIDTH), jnp.bfloat16),
                       8 * _nbytes((chunk, SSM_WIDTH), jnp.float32))
    return pl.pallas_call(
        kern,
        out_shape=jax.ShapeDtypeStruct(mixed.shape, mixed.dtype),
        grid=(batch, nc),
        in_specs=[
            pl.BlockSpec((chunk, SSM_WIDTH), lambda b, c: (b * nc + c, u_col)),
            pl.BlockSpec(memory_space=pl.ANY),
            pl.BlockSpec(bmat.shape, const3),
            pl.BlockSpec(cmat.shape, const3),
            pl.BlockSpec(pinv.shape, const3),
            pl.BlockSpec(pfwd.shape, const3),
            pl.BlockSpec(pcar.shape, const2),
            pl.BlockSpec((1, SSM_WIDTH), const2),
            pl.BlockSpec((SSM_WIDTH, SSM_WIDTH), const2),
            pl.BlockSpec((1, SSM_WIDTH), const2),
            pl.BlockSpec((1, SSM_WIDTH), const2),
        ],
        out_specs=pl.BlockSpec((chunk, SSM_WIDTH), lambda b, c: (b * nc + c, o_col)),
        scratch_shapes=[
            pltpu.VMEM((SSM_CLUSTERS, 2 * SSM_CL_STATE), jnp.float32),
            pltpu.VMEM((chunk, SSM_WIDTH), jnp.float32),
        ],
        input_output_aliases={1: 0},
        compiler_params=pltpu.CompilerParams(
            dimension_semantics=("arbitrary", "arbitrary"), vmem_limit_bytes=vmem),
        name="s5_ssm",
    )(proj, mixed, bmat, cmat, pinv, pfwd, pcar, d_skip, w_glu, b_glu, out_gain)


def _res_matmul_kernel(a_ref, w_ref, r_ref, o_ref):
    o_ref[...] = r_ref[...] + jnp.dot(a_ref[...], w_ref[...], preferred_element_type=jnp.float32)


def _res_matmul(a, w, res, *, tm, tn, name):
    m, k = a.shape
    n = w.shape[1]
    assert m % tm == 0 and n % tn == 0
    vmem = _vmem_limit(2 * _nbytes((tm, k), a.dtype), 2 * _nbytes((k, tn), w.dtype),
                       5 * _nbytes((tm, tn), jnp.float32))
    return pl.pallas_call(
        _res_matmul_kernel,
        out_shape=jax.ShapeDtypeStruct((m, n), jnp.float32),
        grid=(m // tm, n // tn),
        in_specs=[
            pl.BlockSpec((tm, k), lambda i, j: (i, 0)),
            pl.BlockSpec((k, tn), lambda i, j: (0, j)),
            pl.BlockSpec((tm, tn), lambda i, j: (i, j)),
        ],
        out_specs=pl.BlockSpec((tm, tn), lambda i, j: (i, j)),
        compiler_params=pltpu.CompilerParams(
            dimension_semantics=("parallel", "parallel"), vmem_limit_bytes=vmem),
        name=name,
    )(a, w, res)


def _ffn_up_kernel(x_ref, halo_ref, g_ref, wg_ref, wu_ref, cw_ref, cb_ref, o_ref, h_ref, gs_ref,
                   *, tm, blocks_per_seq):
    i = pl.program_id(0)
    j = pl.program_id(1)
    halo = V7X_BF16_ROWS_PER_TILE

    @pl.when(j == 0)
    def _():
        hh = _rms_rows(halo_ref[...], g_ref[...])
        hh = jnp.where(i % blocks_per_seq == 0, 0.0, hh)
        h_ref[0:halo, :] = hh.astype(h_ref.dtype)
        _norm_into(x_ref, g_ref, h_ref, rows=tm, row_chunk=32, h_offset=halo)

    gs_ref[...] = jnp.dot(h_ref[...], wg_ref[...], preferred_element_type=jnp.float32)
    up = jnp.dot(h_ref[halo:, :], wu_ref[...], preferred_element_type=jnp.float32)
    gc = (cb_ref[...]
          + cw_ref[0:1, :] * gs_ref[pl.ds(halo - 2, tm), :]
          + cw_ref[1:2, :] * gs_ref[pl.ds(halo - 1, tm), :]
          + cw_ref[2:3, :] * gs_ref[pl.ds(halo, tm), :])
    act = gc * (1.0 / (1.0 + jnp.exp(-gc))) * up
    o_ref[...] = act.astype(o_ref.dtype)


def _ffn_up(x1, gain, wg, wu, conv_w, conv_b, *, seq, tm=512, tf=F_TILE):
    m, k = x1.shape
    f = wg.shape[1]
    halo = V7X_BF16_ROWS_PER_TILE
    assert m % tm == 0 and seq % tm == 0
    kern = functools.partial(_ffn_up_kernel, tm=tm, blocks_per_seq=seq // tm)
    vmem = _vmem_limit(2 * _nbytes((tm, k), jnp.float32), _nbytes((tm + halo, k), jnp.bfloat16),
                       4 * _nbytes((k, tf), jnp.bfloat16), 2 * _nbytes((tm, tf), jnp.bfloat16),
                       4 * _nbytes((tm + halo, tf), jnp.float32))
    return pl.pallas_call(
        kern,
        out_shape=jax.ShapeDtypeStruct((m, f), jnp.bfloat16),
        grid=(m // tm, pl.cdiv(f, tf)),
        in_specs=[
            pl.BlockSpec((tm, k), lambda i, j: (i, 0)),
            pl.BlockSpec((halo, k), lambda i, j: (jnp.maximum(i * (tm // halo) - 1, 0), 0)),
            pl.BlockSpec((1, k), lambda i, j: (0, 0)),
            pl.BlockSpec((k, tf), lambda i, j: (0, j)),
            pl.BlockSpec((k, tf), lambda i, j: (0, j)),
            pl.BlockSpec((CONV_WIDTH, tf), lambda i, j: (0, j)),
            pl.BlockSpec((1, tf), lambda i, j: (0, j)),
        ],
        out_specs=pl.BlockSpec((tm, tf), lambda i, j: (i, j)),
        scratch_shapes=[
            pltpu.VMEM((tm + halo, k), jnp.bfloat16),
            pltpu.VMEM((tm + halo, tf), jnp.float32),
        ],
        compiler_params=pltpu.CompilerParams(
            dimension_semantics=("arbitrary", "arbitrary"), vmem_limit_bytes=vmem),
        name="ffn_up",
    )(x1, x1, gain, wg, wu, conv_w, conv_b)


def kernel(x, attn_norm, w_in, q_gain, k_gain, lam_q1, lam_k1, lam_q2, lam_k2, sub_gain, ssm_log_dt, ssm_a_re, ssm_a_im, ssm_b_re, ssm_b_im, ssm_c_re, ssm_c_im, ssm_d, ssm_w_glu, ssm_b_glu, ssm_out_gain, w_out, ffn_norm, w_gate, w_up, conv_w, conv_b, w_down):
    b, s, d = x.shape
    depth = w_in.shape[0]
    bf16, f32 = jnp.bfloat16, jnp.float32
    xf = x.reshape(b * s, d)
    for l in range(depth):
        lam_init = 0.8 - 0.6 * math.exp(-0.3 * l)
        q_scale = HEAD_DIM ** -0.5 * math.log2(math.e)
        qg, kg = q_gain[l].astype(f32), k_gain[l].astype(f32)
        col_gain = jnp.concatenate([
            jnp.tile(qg * q_scale, ATTN_WIDTH // HEAD_DIM),
            jnp.tile(kg, ATTN_WIDTH // HEAD_DIM),
            jnp.ones((IN_WIDTH - 2 * ATTN_WIDTH,), f32)])[None, :]
        proj = _in_proj(xf, attn_norm[l].astype(f32)[None, :], w_in[l].astype(bf16), col_gain)

        logit_bound = 1.01 * HEAD_DIM * q_scale * jnp.max(jnp.abs(qg)) * jnp.max(jnp.abs(kg))
        small_logits = (logit_bound <= _UNSHIFTED_SOFTMAX_MAX_LOG2).astype(jnp.int32).reshape(1)
        lamv = jnp.stack([lam_q1[l], lam_k1[l], lam_q2[l], lam_k2[l]]).astype(f32)
        mixed = _attention(proj, small_logits, lamv, sub_gain[l].astype(f32)[None, :],
                           batch=b, seq=s, lam_init=lam_init)

        tables = _ssm_tables(ssm_log_dt[l], ssm_a_re[l], ssm_a_im[l], ssm_b_re[l], ssm_b_im[l],
                             ssm_c_re[l], ssm_c_im[l], SSM_CHUNK)
        mixed = _ssm(proj, mixed, tables, ssm_d[l].astype(f32).reshape(1, SSM_WIDTH),
                     ssm_w_glu[l].astype(bf16), ssm_b_glu[l].astype(f32)[None, :],
                     ssm_out_gain[l].astype(f32)[None, :], batch=b, seq=s)

        x1 = _res_matmul(mixed, w_out[l].astype(bf16), xf, tm=512, tn=1024, name="out_proj")

        act = _ffn_up(x1, ffn_norm[l].astype(f32)[None, :], w_gate[l].astype(bf16),
                      w_up[l].astype(bf16), conv_w[l].astype(f32), conv_b[l].astype(f32)[None, :],
                      seq=s)
        xf = _res_matmul(act, w_down[l].astype(bf16), x1, tm=512, tn=512, name="ffn_down")
    return xf.reshape(b, s, d)
```

```python
import functools
import math

import jax
import jax.numpy as jnp
from jax import lax
from jax.experimental import pallas as pl
from jax.experimental.pallas import tpu as pltpu

D_MODEL = 4096
ATTN_HEADS = 12
HEAD_DIM = 128
ATTN_WIDTH = ATTN_HEADS * 2 * HEAD_DIM
SSM_WIDTH = D_MODEL - ATTN_WIDTH
SSM_GROUP = 16
SSM_GROUPS = SSM_WIDTH // SSM_GROUP
SSM_STATE = 64
IN_WIDTH = 3 * ATTN_WIDTH + SSM_WIDTH
D_FF = 11008
CONV_WIDTH = 3
EPS = 1e-6

V7X_LANES = 128
V7X_SUBLANES = 8
V7X_BF16_ROWS_PER_TILE = 2 * V7X_SUBLANES
V7X_VMEM_BYTES = 64 * 1024 * 1024
V7X_VMEM_RESERVED_BYTES = 6 * 1024 * 1024

F_TILE = 512
SSM_CLUSTER = V7X_LANES // SSM_GROUP
SSM_CLUSTERS = SSM_GROUPS // SSM_CLUSTER
SSM_CL_STATE = SSM_CLUSTER * SSM_STATE
SSM_CHUNK = 128


def _vmem_limit(*nbytes):
    want = int(sum(nbytes)) + 12 * 1024 * 1024
    return min(want, V7X_VMEM_BYTES - V7X_VMEM_RESERVED_BYTES)


def _nbytes(shape, dtype):
    return math.prod(shape) * jnp.dtype(dtype).itemsize


def _rms_rows(x, gain):
    ms = jnp.mean(x * x, axis=-1, keepdims=True)
    return x * lax.rsqrt(ms + EPS) * gain


def _norm_into(x_ref, gain_ref, h_ref, *, rows, row_chunk, h_offset):
    def body(c, carry):
        r = pl.multiple_of(c * row_chunk, row_chunk)
        xb = x_ref[pl.ds(r, row_chunk), :]
        h_ref[pl.ds(h_offset + r, row_chunk), :] = _rms_rows(xb, gain_ref[...]).astype(h_ref.dtype)
        return carry
    lax.fori_loop(0, rows // row_chunk, body, 0)


def _lagged_steps(j, n, prologue, matmul, epilogue):
    @pl.when(j == 0)
    def _():
        prologue()
        matmul(0)

    for parity in range(2):
        @pl.when(jnp.logical_and(jnp.logical_and(j > 0, j < n), j % 2 == parity))
        def _():
            epilogue(1 - parity)
            matmul(parity)

    @pl.when(j == n)
    def _():
        epilogue((n - 1) % 2)


def _in_proj_kernel(x_ref, g_ref, w_ref, cg_ref, o_ref, h_ref, acc_ref, *, tm, tn, n_tiles,
                    n_norm_tiles):
    j = pl.program_id(1)

    def prologue():
        _norm_into(x_ref, g_ref, h_ref, rows=tm, row_chunk=32, h_offset=0)

    def matmul(buf):
        acc_ref[buf] = jnp.dot(h_ref[...], w_ref[...], preferred_element_type=jnp.float32)

    def epilogue(buf):
        is_qk = (j - 1) < n_norm_tiles
        for s in range(tn // HEAD_DIM):
            sl = slice(s * HEAD_DIM, (s + 1) * HEAD_DIM)
            raw = acc_ref[buf, :, sl]
            o_ref[:, sl] = jnp.where(is_qk, _rms_rows(raw, cg_ref[:, sl]), raw).astype(o_ref.dtype)

    _lagged_steps(j, n_tiles, prologue, matmul, epilogue)


def _in_proj(x2d, gain, w_bf16, col_gain, *, tm=512, tn=1024):
    m, k = x2d.shape
    n = w_bf16.shape[1]
    n_tiles = n // tn
    n_norm_tiles = 2 * ATTN_WIDTH // tn
    assert 2 * ATTN_WIDTH % tn == 0 and m % tm == 0 and n % tn == 0
    kern = functools.partial(_in_proj_kernel, tm=tm, tn=tn, n_tiles=n_tiles,
                             n_norm_tiles=n_norm_tiles)
    vmem = _vmem_limit(2 * _nbytes((tm, k), jnp.float32), _nbytes((tm, k), jnp.bfloat16),
                       2 * _nbytes((k, tn), jnp.bfloat16), 2 * _nbytes((tm, tn), jnp.bfloat16),
                       2 * _nbytes((tm, tn), jnp.float32))
    mm_tile = lambda i, j: (0, jnp.minimum(j, n_tiles - 1))
    ep_tile = lambda i, j: (0, jnp.maximum(j - 1, 0))
    return pl.pallas_call(
        kern,
        out_shape=jax.ShapeDtypeStruct((m, n), jnp.bfloat16),
        grid=(m // tm, n_tiles + 1),
        in_specs=[
            pl.BlockSpec((tm, k), lambda i, j: (i, 0)),
            pl.BlockSpec((1, k), lambda i, j: (0, 0)),
            pl.BlockSpec((k, tn), mm_tile),
            pl.BlockSpec((1, tn), ep_tile),
        ],
        out_specs=pl.BlockSpec((tm, tn), lambda i, j: (i, jnp.maximum(j - 1, 0))),
        scratch_shapes=[pltpu.VMEM((tm, k), jnp.bfloat16), pltpu.VMEM((2, tm, tn), jnp.float32)],
        compiler_params=pltpu.CompilerParams(
            dimension_semantics=("arbitrary", "arbitrary"), vmem_limit_bytes=vmem),
        name="in_proj",
    )(x2d, gain, w_bf16, col_gain)


_UNSHIFTED_SOFTMAX_MAX_LOG2 = 60.0


def _attn_scores(q_ref, k_ref, ks, mi, masked):
    hs = slice(mi * HEAD_DIM, (mi + 1) * HEAD_DIM)
    s = lax.dot_general(q_ref[:, hs], k_ref[ks, hs], (((1,), (1,)), ((), ())),
                        preferred_element_type=jnp.float32)
    if masked:
        row = lax.broadcasted_iota(jnp.int32, s.shape, 0)
        col = lax.broadcasted_iota(jnp.int32, s.shape, 1)
        s = jnp.where(col <= row, s, -jnp.inf)
    return s


def _attn_kernel(small_ref, lamv_ref, q_ref, k_ref, v_ref, sg_ref, o_ref,
                 m_ref, l_ref, acc_ref, p_ref, *, blk, lam_init):
    qi = pl.program_id(2)
    l_ref[...] = jnp.zeros(l_ref.shape, jnp.float32)
    acc_ref[...] = jnp.zeros(acc_ref.shape, jnp.float32)
    lanes = V7X_LANES

    def kv_slice(j):
        return pl.ds(pl.multiple_of(j * blk, blk), blk)

    def unshifted_step(j, masked):
        ks = kv_slice(j)
        for mi in range(2):
            p = jnp.exp2(_attn_scores(q_ref, k_ref, ks, mi, masked))
            ps = p[:, 0:lanes]
            for c in range(1, blk // lanes):
                ps = ps + p[:, c * lanes:(c + 1) * lanes]
            l_ref[mi] += ps
            p_ref[mi * blk:(mi + 1) * blk, :] = p.astype(p_ref.dtype)
        acc_ref[...] += jnp.dot(p_ref[...], v_ref[ks, :], preferred_element_type=jnp.float32)

    def shifted_step(j, masked):
        ks = kv_slice(j)
        vb = v_ref[ks, :]
        for mi in range(2):
            s = _attn_scores(q_ref, k_ref, ks, mi, masked)
            rows = slice(mi * blk, (mi + 1) * blk)
            m_old = m_ref[mi]
            m_new = jnp.maximum(m_old, jnp.max(s, axis=-1, keepdims=True))
            alpha = jnp.exp2(m_old - m_new)
            p = jnp.exp2(s - m_new)
            l_ref[mi, :, 0:1] = alpha * l_ref[mi, :, 0:1] + jnp.sum(p, axis=-1, keepdims=True)
            acc_ref[rows, :] = alpha * acc_ref[rows, :] + jnp.dot(
                p.astype(vb.dtype), vb, preferred_element_type=jnp.float32)
            m_ref[mi] = m_new

    def run(step):
        def body(j, carry):
            step(j, False)
            return carry
        lax.fori_loop(0, qi, body, 0)
        step(qi, True)

    unshifted = small_ref[0] == 1

    @pl.when(unshifted)
    def _():
        run(unshifted_step)

    @pl.when(jnp.logical_not(unshifted))
    def _():
        m_ref[...] = jnp.full(m_ref.shape, -jnp.inf, jnp.float32)
        run(shifted_step)

    lamv = lamv_ref[...]
    lam = (jnp.exp(jnp.sum(lamv[0:1] * lamv[1:2], axis=-1, keepdims=True))
           - jnp.exp(jnp.sum(lamv[2:3] * lamv[3:4], axis=-1, keepdims=True)) + lam_init)
    l1 = jnp.sum(l_ref[0], axis=-1, keepdims=True)
    l2 = jnp.sum(l_ref[1], axis=-1, keepdims=True)
    o = acc_ref[0:blk, :] / l1 - lam * (acc_ref[blk:2 * blk, :] / l2)
    o_ref[...] = (_rms_rows(o, sg_ref[...]) * (1.0 - lam_init)).astype(o_ref.dtype)


def _attention(proj, small_logits, lamv, sub_gain, *, batch, seq, lam_init, blk=512):
    nq = seq // blk
    hw = 2 * HEAD_DIM
    kern = functools.partial(_attn_kernel, blk=blk, lam_init=lam_init)
    vmem = _vmem_limit(4 * _nbytes((seq, hw), jnp.bfloat16), 4 * _nbytes((blk, hw), jnp.bfloat16),
                       2 * _nbytes((blk, hw), jnp.float32), 4 * _nbytes((blk, V7X_LANES), jnp.float32),
                       6 * _nbytes((blk, blk), jnp.float32))
    return pl.pallas_call(
        kern,
        out_shape=jax.ShapeDtypeStruct((batch * seq, D_MODEL), jnp.bfloat16),
        grid=(batch, ATTN_HEADS, nq),
        in_specs=[
            pl.BlockSpec(memory_space=pltpu.SMEM),
            pl.BlockSpec((4, HEAD_DIM), lambda b, h, i: (0, 0)),
            pl.BlockSpec((blk, hw), lambda b, h, i: (b * nq + i, h)),
            pl.BlockSpec((seq, hw), lambda b, h, i: (b, ATTN_HEADS + h)),
            pl.BlockSpec((seq, hw), lambda b, h, i: (b, 2 * ATTN_HEADS + h)),
            pl.BlockSpec((1, hw), lambda b, h, i: (0, 0)),
        ],
        out_specs=pl.BlockSpec((blk, hw), lambda b, h, i: (b * nq + i, h)),
        scratch_shapes=[
            pltpu.VMEM((2, blk, 1), jnp.float32),
            pltpu.VMEM((2, blk, V7X_LANES), jnp.float32),
            pltpu.VMEM((2 * blk, hw), jnp.float32),
            pltpu.VMEM((2 * blk, blk), jnp.bfloat16),
        ],
        compiler_params=pltpu.CompilerParams(
            dimension_semantics=("arbitrary", "arbitrary", "arbitrary"), vmem_limit_bytes=vmem),
        name="diff_attn",
    )(small_logits, lamv, proj, proj, proj, sub_gain)


def _cmul(ar, ai, br, bi):
    return ar * br - ai * bi, ar * bi + ai * br


def _ssm_kernel(u_ref, mixed_ref, bmat_ref, cmat_ref, pinv_ref, pfwd_ref, pcar_ref, d_ref,
                wglu_ref, bglu_ref, og_ref, o_ref, h_ref, y_ref, *, chunk):
    del mixed_ref
    c = pl.program_id(1)

    @pl.when(c == 0)
    def _():
        h_ref[...] = jnp.zeros(h_ref.shape, jnp.float32)

    n = SSM_CL_STATE
    row = lax.broadcasted_iota(jnp.int32, (chunk, chunk), 0)
    col = lax.broadcasted_iota(jnp.int32, (chunk, chunk), 1)
    tri = (col <= row).astype(jnp.bfloat16)

    for cl in range(SSM_CLUSTERS):
        ls = slice(cl * V7X_LANES, (cl + 1) * V7X_LANES)
        bu = jnp.dot(u_ref[:, ls], bmat_ref[cl], preferred_element_type=jnp.float32)
        xr, xi = _cmul(pinv_ref[cl, :, :n], pinv_ref[cl, :, n:], bu[:, :n], bu[:, n:])
        x = jnp.concatenate([xr, xi], axis=1).astype(jnp.bfloat16)
        cs = jnp.dot(tri, x, preferred_element_type=jnp.float32)
        hp = h_ref[cl:cl + 1, :]
        cr, ci = _cmul(pcar_ref[cl:cl + 1, :n], pcar_ref[cl:cl + 1, n:], hp[:, :n], hp[:, n:])
        hr, hi = _cmul(pfwd_ref[cl, :, :n], pfwd_ref[cl, :, n:], cs[:, :n] + cr, cs[:, n:] + ci)
        h = jnp.concatenate([hr, hi], axis=1)
        h_ref[cl:cl + 1, :] = h[chunk - 1:chunk, :]
        y_ref[:, ls] = jnp.dot(h.astype(jnp.bfloat16), cmat_ref[cl],
                               preferred_element_type=jnp.float32)

    y = y_ref[...] + d_ref[...] * u_ref[...].astype(jnp.float32)
    y = jax.nn.gelu(y, approximate=True)
    z = jnp.dot(y.astype(jnp.bfloat16), wglu_ref[...], preferred_element_type=jnp.float32)
    z = z + bglu_ref[...]
    y = y * (1.0 / (1.0 + jnp.exp(-z)))
    o_ref[...] = _rms_rows(y, og_ref[...]).astype(o_ref.dtype)


def _ssm_tables(log_dt, a_re, a_im, b_re, b_im, c_re, c_im, chunk):
    f32 = jnp.float32
    g, n, c = SSM_GROUPS, SSM_STATE, SSM_GROUP
    dt = jnp.exp(log_dt.astype(f32))[:, None]
    ar, ai = a_re.astype(f32), a_im.astype(f32)
    mag = jnp.exp(ar * dt)
    lb_re = mag * jnp.cos(ai * dt)
    lb_im = mag * jnp.sin(ai * dt)
    pr, pi_ = lb_re - 1.0, lb_im
    den = ar * ar + ai * ai
    z_re = ((pr * ar + pi_ * ai) / den)[..., None]
    z_im = ((pi_ * ar - pr * ai) / den)[..., None]
    br, bi = b_re.astype(f32), b_im.astype(f32)
    bb_re = z_re * br - z_im * bi
    bb_im = z_re * bi + z_im * br

    def powers(k):
        kk = k.astype(f32)[:, None, None]
        pm = jnp.exp(kk * (ar * dt)[None])
        ph = kk * (ai * dt)[None]
        return pm * jnp.cos(ph), pm * jnp.sin(ph)

    def cl_cols(re, im):
        k = re.shape[0]
        re = re.reshape(k, SSM_CLUSTERS, SSM_CL_STATE)
        im = im.reshape(k, SSM_CLUSTERS, SSM_CL_STATE)
        return jnp.concatenate([re, im], axis=-1).transpose(1, 0, 2)

    c0 = chunk // 2
    t = jnp.arange(chunk)
    pinv = cl_cols(*powers(c0 - t))
    pfwd = cl_cols(*powers(t - c0))
    pcar = cl_cols(*powers(jnp.array([c0 + 1])))[:, 0, :]

    eye = jnp.eye(SSM_CLUSTER, dtype=f32)
    def bpart(bb):
        b4 = bb.reshape(SSM_CLUSTERS, SSM_CLUSTER, n, c)
        return jnp.einsum('kgnc,gh->kgchn', b4, eye).reshape(SSM_CLUSTERS, V7X_LANES, SSM_CL_STATE)
    bmat = jnp.concatenate([bpart(bb_re), bpart(bb_im)], axis=-1).astype(jnp.bfloat16)
    def cpart(cc):
        c4 = cc.astype(f32).reshape(SSM_CLUSTERS, SSM_CLUSTER, c, n)
        return jnp.einsum('kgcn,gh->kgnhc', c4, eye).reshape(SSM_CLUSTERS, SSM_CL_STATE, V7X_LANES)
    cmat = jnp.concatenate([cpart(c_re), -cpart(c_im)], axis=1).astype(jnp.bfloat16)
    return bmat, cmat, pinv, pfwd, pcar


def _ssm(proj, mixed, tables, d_skip, w_glu, b_glu, out_gain, *, batch, seq, chunk=SSM_CHUNK):
    bmat, cmat, pinv, pfwd, pcar = tables
    nc = seq // chunk
    u_col = 3 * ATTN_WIDTH // SSM_WIDTH
    o_col = ATTN_WIDTH // SSM_WIDTH
    kern = functools.partial(_ssm_kernel, chunk=chunk)
    const3 = lambda b, c: (0, 0, 0)
    const2 = lambda b, c: (0, 0)
    vmem = _vmem_limit(2 * (bmat.nbytes + cmat.nbytes + pinv.nbytes + pfwd.nbytes + pcar.nbytes),
                       4 * _nbytes((SSM_WIDTH, SSM_WIDTH), jnp.bfloat16),
                       8 * _nbytes((chunk, SSM_WIDTH), jnp.float32))
    return pl.pallas_call(
        kern,
        out_shape=jax.ShapeDtypeStruct(mixed.shape, mixed.dtype),
        grid=(batch, nc),
        in_specs=[
            pl.BlockSpec((chunk, SSM_WIDTH), lambda b, c: (b * nc + c, u_col)),
            pl.BlockSpec(memory_space=pl.ANY),
            pl.BlockSpec(bmat.shape, const3),
            pl.BlockSpec(cmat.shape, const3),
            pl.BlockSpec(pinv.shape, const3),
            pl.BlockSpec(pfwd.shape, const3),
            pl.BlockSpec(pcar.shape, const2),
            pl.BlockSpec((1, SSM_WIDTH), const2),
            pl.BlockSpec((SSM_WIDTH, SSM_WIDTH), const2),
            pl.BlockSpec((1, SSM_WIDTH), const2),
            pl.BlockSpec((1, SSM_WIDTH), const2),
        ],
        out_specs=pl.BlockSpec((chunk, SSM_WIDTH), lambda b, c: (b * nc + c, o_col)),
        scratch_shapes=[
            pltpu.VMEM((SSM_CLUSTERS, 2 * SSM_CL_STATE), jnp.float32),
            pltpu.VMEM((chunk, SSM_WIDTH), jnp.float32),
        ],
        input_output_aliases={1: 0},
        compiler_params=pltpu.CompilerParams(
            dimension_semantics=("arbitrary", "arbitrary"), vmem_limit_bytes=vmem),
        name="s5_ssm",
    )(proj, mixed, bmat, cmat, pinv, pfwd, pcar, d_skip, w_glu, b_glu, out_gain)


def _res_matmul_kernel(a_ref, w_ref, r_ref, o_ref):
    o_ref[...] = r_ref[...] + jnp.dot(a_ref[...], w_ref[...], preferred_element_type=jnp.float32)


def _res_matmul(a, w, res, *, tm, tn, name):
    m, k = a.shape
    n = w.shape[1]
    assert m % tm == 0 and n % tn == 0
    vmem = _vmem_limit(2 * _nbytes((tm, k), a.dtype), 2 * _nbytes((k, tn), w.dtype),
                       5 * _nbytes((tm, tn), jnp.float32))
    return pl.pallas_call(
        _res_matmul_kernel,
        out_shape=jax.ShapeDtypeStruct((m, n), jnp.float32),
        grid=(m // tm, n // tn),
        in_specs=[
            pl.BlockSpec((tm, k), lambda i, j: (i, 0)),
            pl.BlockSpec((k, tn), lambda i, j: (0, j)),
            pl.BlockSpec((tm, tn), lambda i, j: (i, j)),
        ],
        out_specs=pl.BlockSpec((tm, tn), lambda i, j: (i, j)),
        compiler_params=pltpu.CompilerParams(
            dimension_semantics=("parallel", "parallel"), vmem_limit_bytes=vmem),
        name=name,
    )(a, w, res)


def _ffn_up_kernel(x_ref, halo_ref, g_ref, wg_ref, wu_ref, cw_ref, cb_ref, o_ref,
                   h_ref, gs_ref, up_ref, *, tm, n_tiles, blocks_per_seq):
    i = pl.program_id(0)
    j = pl.program_id(1)
    halo = V7X_BF16_ROWS_PER_TILE

    def prologue():
        hh = _rms_rows(halo_ref[...], g_ref[...])
        hh = jnp.where(i % blocks_per_seq == 0, 0.0, hh)
        h_ref[0:halo, :] = hh.astype(h_ref.dtype)
        _norm_into(x_ref, g_ref, h_ref, rows=tm, row_chunk=32, h_offset=halo)

    def matmul(buf):
        gs_ref[buf] = jnp.dot(h_ref[...], wg_ref[...], preferred_element_type=jnp.float32)
        up_ref[buf] = jnp.dot(h_ref[halo:, :], wu_ref[...], preferred_element_type=jnp.float32)

    def epilogue(buf):
        rows = 128
        for r0 in range(0, tm, rows):
            for c0 in range(0, o_ref.shape[1], V7X_LANES):
                cs = slice(c0, c0 + V7X_LANES)
                g = gs_ref[buf, r0:r0 + halo + rows, cs]
                t = cw_ref[1:2, cs] * g + pltpu.roll(cw_ref[0:1, cs] * g, 1, 0)
                gc = (cb_ref[:, cs] + cw_ref[2:3, cs] * g + pltpu.roll(t, 1, 0))[halo:, :]
                act = gc * (1.0 / (1.0 + jnp.exp(-gc))) * up_ref[buf, r0:r0 + rows, cs]
                o_ref[r0:r0 + rows, cs] = act.astype(o_ref.dtype)

    _lagged_steps(j, n_tiles, prologue, matmul, epilogue)


def _ffn_up(x1, gain, wg, wu, conv_w, conv_b, *, seq, tm=512, tf=F_TILE):
    m, k = x1.shape
    f = wg.shape[1]
    halo = V7X_BF16_ROWS_PER_TILE
    n_tiles = pl.cdiv(f, tf)
    assert m % tm == 0 and seq % tm == 0
    kern = functools.partial(_ffn_up_kernel, tm=tm, n_tiles=n_tiles, blocks_per_seq=seq // tm)
    vmem = _vmem_limit(2 * _nbytes((tm, k), jnp.float32), _nbytes((tm + halo, k), jnp.bfloat16),
                       4 * _nbytes((k, tf), jnp.bfloat16), 2 * _nbytes((tm, tf), jnp.bfloat16),
                       4 * _nbytes((tm + halo, tf), jnp.float32))
    mm_tile = lambda i, j: (0, jnp.minimum(j, n_tiles - 1))
    ep_tile = lambda i, j: (0, jnp.maximum(j - 1, 0))
    return pl.pallas_call(
        kern,
        out_shape=jax.ShapeDtypeStruct((m, f), jnp.bfloat16),
        grid=(m // tm, n_tiles + 1),
        in_specs=[
            pl.BlockSpec((tm, k), lambda i, j: (i, 0)),
            pl.BlockSpec((halo, k), lambda i, j: (jnp.maximum(i * (tm // halo) - 1, 0), 0)),
            pl.BlockSpec((1, k), lambda i, j: (0, 0)),
            pl.BlockSpec((k, tf), mm_tile),
            pl.BlockSpec((k, tf), mm_tile),
            pl.BlockSpec((CONV_WIDTH, tf), ep_tile),
            pl.BlockSpec((1, tf), ep_tile),
        ],
        out_specs=pl.BlockSpec((tm, tf), lambda i, j: (i, jnp.maximum(j - 1, 0))),
        scratch_shapes=[
            pltpu.VMEM((tm + halo, k), jnp.bfloat16),
            pltpu.VMEM((2, tm + halo, tf), jnp.float32),
            pltpu.VMEM((2, tm, tf), jnp.float32),
        ],
        compiler_params=pltpu.CompilerParams(
            dimension_semantics=("arbitrary", "arbitrary"), vmem_limit_bytes=vmem),
        name="ffn_up",
    )(x1, x1, gain, wg, wu, conv_w, conv_b)


def kernel(x, attn_norm, w_in, q_gain, k_gain, lam_q1, lam_k1, lam_q2, lam_k2, sub_gain, ssm_log_dt, ssm_a_re, ssm_a_im, ssm_b_re, ssm_b_im, ssm_c_re, ssm_c_im, ssm_d, ssm_w_glu, ssm_b_glu, ssm_out_gain, w_out, ffn_norm, w_gate, w_up, conv_w, conv_b, w_down):
    b, s, d = x.shape
    depth = w_in.shape[0]
    bf16, f32 = jnp.bfloat16, jnp.float32
    xf = x.reshape(b * s, d)
    for l in range(depth):
        lam_init = 0.8 - 0.6 * math.exp(-0.3 * l)
        q_scale = HEAD_DIM ** -0.5 * math.log2(math.e)
        qg, kg = q_gain[l].astype(f32), k_gain[l].astype(f32)
        col_gain = jnp.concatenate([
            jnp.tile(qg * q_scale, ATTN_WIDTH // HEAD_DIM),
            jnp.tile(kg, ATTN_WIDTH // HEAD_DIM),
            jnp.ones((IN_WIDTH - 2 * ATTN_WIDTH,), f32)])[None, :]
        proj = _in_proj(xf, attn_norm[l].astype(f32)[None, :], w_in[l].astype(bf16), col_gain)

        logit_bound = 1.01 * HEAD_DIM * q_scale * jnp.max(jnp.abs(qg)) * jnp.max(jnp.abs(kg))
        small_logits = (logit_bound <= _UNSHIFTED_SOFTMAX_MAX_LOG2).astype(jnp.int32).reshape(1)
        lamv = jnp.stack([lam_q1[l], lam_k1[l], lam_q2[l], lam_k2[l]]).astype(f32)
        mixed = _attention(proj, small_logits, lamv, sub_gain[l].astype(f32)[None, :],
                           batch=b, seq=s, lam_init=lam_init)

        tables = _ssm_tables(ssm_log_dt[l], ssm_a_re[l], ssm_a_im[l], ssm_b_re[l], ssm_b_im[l],
                             ssm_c_re[l], ssm_c_im[l], SSM_CHUNK)
        mixed = _ssm(proj, mixed, tables, ssm_d[l].astype(f32).reshape(1, SSM_WIDTH),
                     ssm_w_glu[l].astype(bf16), ssm_b_glu[l].astype(f32)[None, :],
                     ssm_out_gain[l].astype(f32)[None, :], batch=b, seq=s)

        x1 = _res_matmul(mixed, w_out[l].astype(bf16), xf, tm=512, tn=1024, name="out_proj")

        act = _ffn_up(x1, ffn_norm[l].astype(f32)[None, :], w_gate[l].astype(bf16),
                      w_up[l].astype(bf16), conv_w[l].astype(f32), conv_b[l].astype(f32)[None, :],
                      seq=s)
        xf = _res_matmul(act, w_down[l].astype(bf16), x1, tm=512, tn=512, name="ffn_down")
    return xf.reshape(b, s, d)
```

```python
import functools
import math

import jax
import jax.numpy as jnp
from jax import lax
from jax.experimental import pallas as pl
from jax.experimental.pallas import tpu as pltpu

D_MODEL = 4096
ATTN_HEADS = 12
HEAD_DIM = 128
ATTN_WIDTH = ATTN_HEADS * 2 * HEAD_DIM
SSM_WIDTH = D_MODEL - ATTN_WIDTH
SSM_GROUP = 16
SSM_GROUPS = SSM_WIDTH // SSM_GROUP
SSM_STATE = 64
IN_WIDTH = 3 * ATTN_WIDTH + SSM_WIDTH
D_FF = 11008
CONV_WIDTH = 3
EPS = 1e-6

V7X_LANES = 128
V7X_SUBLANES = 8
V7X_BF16_ROWS_PER_TILE = 2 * V7X_SUBLANES
V7X_VMEM_BYTES = 64 * 1024 * 1024
V7X_VMEM_RESERVED_BYTES = 6 * 1024 * 1024

F_TILE = 512
SSM_CLUSTER = V7X_LANES // SSM_GROUP
SSM_CLUSTERS = SSM_GROUPS // SSM_CLUSTER
SSM_PAIRS = 1
SSM_PAIR_STATE = SSM_CLUSTER * SSM_STATE // SSM_PAIRS
SSM_CL_COLS = SSM_PAIRS * 2 * SSM_PAIR_STATE
SSM_CHUNK = 128


def _vmem_limit(*nbytes):
    want = int(sum(nbytes)) + 12 * 1024 * 1024
    return min(want, V7X_VMEM_BYTES - V7X_VMEM_RESERVED_BYTES)


def _nbytes(shape, dtype):
    return math.prod(shape) * jnp.dtype(dtype).itemsize


def _rms_rows(x, gain):
    ms = jnp.mean(x * x, axis=-1, keepdims=True)
    return x * lax.rsqrt(ms + EPS) * gain


def _norm_into(x_ref, gain_ref, h_ref, *, rows, row_chunk, h_offset):
    def body(c, carry):
        r = pl.multiple_of(c * row_chunk, row_chunk)
        xb = x_ref[pl.ds(r, row_chunk), :]
        h_ref[pl.ds(h_offset + r, row_chunk), :] = _rms_rows(xb, gain_ref[...]).astype(h_ref.dtype)
        return carry
    lax.fori_loop(0, rows // row_chunk, body, 0)


def _in_proj_kernel(x_ref, g_ref, w_ref, cg_ref, o_ref, h_ref, *, tm, tn, n_norm_tiles):
    j = pl.program_id(1)

    @pl.when(j == 0)
    def _():
        _norm_into(x_ref, g_ref, h_ref, rows=tm, row_chunk=32, h_offset=0)

    acc = jnp.dot(h_ref[...], w_ref[...], preferred_element_type=jnp.float32)

    @pl.when(j < n_norm_tiles)
    def _():
        for s in range(tn // HEAD_DIM):
            sl = slice(s * HEAD_DIM, (s + 1) * HEAD_DIM)
            o_ref[:, sl] = _rms_rows(acc[:, sl], cg_ref[:, sl]).astype(o_ref.dtype)

    @pl.when(j >= n_norm_tiles)
    def _():
        o_ref[...] = acc.astype(o_ref.dtype)


def _in_proj(x2d, gain, w_bf16, col_gain, *, tm=512, tn=1024):
    m, k = x2d.shape
    n = w_bf16.shape[1]
    n_norm_tiles = 2 * ATTN_WIDTH // tn
    assert 2 * ATTN_WIDTH % tn == 0 and m % tm == 0 and n % tn == 0
    kern = functools.partial(_in_proj_kernel, tm=tm, tn=tn, n_norm_tiles=n_norm_tiles)
    vmem = _vmem_limit(2 * _nbytes((tm, k), jnp.float32), _nbytes((tm, k), jnp.bfloat16),
                       2 * _nbytes((k, tn), jnp.bfloat16), 2 * _nbytes((tm, tn), jnp.bfloat16),
                       _nbytes((tm, tn), jnp.float32))
    return pl.pallas_call(
        kern,
        out_shape=jax.ShapeDtypeStruct((m, n), jnp.bfloat16),
        grid=(m // tm, n // tn),
        in_specs=[
            pl.BlockSpec((tm, k), lambda i, j: (i, 0)),
            pl.BlockSpec((1, k), lambda i, j: (0, 0)),
            pl.BlockSpec((k, tn), lambda i, j: (0, j)),
            pl.BlockSpec((1, tn), lambda i, j: (0, j)),
        ],
        out_specs=pl.BlockSpec((tm, tn), lambda i, j: (i, j)),
        scratch_shapes=[pltpu.VMEM((tm, k), jnp.bfloat16)],
        compiler_params=pltpu.CompilerParams(
            dimension_semantics=("arbitrary", "arbitrary"), vmem_limit_bytes=vmem),
        name="in_proj",
    )(x2d, gain, w_bf16, col_gain)


_UNSHIFTED_SOFTMAX_MAX_LOG2 = 60.0


def _qk_scores(q, k, row_offset):
    s = lax.dot_general(q, k, (((1,), (1,)), ((), ())), preferred_element_type=jnp.float32)
    if row_offset is not None:
        row = lax.broadcasted_iota(jnp.int32, s.shape, 0)
        col = lax.broadcasted_iota(jnp.int32, s.shape, 1)
        s = jnp.where(col <= row + row_offset, s, -jnp.inf)
    return s


def _lane_partial_sums(p):
    ps = p[:, 0:V7X_LANES]
    for c in range(1, p.shape[1] // V7X_LANES):
        ps = ps + p[:, c * V7X_LANES:(c + 1) * V7X_LANES]
    return ps


def _attn_kernel(small_ref, lamv_ref, q_ref, k_ref, v_ref, sg_ref, o_ref,
                 m_ref, l_ref, acc_ref, p_ref, *, blk, nq, lam_init):
    unshifted = small_ref[0] == 1
    lamv = lamv_ref[...]
    lam = (jnp.exp(jnp.sum(lamv[0:1] * lamv[1:2], axis=-1, keepdims=True))
           - jnp.exp(jnp.sum(lamv[2:3] * lamv[3:4], axis=-1, keepdims=True)) + lam_init)

    def head(mi):
        return slice(mi * HEAD_DIM, (mi + 1) * HEAD_DIM)

    def rows_of(mi):
        return slice(mi * blk, (mi + 1) * blk)

    def q_block(qi, carry):
        qs = pl.ds(pl.multiple_of(qi * blk, blk), blk)
        l_ref[...] = jnp.zeros(l_ref.shape, jnp.float32)
        acc_ref[...] = jnp.zeros(acc_ref.shape, jnp.float32)

        def unshifted_step(start, width, masked):
            ks = pl.ds(start, width)
            for mi in range(2):
                p = jnp.exp2(_qk_scores(q_ref[qs, head(mi)], k_ref[ks, head(mi)],
                                        0 if masked else None))
                l_ref[rows_of(mi), :] += _lane_partial_sums(p)
                p_ref[rows_of(mi), 0:width] = p.astype(p_ref.dtype)
            acc_ref[...] += jnp.dot(p_ref[:, 0:width], v_ref[ks, :],
                                    preferred_element_type=jnp.float32)

        def shifted_step(j, masked):
            ks = pl.ds(pl.multiple_of(j * blk, blk), blk)
            vb = v_ref[ks, :]
            for mi in range(2):
                s = _qk_scores(q_ref[qs, head(mi)], k_ref[ks, head(mi)], 0 if masked else None)
                rs = rows_of(mi)
                m_old = m_ref[mi]
                m_new = jnp.maximum(m_old, jnp.max(s, axis=-1, keepdims=True))
                alpha = jnp.exp2(m_old - m_new)
                p = jnp.exp2(s - m_new)
                l_ref[rs, 0:1] = alpha * l_ref[rs, 0:1] + jnp.sum(p, axis=-1, keepdims=True)
                acc_ref[rs, :] = alpha * acc_ref[rs, :] + jnp.dot(
                    p.astype(vb.dtype), vb, preferred_element_type=jnp.float32)
                m_ref[mi] = m_new

        @pl.when(unshifted)
        def _():
            def pair_body(j, c):
                unshifted_step(pl.multiple_of(j * 2 * blk, 2 * blk), 2 * blk, False)
                return c
            lax.fori_loop(0, qi // 2, pair_body, 0)

            @pl.when(qi % 2 == 1)
            def _():
                unshifted_step(pl.multiple_of((qi - 1) * blk, blk), blk, False)

            unshifted_step(pl.multiple_of(qi * blk, blk), blk, True)

        @pl.when(jnp.logical_not(unshifted))
        def _():
            m_ref[...] = jnp.full(m_ref.shape, -jnp.inf, jnp.float32)

            def body(j, c):
                shifted_step(j, False)
                return c
            lax.fori_loop(0, qi, body, 0)
            shifted_step(qi, True)

        r1 = 1.0 / jnp.sum(l_ref[rows_of(0), :], axis=-1, keepdims=True)
        r2 = lam / jnp.sum(l_ref[rows_of(1), :], axis=-1, keepdims=True)
        o = acc_ref[rows_of(0), :] * r1 - acc_ref[rows_of(1), :] * r2
        o_ref[qs, :] = (_rms_rows(o, sg_ref[...]) * (1.0 - lam_init)).astype(o_ref.dtype)
        return carry

    lax.fori_loop(0, nq, q_block, 0)


def _attention(proj, small_logits, lamv, sub_gain, *, batch, seq, lam_init, blk=512):
    nq = seq // blk
    hw = 2 * HEAD_DIM
    kern = functools.partial(_attn_kernel, blk=blk, nq=nq, lam_init=lam_init)
    vmem = _vmem_limit(8 * _nbytes((seq, hw), jnp.bfloat16),
                       _nbytes((2 * blk, hw), jnp.float32), _nbytes((2 * blk, 2 * blk), jnp.bfloat16),
                       3 * _nbytes((blk, V7X_LANES), jnp.float32),
                       4 * _nbytes((blk, 2 * blk), jnp.float32))
    return pl.pallas_call(
        kern,
        out_shape=jax.ShapeDtypeStruct((batch * seq, D_MODEL), jnp.bfloat16),
        grid=(batch, ATTN_HEADS),
        in_specs=[
            pl.BlockSpec(memory_space=pltpu.SMEM),
            pl.BlockSpec((4, HEAD_DIM), lambda b, h: (0, 0)),
            pl.BlockSpec((seq, hw), lambda b, h: (b, h)),
            pl.BlockSpec((seq, hw), lambda b, h: (b, ATTN_HEADS + h)),
            pl.BlockSpec((seq, hw), lambda b, h: (b, 2 * ATTN_HEADS + h)),
            pl.BlockSpec((1, hw), lambda b, h: (0, 0)),
        ],
        out_specs=pl.BlockSpec((seq, hw), lambda b, h: (b, h)),
        scratch_shapes=[
            pltpu.VMEM((2, blk, 1), jnp.float32),
            pltpu.VMEM((2 * blk, V7X_LANES), jnp.float32),
            pltpu.VMEM((2 * blk, hw), jnp.float32),
            pltpu.VMEM((2 * blk, 2 * blk), jnp.bfloat16),
        ],
        compiler_params=pltpu.CompilerParams(
            dimension_semantics=("arbitrary", "arbitrary"), vmem_limit_bytes=vmem),
        name="diff_attn",
    )(small_logits, lamv, proj, proj, proj, sub_gain)


def _cmul(ar, ai, br, bi):
    return ar * br - ai * bi, ar * bi + ai * br


def _ssm_kernel(u_ref, mixed_ref, bmat_ref, cmat_ref, pinv_ref, pfwd_ref, pcar_ref, d_ref,
                wglu_ref, bglu_ref, og_ref, o_ref, h_ref, y_ref, *, chunk):
    del mixed_ref
    c = pl.program_id(1)

    @pl.when(c == 0)
    def _():
        h_ref[...] = jnp.zeros(h_ref.shape, jnp.float32)

    n = SSM_PAIR_STATE
    row = lax.broadcasted_iota(jnp.int32, (chunk, chunk), 0)
    col = lax.broadcasted_iota(jnp.int32, (chunk, chunk), 1)
    tri = (col <= row).astype(jnp.bfloat16)

    for cl in range(SSM_CLUSTERS):
        ls = slice(cl * V7X_LANES, (cl + 1) * V7X_LANES)
        u_cl = u_ref[:, ls]
        y_cl = None
        for pr in range(SSM_PAIRS):
            re = slice(pr * 2 * n, pr * 2 * n + n)
            im = slice(pr * 2 * n + n, (pr + 1) * 2 * n)
            both = slice(pr * 2 * n, (pr + 1) * 2 * n)
            bu = jnp.dot(u_cl, bmat_ref[cl, :, both], preferred_element_type=jnp.float32)
            xr, xi = _cmul(pinv_ref[cl, :, re], pinv_ref[cl, :, im], bu[:, :n], bu[:, n:])
            x = jnp.concatenate([xr, xi], axis=1).astype(jnp.bfloat16)
            cs = jnp.dot(tri, x, preferred_element_type=jnp.float32)
            cr, ci = _cmul(pcar_ref[cl:cl + 1, re], pcar_ref[cl:cl + 1, im],
                           h_ref[cl:cl + 1, re], h_ref[cl:cl + 1, im])
            hr, hi = _cmul(pfwd_ref[cl, :, re], pfwd_ref[cl, :, im], cs[:, :n] + cr, cs[:, n:] + ci)
            h_ref[cl:cl + 1, re] = hr[chunk - 1:chunk, :]
            h_ref[cl:cl + 1, im] = hi[chunk - 1:chunk, :]
            h = jnp.concatenate([hr, hi], axis=1).astype(jnp.bfloat16)
            part = jnp.dot(h, cmat_ref[cl, both, :], preferred_element_type=jnp.float32)
            y_cl = part if y_cl is None else y_cl + part
        y_ref[:, ls] = y_cl

    y = y_ref[...] + d_ref[...] * u_ref[...].astype(jnp.float32)
    y = jax.nn.gelu(y, approximate=True)
    z = jnp.dot(y.astype(jnp.bfloat16), wglu_ref[...], preferred_element_type=jnp.float32)
    z = z + bglu_ref[...]
    y = y * (1.0 / (1.0 + jnp.exp(-z)))
    o_ref[...] = _rms_rows(y, og_ref[...]).astype(o_ref.dtype)


def _ssm_tables(log_dt, a_re, a_im, b_re, b_im, c_re, c_im, chunk):
    f32 = jnp.float32
    n, c = SSM_STATE, SSM_GROUP
    dt = jnp.exp(log_dt.astype(f32))[:, None]
    ar, ai = a_re.astype(f32), a_im.astype(f32)
    mag = jnp.exp(ar * dt)
    lb_re = mag * jnp.cos(ai * dt)
    lb_im = mag * jnp.sin(ai * dt)
    pr, pi_ = lb_re - 1.0, lb_im
    den = ar * ar + ai * ai
    z_re = ((pr * ar + pi_ * ai) / den)[..., None]
    z_im = ((pi_ * ar - pr * ai) / den)[..., None]
    br, bi = b_re.astype(f32), b_im.astype(f32)
    bb_re = z_re * br - z_im * bi
    bb_im = z_re * bi + z_im * br

    def powers(k):
        kk = k.astype(f32)[:, None, None]
        pm = jnp.exp(kk * (ar * dt)[None])
        ph = kk * (ai * dt)[None]
        return pm * jnp.cos(ph), pm * jnp.sin(ph)

    def pair_cols(re, im):
        lead = re.shape[:-1]
        re = re.reshape(*lead, SSM_PAIRS, 1, SSM_PAIR_STATE)
        im = im.reshape(*lead, SSM_PAIRS, 1, SSM_PAIR_STATE)
        return jnp.concatenate([re, im], axis=-2).reshape(*lead, SSM_CL_COLS)

    def table(k):
        re, im = powers(k)
        kk = re.shape[0]
        re = re.reshape(kk, SSM_CLUSTERS, SSM_CLUSTER * n)
        im = im.reshape(kk, SSM_CLUSTERS, SSM_CLUSTER * n)
        return pair_cols(re, im).transpose(1, 0, 2)

    c0 = chunk // 2
    t = jnp.arange(chunk)
    pinv = table(c0 - t)
    pfwd = table(t - c0)
    pcar = table(jnp.array([c0 + 1]))[:, 0, :]

    eye = jnp.eye(SSM_CLUSTER, dtype=f32)
    def bpart(bb):
        b4 = bb.reshape(SSM_CLUSTERS, SSM_CLUSTER, n, c)
        return jnp.einsum('kgnc,gh->kgchn', b4, eye).reshape(
            SSM_CLUSTERS, V7X_LANES, SSM_CLUSTER * n)
    bmat = pair_cols(bpart(bb_re), bpart(bb_im)).astype(jnp.bfloat16)
    def cpart(cc):
        c4 = cc.astype(f32).reshape(SSM_CLUSTERS, SSM_CLUSTER, c, n)
        return jnp.einsum('kgcn,gh->khcgn', c4, eye).reshape(
            SSM_CLUSTERS, V7X_LANES, SSM_CLUSTER * n)
    cmat = pair_cols(cpart(c_re), -cpart(c_im)).transpose(0, 2, 1).astype(jnp.bfloat16)
    return bmat, cmat, pinv, pfwd, pcar


def _ssm(proj, mixed, tables, d_skip, w_glu, b_glu, out_gain, *, batch, seq, chunk=SSM_CHUNK):
    bmat, cmat, pinv, pfwd, pcar = tables
    nc = seq // chunk
    u_col = 3 * ATTN_WIDTH // SSM_WIDTH
    o_col = ATTN_WIDTH // SSM_WIDTH
    kern = functools.partial(_ssm_kernel, chunk=chunk)
    const3 = lambda b, c: (0, 0, 0)
    const2 = lambda b, c: (0, 0)
    vmem = _vmem_limit(2 * (bmat.nbytes + cmat.nbytes + pinv.nbytes + pfwd.nbytes + pcar.nbytes),
                       4 * _nbytes((SSM_WIDTH, SSM_WIDTH), jnp.bfloat16),
                       8 * _nbytes((chunk, SSM_WIDTH), jnp.float32))
    return pl.pallas_call(
        kern,
        out_shape=jax.ShapeDtypeStruct(mixed.shape, mixed.dtype),
        grid=(batch, nc),
        in_specs=[
            pl.BlockSpec((chunk, SSM_WIDTH), lambda b, c: (b * nc + c, u_col)),
            pl.BlockSpec(memory_space=pl.ANY),
            pl.BlockSpec(bmat.shape, const3),
            pl.BlockSpec(cmat.shape, const3),
            pl.BlockSpec(pinv.shape, const3),
            pl.BlockSpec(pfwd.shape, const3),
            pl.BlockSpec(pcar.shape, const2),
            pl.BlockSpec((1, SSM_WIDTH), const2),
            pl.BlockSpec((SSM_WIDTH, SSM_WIDTH), const2),
            pl.BlockSpec((1, SSM_WIDTH), const2),
            pl.BlockSpec((1, SSM_WIDTH), const2),
        ],
        out_specs=pl.BlockSpec((chunk, SSM_WIDTH), lambda b, c: (b * nc + c, o_col)),
        scratch_shapes=[
            pltpu.VMEM((SSM_CLUSTERS, SSM_CL_COLS), jnp.float32),
            pltpu.VMEM((chunk, SSM_WIDTH), jnp.float32),
        ],
        input_output_aliases={1: 0},
        compiler_params=pltpu.CompilerParams(
            dimension_semantics=("arbitrary", "arbitrary"), vmem_limit_bytes=vmem),
        name="s5_ssm",
    )(proj, mixed, bmat, cmat, pinv, pfwd, pcar, d_skip, w_glu, b_glu, out_gain)


def _res_matmul_kernel(a_ref, w_ref, r_ref, o_ref):
    o_ref[...] = r_ref[...] + jnp.dot(a_ref[...], w_ref[...], preferred_element_type=jnp.float32)


def _res_matmul(a, w, res, *, tm, tn, name):
    m, k = a.shape
    n = w.shape[1]
    assert m % tm == 0 and n % tn == 0
    vmem = _vmem_limit(2 * _nbytes((tm, k), a.dtype), 2 * _nbytes((k, tn), w.dtype),
                       5 * _nbytes((tm, tn), jnp.float32))
    return pl.pallas_call(
        _res_matmul_kernel,
        out_shape=jax.ShapeDtypeStruct((m, n), jnp.float32),
        grid=(m // tm, n // tn),
        in_specs=[
            pl.BlockSpec((tm, k), lambda i, j: (i, 0)),
            pl.BlockSpec((k, tn), lambda i, j: (0, j)),
            pl.BlockSpec((tm, tn), lambda i, j: (i, j)),
        ],
        out_specs=pl.BlockSpec((tm, tn), lambda i, j: (i, j)),
        compiler_params=pltpu.CompilerParams(
            dimension_semantics=("parallel", "parallel"), vmem_limit_bytes=vmem),
        name=name,
    )(a, w, res)


def _ffn_up_kernel(x_ref, halo_ref, g_ref, wg_ref, wu_ref, cw_ref, cb_ref, o_ref, h_ref,
                   *, tm, blocks_per_seq):
    i = pl.program_id(0)
    j = pl.program_id(1)
    halo = V7X_BF16_ROWS_PER_TILE

    @pl.when(j == 0)
    def _():
        hh = _rms_rows(halo_ref[...], g_ref[...])
        hh = jnp.where(i % blocks_per_seq == 0, 0.0, hh)
        h_ref[0:halo, :] = hh.astype(h_ref.dtype)
        _norm_into(x_ref, g_ref, h_ref, rows=tm, row_chunk=32, h_offset=halo)

    g_all = jnp.dot(h_ref[...], wg_ref[...], preferred_element_type=jnp.float32)
    up = jnp.dot(h_ref[halo:, :], wu_ref[...], preferred_element_type=jnp.float32)
    rows = 128
    for r0 in range(0, tm, rows):
        for c0 in range(0, o_ref.shape[1], V7X_LANES):
            cs = slice(c0, c0 + V7X_LANES)
            g = g_all[r0:r0 + halo + rows, cs]
            t = cw_ref[1:2, cs] * g + pltpu.roll(cw_ref[0:1, cs] * g, 1, 0)
            gc = (cb_ref[:, cs] + cw_ref[2:3, cs] * g + pltpu.roll(t, 1, 0))[halo:, :]
            act = gc * (1.0 / (1.0 + jnp.exp(-gc))) * up[r0:r0 + rows, cs]
            o_ref[r0:r0 + rows, cs] = act.astype(o_ref.dtype)


def _ffn_up(x1, gain, wg, wu, conv_w, conv_b, *, seq, tm=512, tf=F_TILE):
    m, k = x1.shape
    f = wg.shape[1]
    halo = V7X_BF16_ROWS_PER_TILE
    assert m % tm == 0 and seq % tm == 0
    kern = functools.partial(_ffn_up_kernel, tm=tm, blocks_per_seq=seq // tm)
    vmem = _vmem_limit(2 * _nbytes((tm, k), jnp.float32), _nbytes((tm + halo, k), jnp.bfloat16),
                       4 * _nbytes((k, tf), jnp.bfloat16), 2 * _nbytes((tm, tf), jnp.bfloat16),
                       3 * _nbytes((tm + halo, tf), jnp.float32))
    return pl.pallas_call(
        kern,
        out_shape=jax.ShapeDtypeStruct((m, f), jnp.bfloat16),
        grid=(m // tm, pl.cdiv(f, tf)),
        in_specs=[
            pl.BlockSpec((tm, k), lambda i, j: (i, 0)),
            pl.BlockSpec((halo, k), lambda i, j: (jnp.maximum(i * (tm // halo) - 1, 0), 0)),
            pl.BlockSpec((1, k), lambda i, j: (0, 0)),
            pl.BlockSpec((k, tf), lambda i, j: (0, j)),
            pl.BlockSpec((k, tf), lambda i, j: (0, j)),
            pl.BlockSpec((CONV_WIDTH, tf), lambda i, j: (0, j)),
            pl.BlockSpec((1, tf), lambda i, j: (0, j)),
        ],
        out_specs=pl.BlockSpec((tm, tf), lambda i, j: (i, j)),
        scratch_shapes=[pltpu.VMEM((tm + halo, k), jnp.bfloat16)],
        compiler_params=pltpu.CompilerParams(
            dimension_semantics=("arbitrary", "arbitrary"), vmem_limit_bytes=vmem),
        name="ffn_up",
    )(x1, x1, gain, wg, wu, conv_w, conv_b)


def kernel(x, attn_norm, w_in, q_gain, k_gain, lam_q1, lam_k1, lam_q2, lam_k2, sub_gain, ssm_log_dt, ssm_a_re, ssm_a_im, ssm_b_re, ssm_b_im, ssm_c_re, ssm_c_im, ssm_d, ssm_w_glu, ssm_b_glu, ssm_out_gain, w_out, ffn_norm, w_gate, w_up, conv_w, conv_b, w_down):
    b, s, d = x.shape
    depth = w_in.shape[0]
    bf16, f32 = jnp.bfloat16, jnp.float32
    xf = x.reshape(b * s, d)
    for l in range(depth):
        lam_init = 0.8 - 0.6 * math.exp(-0.3 * l)
        q_scale = HEAD_DIM ** -0.5 * math.log2(math.e)
        qg, kg = q_gain[l].astype(f32), k_gain[l].astype(f32)
        col_gain = jnp.concatenate([
            jnp.tile(qg * q_scale, ATTN_WIDTH // HEAD_DIM),
            jnp.tile(kg, ATTN_WIDTH // HEAD_DIM),
            jnp.ones((IN_WIDTH - 2 * ATTN_WIDTH,), f32)])[None, :]
        proj = _in_proj(xf, attn_norm[l].astype(f32)[None, :], w_in[l].astype(bf16), col_gain)

        logit_bound = 1.01 * HEAD_DIM * q_scale * jnp.max(jnp.abs(qg)) * jnp.max(jnp.abs(kg))
        small_logits = (logit_bound <= _UNSHIFTED_SOFTMAX_MAX_LOG2).astype(jnp.int32).reshape(1)
        lamv = jnp.stack([lam_q1[l], lam_k1[l], lam_q2[l], lam_k2[l]]).astype(f32)
        mixed = _attention(proj, small_logits, lamv, sub_gain[l].astype(f32)[None, :],
                           batch=b, seq=s, lam_init=lam_init)

        tables = _ssm_tables(ssm_log_dt[l], ssm_a_re[l], ssm_a_im[l], ssm_b_re[l], ssm_b_im[l],
                             ssm_c_re[l], ssm_c_im[l], SSM_CHUNK)
        mixed = _ssm(proj, mixed, tables, ssm_d[l].astype(f32).reshape(1, SSM_WIDTH),
                     ssm_w_glu[l].astype(bf16), ssm_b_glu[l].astype(f32)[None, :],
                     ssm_out_gain[l].astype(f32)[None, :], batch=b, seq=s)

        x1 = _res_matmul(mixed, w_out[l].astype(bf16), xf, tm=512, tn=1024, name="out_proj")

        act = _ffn_up(x1, ffn_norm[l].astype(f32)[None, :], w_gate[l].astype(bf16),
                      w_up[l].astype(bf16), conv_w[l].astype(f32), conv_b[l].astype(f32)[None, :],
                      seq=s)
        xf = _res_matmul(act, w_down[l].astype(bf16), x1, tm=512, tn=512, name="ffn_down")
    return xf.reshape(b, s, d)
```

```python
import functools
import math

import jax
import jax.numpy as jnp
from jax import lax
from jax.experimental import pallas as pl
from jax.experimental.pallas import tpu as pltpu

D_MODEL = 4096
ATTN_HEADS = 12
HEAD_DIM = 128
ATTN_WIDTH = ATTN_HEADS * 2 * HEAD_DIM
SSM_WIDTH = D_MODEL - ATTN_WIDTH
SSM_GROUP = 16
SSM_GROUPS = SSM_WIDTH // SSM_GROUP
SSM_STATE = 64
IN_WIDTH = 3 * ATTN_WIDTH + SSM_WIDTH
D_FF = 11008
CONV_WIDTH = 3
EPS = 1e-6

V7X_LANES = 128
V7X_SUBLANES = 8
V7X_BF16_ROWS_PER_TILE = 2 * V7X_SUBLANES
V7X_VMEM_BYTES = 64 * 1024 * 1024
V7X_VMEM_RESERVED_BYTES = 6 * 1024 * 1024

F_TILE = 512
SSM_CLUSTER = V7X_LANES // SSM_GROUP
SSM_CLUSTERS = SSM_GROUPS // SSM_CLUSTER
SSM_PAIRS = 1
SSM_PAIR_STATE = SSM_CLUSTER * SSM_STATE // SSM_PAIRS
SSM_CL_COLS = SSM_PAIRS * 2 * SSM_PAIR_STATE
SSM_CHUNK = 256


def _vmem_limit(*nbytes):
    want = int(sum(nbytes)) + 12 * 1024 * 1024
    return min(want, V7X_VMEM_BYTES - V7X_VMEM_RESERVED_BYTES)


def _nbytes(shape, dtype):
    return math.prod(shape) * jnp.dtype(dtype).itemsize


def _rms_rows(x, gain):
    ms = jnp.mean(x * x, axis=-1, keepdims=True)
    return x * lax.rsqrt(ms + EPS) * gain


def _norm_into(x_ref, gain_ref, h_ref, *, rows, row_chunk, h_offset):
    def body(c, carry):
        r = pl.multiple_of(c * row_chunk, row_chunk)
        xb = x_ref[pl.ds(r, row_chunk), :]
        h_ref[pl.ds(h_offset + r, row_chunk), :] = _rms_rows(xb, gain_ref[...]).astype(h_ref.dtype)
        return carry
    lax.fori_loop(0, rows // row_chunk, body, 0)


def _in_proj_kernel(x_ref, g_ref, w_ref, cg_ref, o_ref, h_ref, *, tm, tn, n_norm_tiles, n_parts):
    j = pl.program_id(1)

    @pl.when(j == 0)
    def _():
        _norm_into(x_ref, g_ref, h_ref, rows=tm, row_chunk=32, h_offset=0)

    is_qk = j < n_norm_tiles
    part = tn // n_parts
    for c0 in range(0, tn, part):
        acc = jnp.dot(h_ref[...], w_ref[:, c0:c0 + part], preferred_element_type=jnp.float32)
        for s0 in range(0, part, HEAD_DIM):
            raw = acc[:, s0:s0 + HEAD_DIM]
            sl = slice(c0 + s0, c0 + s0 + HEAD_DIM)
            o_ref[:, sl] = jnp.where(is_qk, _rms_rows(raw, cg_ref[:, sl]), raw).astype(o_ref.dtype)


def _in_proj(x2d, gain, w_bf16, col_gain, *, tm=512, tn=1024):
    m, k = x2d.shape
    n = w_bf16.shape[1]
    n_norm_tiles = 2 * ATTN_WIDTH // tn
    assert 2 * ATTN_WIDTH % tn == 0 and m % tm == 0 and n % tn == 0
    kern = functools.partial(_in_proj_kernel, tm=tm, tn=tn, n_norm_tiles=n_norm_tiles, n_parts=4)
    vmem = _vmem_limit(2 * _nbytes((tm, k), jnp.float32), _nbytes((tm, k), jnp.bfloat16),
                       2 * _nbytes((k, tn), jnp.bfloat16), 2 * _nbytes((tm, tn), jnp.bfloat16),
                       _nbytes((tm, tn), jnp.float32))
    return pl.pallas_call(
        kern,
        out_shape=jax.ShapeDtypeStruct((m, n), jnp.bfloat16),
        grid=(m // tm, n // tn),
        in_specs=[
            pl.BlockSpec((tm, k), lambda i, j: (i, 0)),
            pl.BlockSpec((1, k), lambda i, j: (0, 0)),
            pl.BlockSpec((k, tn), lambda i, j: (0, j)),
            pl.BlockSpec((1, tn), lambda i, j: (0, j)),
        ],
        out_specs=pl.BlockSpec((tm, tn), lambda i, j: (i, j)),
        scratch_shapes=[pltpu.VMEM((tm, k), jnp.bfloat16)],
        compiler_params=pltpu.CompilerParams(
            dimension_semantics=("arbitrary", "arbitrary"), vmem_limit_bytes=vmem),
        name="in_proj",
    )(x2d, gain, w_bf16, col_gain)


_UNSHIFTED_SOFTMAX_MAX_LOG2 = 60.0


def _qk_scores(q, k, row_offset):
    s = lax.dot_general(q, k, (((1,), (1,)), ((), ())), preferred_element_type=jnp.float32)
    if row_offset is not None:
        row = lax.broadcasted_iota(jnp.int32, s.shape, 0)
        col = lax.broadcasted_iota(jnp.int32, s.shape, 1)
        s = jnp.where(col <= row + row_offset, s, -jnp.inf)
    return s


def _lane_partial_sums(p):
    ps = p[:, 0:V7X_LANES]
    for c in range(1, p.shape[1] // V7X_LANES):
        ps = ps + p[:, c * V7X_LANES:(c + 1) * V7X_LANES]
    return ps


def _attn_kernel(small_ref, lamv_ref, q_ref, k_ref, v_ref, sg_ref, o_ref,
                 m_ref, l_ref, acc_ref, p_ref, *, blk, nq, lam_init):
    unshifted = small_ref[0] == 1
    lamv = lamv_ref[...]
    lam = (jnp.exp(jnp.sum(lamv[0:1] * lamv[1:2], axis=-1, keepdims=True))
           - jnp.exp(jnp.sum(lamv[2:3] * lamv[3:4], axis=-1, keepdims=True)) + lam_init)

    def head(mi):
        return slice(mi * HEAD_DIM, (mi + 1) * HEAD_DIM)

    def rows_of(mi):
        return slice(mi * blk, (mi + 1) * blk)

    def q_block(qi, carry):
        qs = pl.ds(pl.multiple_of(qi * blk, blk), blk)
        l_ref[...] = jnp.zeros(l_ref.shape, jnp.float32)
        acc_ref[...] = jnp.zeros(acc_ref.shape, jnp.float32)

        def unshifted_step(start, width, masked):
            ks = pl.ds(start, width)
            for mi in range(2):
                p = jnp.exp2(_qk_scores(q_ref[qs, head(mi)], k_ref[ks, head(mi)],
                                        0 if masked else None))
                l_ref[rows_of(mi), :] += _lane_partial_sums(p)
                p_ref[rows_of(mi), 0:width] = p.astype(p_ref.dtype)
            acc_ref[...] += jnp.dot(p_ref[:, 0:width], v_ref[ks, :],
                                    preferred_element_type=jnp.float32)

        def shifted_step(j, masked):
            ks = pl.ds(pl.multiple_of(j * blk, blk), blk)
            vb = v_ref[ks, :]
            for mi in range(2):
                s = _qk_scores(q_ref[qs, head(mi)], k_ref[ks, head(mi)], 0 if masked else None)
                rs = rows_of(mi)
                m_old = m_ref[mi]
                m_new = jnp.maximum(m_old, jnp.max(s, axis=-1, keepdims=True))
                alpha = jnp.exp2(m_old - m_new)
                p = jnp.exp2(s - m_new)
                l_ref[rs, 0:1] = alpha * l_ref[rs, 0:1] + jnp.sum(p, axis=-1, keepdims=True)
                acc_ref[rs, :] = alpha * acc_ref[rs, :] + jnp.dot(
                    p.astype(vb.dtype), vb, preferred_element_type=jnp.float32)
                m_ref[mi] = m_new

        @pl.when(unshifted)
        def _():
            def pair_body(j, c):
                unshifted_step(pl.multiple_of(j * 2 * blk, 2 * blk), 2 * blk, False)
                return c
            lax.fori_loop(0, qi // 2, pair_body, 0)

            @pl.when(qi % 2 == 1)
            def _():
                unshifted_step(pl.multiple_of((qi - 1) * blk, blk), blk, False)

            unshifted_step(pl.multiple_of(qi * blk, blk), blk, True)

        @pl.when(jnp.logical_not(unshifted))
        def _():
            m_ref[...] = jnp.full(m_ref.shape, -jnp.inf, jnp.float32)

            def body(j, c):
                shifted_step(j, False)
                return c
            lax.fori_loop(0, qi, body, 0)
            shifted_step(qi, True)

        r1 = 1.0 / jnp.sum(l_ref[rows_of(0), :], axis=-1, keepdims=True)
        r2 = lam / jnp.sum(l_ref[rows_of(1), :], axis=-1, keepdims=True)
        o = acc_ref[rows_of(0), :] * r1 - acc_ref[rows_of(1), :] * r2
        o_ref[qs, :] = (_rms_rows(o, sg_ref[...]) * (1.0 - lam_init)).astype(o_ref.dtype)
        return carry

    lax.fori_loop(0, nq, q_block, 0)


def _attention(proj, small_logits, lamv, sub_gain, *, batch, seq, lam_init, blk=512):
    nq = seq // blk
    hw = 2 * HEAD_DIM
    kern = functools.partial(_attn_kernel, blk=blk, nq=nq, lam_init=lam_init)
    vmem = _vmem_limit(8 * _nbytes((seq, hw), jnp.bfloat16),
                       _nbytes((2 * blk, hw), jnp.float32), _nbytes((2 * blk, 2 * blk), jnp.bfloat16),
                       3 * _nbytes((blk, V7X_LANES), jnp.float32),
                       4 * _nbytes((blk, 2 * blk), jnp.float32))
    return pl.pallas_call(
        kern,
        out_shape=jax.ShapeDtypeStruct((batch * seq, D_MODEL), jnp.bfloat16),
        grid=(batch, ATTN_HEADS),
        in_specs=[
            pl.BlockSpec(memory_space=pltpu.SMEM),
            pl.BlockSpec((4, HEAD_DIM), lambda b, h: (0, 0)),
            pl.BlockSpec((seq, hw), lambda b, h: (b, h)),
            pl.BlockSpec((seq, hw), lambda b, h: (b, ATTN_HEADS + h)),
            pl.BlockSpec((seq, hw), lambda b, h: (b, 2 * ATTN_HEADS + h)),
            pl.BlockSpec((1, hw), lambda b, h: (0, 0)),
        ],
        out_specs=pl.BlockSpec((seq, hw), lambda b, h: (b, h)),
        scratch_shapes=[
            pltpu.VMEM((2, blk, 1), jnp.float32),
            pltpu.VMEM((2 * blk, V7X_LANES), jnp.float32),
            pltpu.VMEM((2 * blk, hw), jnp.float32),
            pltpu.VMEM((2 * blk, 2 * blk), jnp.bfloat16),
        ],
        compiler_params=pltpu.CompilerParams(
            dimension_semantics=("arbitrary", "arbitrary"), vmem_limit_bytes=vmem),
        name="diff_attn",
    )(small_logits, lamv, proj, proj, proj, sub_gain)


def _cmul(ar, ai, br, bi):
    return ar * br - ai * bi, ar * bi + ai * br


def _ssm_kernel(u_ref, mixed_ref, bmat_ref, cmat_ref, pinv_ref, pfwd_ref, pcar_ref, d_ref,
                wglu_ref, bglu_ref, og_ref, o_ref, h_ref, y_ref, *, chunk):
    del mixed_ref
    c = pl.program_id(1)

    @pl.when(c == 0)
    def _():
        h_ref[...] = jnp.zeros(h_ref.shape, jnp.float32)

    n = SSM_PAIR_STATE
    row = lax.broadcasted_iota(jnp.int32, (chunk, chunk), 0)
    col = lax.broadcasted_iota(jnp.int32, (chunk, chunk), 1)
    tri = (col <= row).astype(jnp.bfloat16)

    for cl in range(SSM_CLUSTERS):
        ls = slice(cl * V7X_LANES, (cl + 1) * V7X_LANES)
        u_cl = u_ref[:, ls]
        y_cl = None
        for pr in range(SSM_PAIRS):
            re = slice(pr * 2 * n, pr * 2 * n + n)
            im = slice(pr * 2 * n + n, (pr + 1) * 2 * n)
            both = slice(pr * 2 * n, (pr + 1) * 2 * n)
            bu = jnp.dot(u_cl, bmat_ref[cl, :, both], preferred_element_type=jnp.float32)
            xr, xi = _cmul(pinv_ref[cl, :, re], pinv_ref[cl, :, im], bu[:, :n], bu[:, n:])
            x = jnp.concatenate([xr, xi], axis=1).astype(jnp.bfloat16)
            cs = jnp.dot(tri, x, preferred_element_type=jnp.float32)
            cr, ci = _cmul(pcar_ref[cl:cl + 1, re], pcar_ref[cl:cl + 1, im],
                           h_ref[cl:cl + 1, re], h_ref[cl:cl + 1, im])
            hr, hi = _cmul(pfwd_ref[cl, :, re], pfwd_ref[cl, :, im], cs[:, :n] + cr, cs[:, n:] + ci)
            h_ref[cl:cl + 1, re] = hr[chunk - 1:chunk, :]
            h_ref[cl:cl + 1, im] = hi[chunk - 1:chunk, :]
            h = jnp.concatenate([hr, hi], axis=1).astype(jnp.bfloat16)
            part = jnp.dot(h, cmat_ref[cl, both, :], preferred_element_type=jnp.float32)
            y_cl = part if y_cl is None else y_cl + part
        y_ref[:, ls] = y_cl

    y = y_ref[...] + d_ref[...] * u_ref[...].astype(jnp.float32)
    y = jax.nn.gelu(y, approximate=True)
    z = jnp.dot(y.astype(jnp.bfloat16), wglu_ref[...], preferred_element_type=jnp.float32)
    z = z + bglu_ref[...]
    y = y * (1.0 / (1.0 + jnp.exp(-z)))
    o_ref[...] = _rms_rows(y, og_ref[...]).astype(o_ref.dtype)


def _ssm_tables(log_dt, a_re, a_im, b_re, b_im, c_re, c_im, chunk):
    f32 = jnp.float32
    n, c = SSM_STATE, SSM_GROUP
    dt = jnp.exp(log_dt.astype(f32))[:, None]
    ar, ai = a_re.astype(f32), a_im.astype(f32)
    mag = jnp.exp(ar * dt)
    lb_re = mag * jnp.cos(ai * dt)
    lb_im = mag * jnp.sin(ai * dt)
    pr, pi_ = lb_re - 1.0, lb_im
    den = ar * ar + ai * ai
    z_re = ((pr * ar + pi_ * ai) / den)[..., None]
    z_im = ((pi_ * ar - pr * ai) / den)[..., None]
    br, bi = b_re.astype(f32), b_im.astype(f32)
    bb_re = z_re * br - z_im * bi
    bb_im = z_re * bi + z_im * br

    def powers(k):
        kk = k.astype(f32)[:, None, None]
        pm = jnp.exp(kk * (ar * dt)[None])
        ph = kk * (ai * dt)[None]
        return pm * jnp.cos(ph), pm * jnp.sin(ph)

    def pair_cols(re, im):
        lead = re.shape[:-1]
        re = re.reshape(*lead, SSM_PAIRS, 1, SSM_PAIR_STATE)
        im = im.reshape(*lead, SSM_PAIRS, 1, SSM_PAIR_STATE)
        return jnp.concatenate([re, im], axis=-2).reshape(*lead, SSM_CL_COLS)

    def table(k):
        re, im = powers(k)
        kk = re.shape[0]
        re = re.reshape(kk, SSM_CLUSTERS, SSM_CLUSTER * n)
        im = im.reshape(kk, SSM_CLUSTERS, SSM_CLUSTER * n)
        return pair_cols(re, im).transpose(1, 0, 2)

    c0 = chunk // 2
    t = jnp.arange(chunk)
    pinv = table(c0 - t)
    pfwd = table(t - c0)
    pcar = table(jnp.array([c0 + 1]))[:, 0, :]

    eye = jnp.eye(SSM_CLUSTER, dtype=f32)
    def bpart(bb):
        b4 = bb.reshape(SSM_CLUSTERS, SSM_CLUSTER, n, c)
        return jnp.einsum('kgnc,gh->kgchn', b4, eye).reshape(
            SSM_CLUSTERS, V7X_LANES, SSM_CLUSTER * n)
    bmat = pair_cols(bpart(bb_re), bpart(bb_im)).astype(jnp.bfloat16)
    def cpart(cc):
        c4 = cc.astype(f32).reshape(SSM_CLUSTERS, SSM_CLUSTER, c, n)
        return jnp.einsum('kgcn,gh->khcgn', c4, eye).reshape(
            SSM_CLUSTERS, V7X_LANES, SSM_CLUSTER * n)
    cmat = pair_cols(cpart(c_re), -cpart(c_im)).transpose(0, 2, 1).astype(jnp.bfloat16)
    return bmat, cmat, pinv, pfwd, pcar


def _ssm(proj, mixed, tables, d_skip, w_glu, b_glu, out_gain, *, batch, seq, chunk=SSM_CHUNK):
    bmat, cmat, pinv, pfwd, pcar = tables
    nc = seq // chunk
    u_col = 3 * ATTN_WIDTH // SSM_WIDTH
    o_col = ATTN_WIDTH // SSM_WIDTH
    kern = functools.partial(_ssm_kernel, chunk=chunk)
    const3 = lambda b, c: (0, 0, 0)
    const2 = lambda b, c: (0, 0)
    vmem = _vmem_limit(2 * (bmat.nbytes + cmat.nbytes + pinv.nbytes + pfwd.nbytes + pcar.nbytes),
                       4 * _nbytes((SSM_WIDTH, SSM_WIDTH), jnp.bfloat16),
                       8 * _nbytes((chunk, SSM_WIDTH), jnp.float32))
    return pl.pallas_call(
        kern,
        out_shape=jax.ShapeDtypeStruct(mixed.shape, mixed.dtype),
        grid=(batch, nc),
        in_specs=[
            pl.BlockSpec((chunk, SSM_WIDTH), lambda b, c: (b * nc + c, u_col)),
            pl.BlockSpec(memory_space=pl.ANY),
            pl.BlockSpec(bmat.shape, const3),
            pl.BlockSpec(cmat.shape, const3),
            pl.BlockSpec(pinv.shape, const3),
            pl.BlockSpec(pfwd.shape, const3),
            pl.BlockSpec(pcar.shape, const2),
            pl.BlockSpec((1, SSM_WIDTH), const2),
            pl.BlockSpec((SSM_WIDTH, SSM_WIDTH), const2),
            pl.BlockSpec((1, SSM_WIDTH), const2),
            pl.BlockSpec((1, SSM_WIDTH), const2),
        ],
        out_specs=pl.BlockSpec((chunk, SSM_WIDTH), lambda b, c: (b * nc + c, o_col)),
        scratch_shapes=[
            pltpu.VMEM((SSM_CLUSTERS, SSM_CL_COLS), jnp.float32),
            pltpu.VMEM((chunk, SSM_WIDTH), jnp.float32),
        ],
        input_output_aliases={1: 0},
        compiler_params=pltpu.CompilerParams(
            dimension_semantics=("arbitrary", "arbitrary"), vmem_limit_bytes=vmem),
        name="s5_ssm",
    )(proj, mixed, bmat, cmat, pinv, pfwd, pcar, d_skip, w_glu, b_glu, out_gain)


def _res_matmul_kernel(a_ref, w_ref, r_ref, o_ref):
    o_ref[...] = r_ref[...] + jnp.dot(a_ref[...], w_ref[...], preferred_element_type=jnp.float32)


def _res_matmul(a, w, res, *, tm, tn, name):
    m, k = a.shape
    n = w.shape[1]
    assert m % tm == 0 and n % tn == 0
    vmem = _vmem_limit(2 * _nbytes((tm, k), a.dtype), 2 * _nbytes((k, tn), w.dtype),
                       5 * _nbytes((tm, tn), jnp.float32))
    return pl.pallas_call(
        _res_matmul_kernel,
        out_shape=jax.ShapeDtypeStruct((m, n), jnp.float32),
        grid=(m // tm, n // tn),
        in_specs=[
            pl.BlockSpec((tm, k), lambda i, j: (i, 0)),
            pl.BlockSpec((k, tn), lambda i, j: (0, j)),
            pl.BlockSpec((tm, tn), lambda i, j: (i, j)),
        ],
        out_specs=pl.BlockSpec((tm, tn), lambda i, j: (i, j)),
        compiler_params=pltpu.CompilerParams(
            dimension_semantics=("parallel", "parallel"), vmem_limit_bytes=vmem),
        name=name,
    )(a, w, res)


def _ffn_up_kernel(x_ref, halo_ref, g_ref, wg_ref, wu_ref, cw_ref, cb_ref, o_ref, h_ref,
                   *, tm, blocks_per_seq):
    i = pl.program_id(0)
    j = pl.program_id(1)
    halo = V7X_BF16_ROWS_PER_TILE

    @pl.when(j == 0)
    def _():
        hh = _rms_rows(halo_ref[...], g_ref[...])
        hh = jnp.where(i % blocks_per_seq == 0, 0.0, hh)
        h_ref[0:halo, :] = hh.astype(h_ref.dtype)
        _norm_into(x_ref, g_ref, h_ref, rows=tm, row_chunk=32, h_offset=halo)

    g_all = jnp.dot(h_ref[...], wg_ref[...], preferred_element_type=jnp.float32)
    up = jnp.dot(h_ref[halo:, :], wu_ref[...], preferred_element_type=jnp.float32)
    rows = 128
    for r0 in range(0, tm, rows):
        for c0 in range(0, o_ref.shape[1], V7X_LANES):
            cs = slice(c0, c0 + V7X_LANES)
            g = g_all[r0:r0 + halo + rows, cs]
            t = cw_ref[1:2, cs] * g + pltpu.roll(cw_ref[0:1, cs] * g, 1, 0)
            gc = (cb_ref[:, cs] + cw_ref[2:3, cs] * g + pltpu.roll(t, 1, 0))[halo:, :]
            act = gc * (1.0 / (1.0 + jnp.exp(-gc))) * up[r0:r0 + rows, cs]
            o_ref[r0:r0 + rows, cs] = act.astype(o_ref.dtype)


def _ffn_up(x1, gain, wg, wu, conv_w, conv_b, *, seq, tm=512, tf=F_TILE):
    m, k = x1.shape
    f = wg.shape[1]
    halo = V7X_BF16_ROWS_PER_TILE
    assert m % tm == 0 and seq % tm == 0
    kern = functools.partial(_ffn_up_kernel, tm=tm, blocks_per_seq=seq // tm)
    vmem = _vmem_limit(2 * _nbytes((tm, k), jnp.float32), _nbytes((tm + halo, k), jnp.bfloat16),
                       4 * _nbytes((k, tf), jnp.bfloat16), 2 * _nbytes((tm, tf), jnp.bfloat16),
                       3 * _nbytes((tm + halo, tf), jnp.float32))
    return pl.pallas_call(
        kern,
        out_shape=jax.ShapeDtypeStruct((m, f), jnp.bfloat16),
        grid=(m // tm, pl.cdiv(f, tf)),
        in_specs=[
            pl.BlockSpec((tm, k), lambda i, j: (i, 0)),
            pl.BlockSpec((halo, k), lambda i, j: (jnp.maximum(i * (tm // halo) - 1, 0), 0)),
            pl.BlockSpec((1, k), lambda i, j: (0, 0)),
            pl.BlockSpec((k, tf), lambda i, j: (0, j)),
            pl.BlockSpec((k, tf), lambda i, j: (0, j)),
            pl.BlockSpec((CONV_WIDTH, tf), lambda i, j: (0, j)),
            pl.BlockSpec((1, tf), lambda i, j: (0, j)),
        ],
        out_specs=pl.BlockSpec((tm, tf), lambda i, j: (i, j)),
        scratch_shapes=[pltpu.VMEM((tm + halo, k), jnp.bfloat16)],
        compiler_params=pltpu.CompilerParams(
            dimension_semantics=("arbitrary", "arbitrary"), vmem_limit_bytes=vmem),
        name="ffn_up",
    )(x1, x1, gain, wg, wu, conv_w, conv_b)


def kernel(x, attn_norm, w_in, q_gain, k_gain, lam_q1, lam_k1, lam_q2, lam_k2, sub_gain, ssm_log_dt, ssm_a_re, ssm_a_im, ssm_b_re, ssm_b_im, ssm_c_re, ssm_c_im, ssm_d, ssm_w_glu, ssm_b_glu, ssm_out_gain, w_out, ffn_norm, w_gate, w_up, conv_w, conv_b, w_down):
    b, s, d = x.shape
    depth = w_in.shape[0]
    bf16, f32 = jnp.bfloat16, jnp.float32
    xf = x.reshape(b * s, d)
    for l in range(depth):
        lam_init = 0.8 - 0.6 * math.exp(-0.3 * l)
        q_scale = HEAD_DIM ** -0.5 * math.log2(math.e)
        qg, kg = q_gain[l].astype(f32), k_gain[l].astype(f32)
        col_gain = jnp.concatenate([
            jnp.tile(qg * q_scale, ATTN_WIDTH // HEAD_DIM),
            jnp.tile(kg, ATTN_WIDTH // HEAD_DIM),
            jnp.ones((IN_WIDTH - 2 * ATTN_WIDTH,), f32)])[None, :]
        proj = _in_proj(xf, attn_norm[l].astype(f32)[None, :], w_in[l].astype(bf16), col_gain)

        logit_bound = 1.01 * HEAD_DIM * q_scale * jnp.max(jnp.abs(qg)) * jnp.max(jnp.abs(kg))
        small_logits = (logit_bound <= _UNSHIFTED_SOFTMAX_MAX_LOG2).astype(jnp.int32).reshape(1)
        lamv = jnp.stack([lam_q1[l], lam_k1[l], lam_q2[l], lam_k2[l]]).astype(f32)
        mixed = _attention(proj, small_logits, lamv, sub_gain[l].astype(f32)[None, :],
                           batch=b, seq=s, lam_init=lam_init)

        tables = _ssm_tables(ssm_log_dt[l], ssm_a_re[l], ssm_a_im[l], ssm_b_re[l], ssm_b_im[l],
                             ssm_c_re[l], ssm_c_im[l], SSM_CHUNK)
        mixed = _ssm(proj, mixed, tables, ssm_d[l].astype(f32).reshape(1, SSM_WIDTH),
                     ssm_w_glu[l].astype(bf16), ssm_b_glu[l].astype(f32)[None, :],
                     ssm_out_gain[l].astype(f32)[None, :], batch=b, seq=s)

        x1 = _res_matmul(mixed, w_out[l].astype(bf16), xf, tm=512, tn=1024, name="out_proj")

        act = _ffn_up(x1, ffn_norm[l].astype(f32)[None, :], w_gate[l].astype(bf16),
                      w_up[l].astype(bf16), conv_w[l].astype(f32), conv_b[l].astype(f32)[None, :],
                      seq=s)
        xf = _res_matmul(act, w_down[l].astype(bf16), x1, tm=512, tn=512, name="ffn_down")
    return xf.reshape(b, s, d)
```

```python
import functools
import math

import jax
import jax.numpy as jnp
from jax import lax
from jax.experimental import pallas as pl
from jax.experimental.pallas import tpu as pltpu

D_MODEL = 4096
ATTN_HEADS = 12
HEAD_DIM = 128
ATTN_WIDTH = ATTN_HEADS * 2 * HEAD_DIM
SSM_WIDTH = D_MODEL - ATTN_WIDTH
SSM_GROUP = 16
SSM_GROUPS = SSM_WIDTH // SSM_GROUP
SSM_STATE = 64
IN_WIDTH = 3 * ATTN_WIDTH + SSM_WIDTH
D_FF = 11008
CONV_WIDTH = 3
EPS = 1e-6

V7X_LANES = 128
V7X_SUBLANES = 8
V7X_BF16_ROWS_PER_TILE = 2 * V7X_SUBLANES
V7X_VMEM_BYTES = 64 * 1024 * 1024
V7X_VMEM_RESERVED_BYTES = 6 * 1024 * 1024

F_TILE = 512
SSM_CLUSTER = V7X_LANES // SSM_GROUP
SSM_CLUSTERS = SSM_GROUPS // SSM_CLUSTER
SSM_PAIRS = 1
SSM_PAIR_STATE = SSM_CLUSTER * SSM_STATE // SSM_PAIRS
SSM_CL_COLS = SSM_PAIRS * 2 * SSM_PAIR_STATE
SSM_CHUNK = 256


def _vmem_limit(*nbytes):
    want = int(sum(nbytes)) + 12 * 1024 * 1024
    return min(want, V7X_VMEM_BYTES - V7X_VMEM_RESERVED_BYTES)


def _nbytes(shape, dtype):
    return math.prod(shape) * jnp.dtype(dtype).itemsize


def _rms_rows(x, gain):
    ms = jnp.mean(x * x, axis=-1, keepdims=True)
    return x * lax.rsqrt(ms + EPS) * gain


def _norm_into(x_ref, gain_ref, h_ref, *, rows, row_chunk, h_offset):
    def body(c, carry):
        r = pl.multiple_of(c * row_chunk, row_chunk)
        xb = x_ref[pl.ds(r, row_chunk), :]
        h_ref[pl.ds(h_offset + r, row_chunk), :] = _rms_rows(xb, gain_ref[...]).astype(h_ref.dtype)
        return carry
    lax.fori_loop(0, rows // row_chunk, body, 0)


def _in_proj_kernel(x_ref, g_ref, w_ref, cg_ref, *refs, tm, tn, n_norm_tiles, n_parts, n_side):
    side_in, o_ref, side_out, h_ref = (refs[:n_side], refs[n_side], refs[n_side + 1:2 * n_side + 1],
                                       refs[2 * n_side + 1])
    j = pl.program_id(1)


    @pl.when(j == 0)
    def _():
        _norm_into(x_ref, g_ref, h_ref, rows=tm, row_chunk=32, h_offset=0)

    is_qk = j < n_norm_tiles
    part = tn // n_parts
    for c0 in range(0, tn, part):
        acc = jnp.dot(h_ref[...], w_ref[:, c0:c0 + part], preferred_element_type=jnp.float32)
        for s0 in range(0, part, HEAD_DIM):
            raw = acc[:, s0:s0 + HEAD_DIM]
            sl = slice(c0 + s0, c0 + s0 + HEAD_DIM)
            o_ref[:, sl] = jnp.where(is_qk, _rms_rows(raw, cg_ref[:, sl]), raw).astype(o_ref.dtype)

    for src, dst in zip(side_in, side_out):
        dst[...] = src[...].astype(dst.dtype)


def _in_proj(x2d, gain, w_bf16, col_gain, side_f32, *, tm=512, tn=1024):
    m, k = x2d.shape
    n = w_bf16.shape[1]
    n_norm_tiles = 2 * ATTN_WIDTH // tn
    n_col = n // tn
    n_steps = (m // tm) * n_col
    assert 2 * ATTN_WIDTH % tn == 0 and m % tm == 0 and n % tn == 0
    kern = functools.partial(_in_proj_kernel, tm=tm, tn=tn, n_norm_tiles=n_norm_tiles, n_parts=4,
                             n_side=len(side_f32))
    side_specs, side_shapes, side_bytes = [], [], 0
    for a in side_f32:
        rows, cols = a.shape
        slab = -(-rows // n_steps)
        slab = -(-slab // V7X_BF16_ROWS_PER_TILE) * V7X_BF16_ROWS_PER_TILE
        assert rows % slab == 0
        last = rows // slab - 1
        side_specs.append(pl.BlockSpec(
            (slab, cols), lambda i, j, last=last: (jnp.minimum(i * n_col + j, last), 0)))
        side_shapes.append(jax.ShapeDtypeStruct(a.shape, jnp.bfloat16))
        side_bytes += 2 * _nbytes((slab, cols), jnp.float32) + 2 * _nbytes((slab, cols), jnp.bfloat16)
    vmem = _vmem_limit(2 * _nbytes((tm, k), jnp.float32), _nbytes((tm, k), jnp.bfloat16),
                       2 * _nbytes((k, tn), jnp.bfloat16), 2 * _nbytes((tm, tn), jnp.bfloat16),
                       _nbytes((tm, tn), jnp.float32), side_bytes)
    outs = pl.pallas_call(
        kern,
        out_shape=[jax.ShapeDtypeStruct((m, n), jnp.bfloat16)] + side_shapes,
        grid=(m // tm, n_col),
        in_specs=[
            pl.BlockSpec((tm, k), lambda i, j: (i, 0)),
            pl.BlockSpec((1, k), lambda i, j: (0, 0)),
            pl.BlockSpec((k, tn), lambda i, j: (0, j)),
            pl.BlockSpec((1, tn), lambda i, j: (0, j)),
        ] + side_specs,
        out_specs=[pl.BlockSpec((tm, tn), lambda i, j: (i, j))] + side_specs,
        scratch_shapes=[pltpu.VMEM((tm, k), jnp.bfloat16)],
        compiler_params=pltpu.CompilerParams(
            dimension_semantics=("arbitrary", "arbitrary"), vmem_limit_bytes=vmem),
        name="in_proj",
    )(x2d, gain, w_bf16, col_gain, *side_f32)
    return outs[0], outs[1:]


_UNSHIFTED_SOFTMAX_MAX_LOG2 = 60.0


def _qk_scores(q, k, row_offset):
    s = lax.dot_general(q, k, (((1,), (1,)), ((), ())), preferred_element_type=jnp.float32)
    if row_offset is not None:
        row = lax.broadcasted_iota(jnp.int32, s.shape, 0)
        col = lax.broadcasted_iota(jnp.int32, s.shape, 1)
        s = jnp.where(col <= row + row_offset, s, -jnp.inf)
    return s


def _lane_partial_sums(p):
    ps = p[:, 0:V7X_LANES]
    for c in range(1, p.shape[1] // V7X_LANES):
        ps = ps + p[:, c * V7X_LANES:(c + 1) * V7X_LANES]
    return ps


def _attn_kernel(small_ref, lamv_ref, q_ref, k_ref, v_ref, sg_ref, o_ref,
                 m_ref, l_ref, acc_ref, p_ref, *, blk, nq, lam_init):
    unshifted = small_ref[0] == 1
    lamv = lamv_ref[...]
    lam = (jnp.exp(jnp.sum(lamv[0:1] * lamv[1:2], axis=-1, keepdims=True))
           - jnp.exp(jnp.sum(lamv[2:3] * lamv[3:4], axis=-1, keepdims=True)) + lam_init)

    def head(mi):
        return slice(mi * HEAD_DIM, (mi + 1) * HEAD_DIM)

    def rows_of(mi):
        return slice(mi * blk, (mi + 1) * blk)

    def q_block(qi, carry):
        qs = pl.ds(pl.multiple_of(qi * blk, blk), blk)
        l_ref[...] = jnp.zeros(l_ref.shape, jnp.float32)
        acc_ref[...] = jnp.zeros(acc_ref.shape, jnp.float32)

        def unshifted_step(start, width, masked):
            ks = pl.ds(start, width)
            for mi in range(2):
                p = jnp.exp2(_qk_scores(q_ref[qs, head(mi)], k_ref[ks, head(mi)],
                                        0 if masked else None))
                l_ref[rows_of(mi), :] += _lane_partial_sums(p)
                p_ref[rows_of(mi), 0:width] = p.astype(p_ref.dtype)
            acc_ref[...] += jnp.dot(p_ref[:, 0:width], v_ref[ks, :],
                                    preferred_element_type=jnp.float32)

        def shifted_step(j, masked):
            ks = pl.ds(pl.multiple_of(j * blk, blk), blk)
            vb = v_ref[ks, :]
            for mi in range(2):
                s = _qk_scores(q_ref[qs, head(mi)], k_ref[ks, head(mi)], 0 if masked else None)
                rs = rows_of(mi)
                m_old = m_ref[mi]
                m_new = jnp.maximum(m_old, jnp.max(s, axis=-1, keepdims=True))
                alpha = jnp.exp2(m_old - m_new)
                p = jnp.exp2(s - m_new)
                l_ref[rs, 0:1] = alpha * l_ref[rs, 0:1] + jnp.sum(p, axis=-1, keepdims=True)
                acc_ref[rs, :] = alpha * acc_ref[rs, :] + jnp.dot(
                    p.astype(vb.dtype), vb, preferred_element_type=jnp.float32)
                m_ref[mi] = m_new

        @pl.when(unshifted)
        def _():
            def pair_body(j, c):
                unshifted_step(pl.multiple_of(j * 2 * blk, 2 * blk), 2 * blk, False)
                return c
            lax.fori_loop(0, qi // 2, pair_body, 0)

            @pl.when(qi % 2 == 1)
            def _():
                unshifted_step(pl.multiple_of((qi - 1) * blk, blk), blk, False)

            unshifted_step(pl.multiple_of(qi * blk, blk), blk, True)

        @pl.when(jnp.logical_not(unshifted))
        def _():
            m_ref[...] = jnp.full(m_ref.shape, -jnp.inf, jnp.float32)

            def body(j, c):
                shifted_step(j, False)
                return c
            lax.fori_loop(0, qi, body, 0)
            shifted_step(qi, True)

        r1 = 1.0 / jnp.sum(l_ref[rows_of(0), :], axis=-1, keepdims=True)
        r2 = lam / jnp.sum(l_ref[rows_of(1), :], axis=-1, keepdims=True)
        o = acc_ref[rows_of(0), :] * r1 - acc_ref[rows_of(1), :] * r2
        o_ref[qs, :] = (_rms_rows(o, sg_ref[...]) * (1.0 - lam_init)).astype(o_ref.dtype)
        return carry

    lax.fori_loop(0, nq, q_block, 0)


def _attention(proj, small_logits, lamv, sub_gain, *, batch, seq, lam_init, blk=512):
    nq = seq // blk
    hw = 2 * HEAD_DIM
    kern = functools.partial(_attn_kernel, blk=blk, nq=nq, lam_init=lam_init)
    vmem = _vmem_limit(8 * _nbytes((seq, hw), jnp.bfloat16),
                       _nbytes((2 * blk, hw), jnp.float32), _nbytes((2 * blk, 2 * blk), jnp.bfloat16),
                       3 * _nbytes((blk, V7X_LANES), jnp.float32),
                       4 * _nbytes((blk, 2 * blk), jnp.float32))
    return pl.pallas_call(
        kern,
        out_shape=jax.ShapeDtypeStruct((batch * seq, D_MODEL), jnp.bfloat16),
        grid=(batch, ATTN_HEADS),
        in_specs=[
            pl.BlockSpec(memory_space=pltpu.SMEM),
            pl.BlockSpec((4, HEAD_DIM), lambda b, h: (0, 0)),
            pl.BlockSpec((seq, hw), lambda b, h: (b, h)),
            pl.BlockSpec((seq, hw), lambda b, h: (b, ATTN_HEADS + h)),
            pl.BlockSpec((seq, hw), lambda b, h: (b, 2 * ATTN_HEADS + h)),
            pl.BlockSpec((1, hw), lambda b, h: (0, 0)),
        ],
        out_specs=pl.BlockSpec((seq, hw), lambda b, h: (b, h)),
        scratch_shapes=[
            pltpu.VMEM((2, blk, 1), jnp.float32),
            pltpu.VMEM((2 * blk, V7X_LANES), jnp.float32),
            pltpu.VMEM((2 * blk, hw), jnp.float32),
            pltpu.VMEM((2 * blk, 2 * blk), jnp.bfloat16),
        ],
        compiler_params=pltpu.CompilerParams(
            dimension_semantics=("arbitrary", "arbitrary"), vmem_limit_bytes=vmem),
        name="diff_attn",
    )(small_logits, lamv, proj, proj, proj, sub_gain)


def _cmul(ar, ai, br, bi):
    return ar * br - ai * bi, ar * bi + ai * br


def _ssm_kernel(u_ref, mixed_ref, bmat_ref, cmat_ref, pinv_ref, pfwd_ref, pcar_ref, d_ref,
                wglu_ref, bglu_ref, og_ref, o_ref, h_ref, y_ref, *, chunk):
    del mixed_ref
    c = pl.program_id(1)

    @pl.when(c == 0)
    def _():
        h_ref[...] = jnp.zeros(h_ref.shape, jnp.float32)

    n = SSM_PAIR_STATE
    row = lax.broadcasted_iota(jnp.int32, (chunk, chunk), 0)
    col = lax.broadcasted_iota(jnp.int32, (chunk, chunk), 1)
    tri = (col <= row).astype(jnp.bfloat16)

    for cl in range(SSM_CLUSTERS):
        ls = slice(cl * V7X_LANES, (cl + 1) * V7X_LANES)
        u_cl = u_ref[:, ls]
        y_cl = None
        for pr in range(SSM_PAIRS):
            re = slice(pr * 2 * n, pr * 2 * n + n)
            im = slice(pr * 2 * n + n, (pr + 1) * 2 * n)
            both = slice(pr * 2 * n, (pr + 1) * 2 * n)
            bu = jnp.dot(u_cl, bmat_ref[cl, :, both], preferred_element_type=jnp.float32)
            xr, xi = _cmul(pinv_ref[cl, :, re], pinv_ref[cl, :, im], bu[:, :n], bu[:, n:])
            x = jnp.concatenate([xr, xi], axis=1).astype(jnp.bfloat16)
            cs = jnp.dot(tri, x, preferred_element_type=jnp.float32)
            cr, ci = _cmul(pcar_ref[cl:cl + 1, re], pcar_ref[cl:cl + 1, im],
                           h_ref[cl:cl + 1, re], h_ref[cl:cl + 1, im])
            hr, hi = _cmul(pfwd_ref[cl, :, re], pfwd_ref[cl, :, im], cs[:, :n] + cr, cs[:, n:] + ci)
            h_ref[cl:cl + 1, re] = hr[chunk - 1:chunk, :]
            h_ref[cl:cl + 1, im] = hi[chunk - 1:chunk, :]
            h = jnp.concatenate([hr, hi], axis=1).astype(jnp.bfloat16)
            part = jnp.dot(h, cmat_ref[cl, both, :], preferred_element_type=jnp.float32)
            y_cl = part if y_cl is None else y_cl + part
        y_ref[:, ls] = y_cl

    y = y_ref[...] + d_ref[...] * u_ref[...].astype(jnp.float32)
    y = jax.nn.gelu(y, approximate=True)
    z = jnp.dot(y.astype(jnp.bfloat16), wglu_ref[...], preferred_element_type=jnp.float32)
    z = z + bglu_ref[...]
    y = y * (1.0 / (1.0 + jnp.exp(-z)))
    o_ref[...] = _rms_rows(y, og_ref[...]).astype(o_ref.dtype)


def _ssm_tables(log_dt, a_re, a_im, b_re, b_im, c_re, c_im, chunk):
    f32 = jnp.float32
    n, c = SSM_STATE, SSM_GROUP
    dt = jnp.exp(log_dt.astype(f32))[:, None]
    ar, ai = a_re.astype(f32), a_im.astype(f32)
    mag = jnp.exp(ar * dt)
    lb_re = mag * jnp.cos(ai * dt)
    lb_im = mag * jnp.sin(ai * dt)
    pr, pi_ = lb_re - 1.0, lb_im
    den = ar * ar + ai * ai
    z_re = ((pr * ar + pi_ * ai) / den)[..., None]
    z_im = ((pi_ * ar - pr * ai) / den)[..., None]
    br, bi = b_re.astype(f32), b_im.astype(f32)
    bb_re = z_re * br - z_im * bi
    bb_im = z_re * bi + z_im * br

    def powers(k):
        kk = k.astype(f32)[:, None, None]
        pm = jnp.exp(kk * (ar * dt)[None])
        ph = kk * (ai * dt)[None]
        return pm * jnp.cos(ph), pm * jnp.sin(ph)

    def pair_cols(re, im):
        lead = re.shape[:-1]
        re = re.reshape(*lead, SSM_PAIRS, 1, SSM_PAIR_STATE)
        im = im.reshape(*lead, SSM_PAIRS, 1, SSM_PAIR_STATE)
        return jnp.concatenate([re, im], axis=-2).reshape(*lead, SSM_CL_COLS)

    def table(k):
        re, im = powers(k)
        kk = re.shape[0]
        re = re.reshape(kk, SSM_CLUSTERS, SSM_CLUSTER * n)
        im = im.reshape(kk, SSM_CLUSTERS, SSM_CLUSTER * n)
        return pair_cols(re, im).transpose(1, 0, 2)

    c0 = chunk // 2
    t = jnp.arange(chunk)
    pinv = table(c0 - t)
    pfwd = table(t - c0)
    pcar = table(jnp.array([c0 + 1]))[:, 0, :]

    eye = jnp.eye(SSM_CLUSTER, dtype=f32)
    def bpart(bb):
        b4 = bb.reshape(SSM_CLUSTERS, SSM_CLUSTER, n, c)
        return jnp.einsum('kgnc,gh->kgchn', b4, eye).reshape(
            SSM_CLUSTERS, V7X_LANES, SSM_CLUSTER * n)
    bmat = pair_cols(bpart(bb_re), bpart(bb_im)).astype(jnp.bfloat16)
    def cpart(cc):
        c4 = cc.astype(f32).reshape(SSM_CLUSTERS, SSM_CLUSTER, c, n)
        return jnp.einsum('kgcn,gh->khcgn', c4, eye).reshape(
            SSM_CLUSTERS, V7X_LANES, SSM_CLUSTER * n)
    cmat = pair_cols(cpart(c_re), -cpart(c_im)).transpose(0, 2, 1).astype(jnp.bfloat16)
    return bmat, cmat, pinv, pfwd, pcar


def _ssm(proj, mixed, tables, d_skip, w_glu, b_glu, out_gain, *, batch, seq, chunk=SSM_CHUNK):
    bmat, cmat, pinv, pfwd, pcar = tables
    nc = seq // chunk
    u_col = 3 * ATTN_WIDTH // SSM_WIDTH
    o_col = ATTN_WIDTH // SSM_WIDTH
    kern = functools.partial(_ssm_kernel, chunk=chunk)
    const3 = lambda b, c: (0, 0, 0)
    const2 = lambda b, c: (0, 0)
    vmem = _vmem_limit(2 * (bmat.nbytes + cmat.nbytes + pinv.nbytes + pfwd.nbytes + pcar.nbytes),
                       4 * _nbytes((SSM_WIDTH, SSM_WIDTH), jnp.bfloat16),
                       8 * _nbytes((chunk, SSM_WIDTH), jnp.float32))
    return pl.pallas_call(
        kern,
        out_shape=jax.ShapeDtypeStruct(mixed.shape, mixed.dtype),
        grid=(batch, nc),
        in_specs=[
            pl.BlockSpec((chunk, SSM_WIDTH), lambda b, c: (b * nc + c, u_col)),
            pl.BlockSpec(memory_space=pl.ANY),
            pl.BlockSpec(bmat.shape, const3),
            pl.BlockSpec(cmat.shape, const3),
            pl.BlockSpec(pinv.shape, const3),
            pl.BlockSpec(pfwd.shape, const3),
            pl.BlockSpec(pcar.shape, const2),
            pl.BlockSpec((1, SSM_WIDTH), const2),
            pl.BlockSpec((SSM_WIDTH, SSM_WIDTH), const2),
            pl.BlockSpec((1, SSM_WIDTH), const2),
            pl.BlockSpec((1, SSM_WIDTH), const2),
        ],
        out_specs=pl.BlockSpec((chunk, SSM_WIDTH), lambda b, c: (b * nc + c, o_col)),
        scratch_shapes=[
            pltpu.VMEM((SSM_CLUSTERS, SSM_CL_COLS), jnp.float32),
            pltpu.VMEM((chunk, SSM_WIDTH), jnp.float32),
        ],
        input_output_aliases={1: 0},
        compiler_params=pltpu.CompilerParams(
            dimension_semantics=("arbitrary", "arbitrary"), vmem_limit_bytes=vmem),
        name="s5_ssm",
    )(proj, mixed, bmat, cmat, pinv, pfwd, pcar, d_skip, w_glu, b_glu, out_gain)


def _res_matmul_kernel(a_ref, w_ref, r_ref, o_ref):
    o_ref[...] = r_ref[...] + jnp.dot(a_ref[...], w_ref[...], preferred_element_type=jnp.float32)


def _res_matmul(a, w, res, *, tm, tn, name):
    m, k = a.shape
    n = w.shape[1]
    assert m % tm == 0 and n % tn == 0
    vmem = _vmem_limit(2 * _nbytes((tm, k), a.dtype), 2 * _nbytes((k, tn), w.dtype),
                       5 * _nbytes((tm, tn), jnp.float32))
    return pl.pallas_call(
        _res_matmul_kernel,
        out_shape=jax.ShapeDtypeStruct((m, n), jnp.float32),
        grid=(m // tm, n // tn),
        in_specs=[
            pl.BlockSpec((tm, k), lambda i, j: (i, 0)),
            pl.BlockSpec((k, tn), lambda i, j: (0, j)),
            pl.BlockSpec((tm, tn), lambda i, j: (i, j)),
        ],
        out_specs=pl.BlockSpec((tm, tn), lambda i, j: (i, j)),
        compiler_params=pltpu.CompilerParams(
            dimension_semantics=("parallel", "parallel"), vmem_limit_bytes=vmem),
        name=name,
    )(a, w, res)


def _ffn_up_kernel(x_ref, halo_ref, g_ref, wg_ref, wu_ref, cw_ref, cb_ref, o_ref, h_ref,
                   *, tm, blocks_per_seq):
    i = pl.program_id(0)
    j = pl.program_id(1)
    halo = V7X_BF16_ROWS_PER_TILE

    @pl.when(j == 0)
    def _():
        hh = _rms_rows(halo_ref[...], g_ref[...])
        hh = jnp.where(i % blocks_per_seq == 0, 0.0, hh)
        h_ref[0:halo, :] = hh.astype(h_ref.dtype)
        _norm_into(x_ref, g_ref, h_ref, rows=tm, row_chunk=32, h_offset=halo)

    g_all = jnp.dot(h_ref[...], wg_ref[...], preferred_element_type=jnp.float32)
    up = jnp.dot(h_ref[halo:, :], wu_ref[...], preferred_element_type=jnp.float32)
    rows = 128
    for r0 in range(0, tm, rows):
        for c0 in range(0, o_ref.shape[1], V7X_LANES):
            cs = slice(c0, c0 + V7X_LANES)
            g = g_all[r0:r0 + halo + rows, cs]
            t = cw_ref[1:2, cs] * g + pltpu.roll(cw_ref[0:1, cs] * g, 1, 0)
            gc = (cb_ref[:, cs] + cw_ref[2:3, cs] * g + pltpu.roll(t, 1, 0))[halo:, :]
            act = gc * (1.0 / (1.0 + jnp.exp(-gc))) * up[r0:r0 + rows, cs]
            o_ref[r0:r0 + rows, cs] = act.astype(o_ref.dtype)


def _ffn_up(x1, gain, wg, wu, conv_w, conv_b, *, seq, tm=512, tf=F_TILE):
    m, k = x1.shape
    f = wg.shape[1]
    halo = V7X_BF16_ROWS_PER_TILE
    assert m % tm == 0 and seq % tm == 0
    kern = functools.partial(_ffn_up_kernel, tm=tm, blocks_per_seq=seq // tm)
    vmem = _vmem_limit(2 * _nbytes((tm, k), jnp.float32), _nbytes((tm + halo, k), jnp.bfloat16),
                       4 * _nbytes((k, tf), jnp.bfloat16), 2 * _nbytes((tm, tf), jnp.bfloat16),
                       3 * _nbytes((tm + halo, tf), jnp.float32))
    return pl.pallas_call(
        kern,
        out_shape=jax.ShapeDtypeStruct((m, f), jnp.bfloat16),
        grid=(m // tm, pl.cdiv(f, tf)),
        in_specs=[
            pl.BlockSpec((tm, k), lambda i, j: (i, 0)),
            pl.BlockSpec((halo, k), lambda i, j: (jnp.maximum(i * (tm // halo) - 1, 0), 0)),
            pl.BlockSpec((1, k), lambda i, j: (0, 0)),
            pl.BlockSpec((k, tf), lambda i, j: (0, j)),
            pl.BlockSpec((k, tf), lambda i, j: (0, j)),
            pl.BlockSpec((CONV_WIDTH, tf), lambda i, j: (0, j)),
            pl.BlockSpec((1, tf), lambda i, j: (0, j)),
        ],
        out_specs=pl.BlockSpec((tm, tf), lambda i, j: (i, j)),
        scratch_shapes=[pltpu.VMEM((tm + halo, k), jnp.bfloat16)],
        compiler_params=pltpu.CompilerParams(
            dimension_semantics=("arbitrary", "arbitrary"), vmem_limit_bytes=vmem),
        name="ffn_up",
    )(x1, x1, gain, wg, wu, conv_w, conv_b)


def kernel(x, attn_norm, w_in, q_gain, k_gain, lam_q1, lam_k1, lam_q2, lam_k2, sub_gain, ssm_log_dt, ssm_a_re, ssm_a_im, ssm_b_re, ssm_b_im, ssm_c_re, ssm_c_im, ssm_d, ssm_w_glu, ssm_b_glu, ssm_out_gain, w_out, ffn_norm, w_gate, w_up, conv_w, conv_b, w_down):
    b, s, d = x.shape
    depth = w_in.shape[0]
    bf16, f32 = jnp.bfloat16, jnp.float32
    xf = x.reshape(b * s, d)
    for l in range(depth):
        lam_init = 0.8 - 0.6 * math.exp(-0.3 * l)
        q_scale = HEAD_DIM ** -0.5 * math.log2(math.e)
        qg, kg = q_gain[l].astype(f32), k_gain[l].astype(f32)
        col_gain = jnp.concatenate([
            jnp.tile(qg * q_scale, ATTN_WIDTH // HEAD_DIM),
            jnp.tile(kg, ATTN_WIDTH // HEAD_DIM),
            jnp.ones((IN_WIDTH - 2 * ATTN_WIDTH,), f32)])[None, :]
        side = (w_out[l], w_gate[l], w_up[l], w_down[l].reshape(d, -1))
        proj, (wo_bf, wg_bf, wu_bf, wd_bf) = _in_proj(
            xf, attn_norm[l].astype(f32)[None, :], w_in[l].astype(bf16), col_gain, side)
        wd_bf = wd_bf.reshape(w_down[l].shape)

        logit_bound = 1.01 * HEAD_DIM * q_scale * jnp.max(jnp.abs(qg)) * jnp.max(jnp.abs(kg))
        small_logits = (logit_bound <= _UNSHIFTED_SOFTMAX_MAX_LOG2).astype(jnp.int32).reshape(1)
        lamv = jnp.stack([lam_q1[l], lam_k1[l], lam_q2[l], lam_k2[l]]).astype(f32)
        mixed = _attention(proj, small_logits, lamv, sub_gain[l].astype(f32)[None, :],
                           batch=b, seq=s, lam_init=lam_init)

        tables = _ssm_tables(ssm_log_dt[l], ssm_a_re[l], ssm_a_im[l], ssm_b_re[l], ssm_b_im[l],
                             ssm_c_re[l], ssm_c_im[l], SSM_CHUNK)
        mixed = _ssm(proj, mixed, tables, ssm_d[l].astype(f32).reshape(1, SSM_WIDTH),
                     ssm_w_glu[l].astype(bf16), ssm_b_glu[l].astype(f32)[None, :],
                     ssm_out_gain[l].astype(f32)[None, :], batch=b, seq=s)

        x1 = _res_matmul(mixed, wo_bf, xf, tm=512, tn=1024, name="out_proj")

        act = _ffn_up(x1, ffn_norm[l].astype(f32)[None, :], wg_bf, wu_bf, conv_w[l].astype(f32),
                      conv_b[l].astype(f32)[None, :], seq=s)
        xf = _res_matmul(act, wd_bf, x1, tm=512, tn=512, name="ffn_down")
    return xf.reshape(b, s, d)
```

```python
import functools
import math

import jax
import jax.numpy as jnp
from jax import lax
from jax.experimental import pallas as pl
from jax.experimental.pallas import tpu as pltpu

D_MODEL = 4096
ATTN_HEADS = 12
HEAD_DIM = 128
ATTN_WIDTH = ATTN_HEADS * 2 * HEAD_DIM
SSM_WIDTH = D_MODEL - ATTN_WIDTH
SSM_GROUP = 16
SSM_GROUPS = SSM_WIDTH // SSM_GROUP
SSM_STATE = 64
IN_WIDTH = 3 * ATTN_WIDTH + SSM_WIDTH
D_FF = 11008
CONV_WIDTH = 3
EPS = 1e-6

V7X_LANES = 128
V7X_SUBLANES = 8
V7X_BF16_ROWS_PER_TILE = 2 * V7X_SUBLANES
V7X_VMEM_BYTES = 64 * 1024 * 1024
V7X_VMEM_RESERVED_BYTES = 6 * 1024 * 1024

F_TILE = 512
SSM_CLUSTER = V7X_LANES // SSM_GROUP
SSM_CLUSTERS = SSM_GROUPS // SSM_CLUSTER
SSM_PAIRS = 1
SSM_PAIR_STATE = SSM_CLUSTER * SSM_STATE // SSM_PAIRS
SSM_CL_COLS = SSM_PAIRS * 2 * SSM_PAIR_STATE
SSM_CHUNK = 256


def _vmem_limit(*nbytes):
    want = int(sum(nbytes)) + 12 * 1024 * 1024
    return min(want, V7X_VMEM_BYTES - V7X_VMEM_RESERVED_BYTES)


def _nbytes(shape, dtype):
    return math.prod(shape) * jnp.dtype(dtype).itemsize


def _rms_rows(x, gain):
    ms = jnp.mean(x * x, axis=-1, keepdims=True)
    return x * lax.rsqrt(ms + EPS) * gain


def _norm_into(x_ref, gain_ref, h_ref, *, rows, row_chunk, h_offset):
    def body(c, carry):
        r = pl.multiple_of(c * row_chunk, row_chunk)
        xb = x_ref[pl.ds(r, row_chunk), :]
        h_ref[pl.ds(h_offset + r, row_chunk), :] = _rms_rows(xb, gain_ref[...]).astype(h_ref.dtype)
        return carry
    lax.fori_loop(0, rows // row_chunk, body, 0)


def _in_proj_kernel(x_ref, g_ref, w_ref, cg_ref, *refs, tm, tn, n_norm_tiles, n_parts, n_side):
    side_in, o_ref, side_out, h_ref = (refs[:n_side], refs[n_side], refs[n_side + 1:2 * n_side + 1],
                                       refs[2 * n_side + 1])
    j = pl.program_id(1)


    @pl.when(j == 0)
    def _():
        _norm_into(x_ref, g_ref, h_ref, rows=tm, row_chunk=32, h_offset=0)

    is_qk = j < n_norm_tiles
    part = tn // n_parts
    for c0 in range(0, tn, part):
        acc = jnp.dot(h_ref[...], w_ref[:, c0:c0 + part], preferred_element_type=jnp.float32)
        for s0 in range(0, part, HEAD_DIM):
            raw = acc[:, s0:s0 + HEAD_DIM]
            sl = slice(c0 + s0, c0 + s0 + HEAD_DIM)
            o_ref[:, sl] = jnp.where(is_qk, _rms_rows(raw, cg_ref[:, sl]), raw).astype(o_ref.dtype)

    for src, dst in zip(side_in, side_out):
        dst[...] = src[...].astype(dst.dtype)


def _in_proj(x2d, gain, w_bf16, col_gain, side_f32, layer, *, tm=512, tn=1024):
    m, k = x2d.shape
    n = w_bf16.shape[1]
    n_norm_tiles = 2 * ATTN_WIDTH // tn
    n_col = n // tn
    n_steps = (m // tm) * n_col
    assert 2 * ATTN_WIDTH % tn == 0 and m % tm == 0 and n % tn == 0
    kern = functools.partial(_in_proj_kernel, tm=tm, tn=tn, n_norm_tiles=n_norm_tiles, n_parts=4,
                             n_side=len(side_f32))
    side_in_specs, side_out_specs, side_shapes, side_bytes = [], [], [], 0
    for a in side_f32:
        _, rows, cols = a.shape
        tile = V7X_BF16_ROWS_PER_TILE
        slab = next(s for s in range(tile, rows + 1, tile)
                    if rows % s == 0 and rows // s <= n_steps)
        last = rows // slab - 1
        side_in_specs.append(pl.BlockSpec(
            (None, slab, cols),
            lambda i, j, last=last: (layer, jnp.minimum(i * n_col + j, last), 0)))
        side_out_specs.append(pl.BlockSpec(
            (slab, cols), lambda i, j, last=last: (jnp.minimum(i * n_col + j, last), 0)))
        side_shapes.append(jax.ShapeDtypeStruct((rows, cols), jnp.bfloat16))
        side_bytes += 2 * _nbytes((slab, cols), jnp.float32) + 2 * _nbytes((slab, cols), jnp.bfloat16)
    vmem = _vmem_limit(2 * _nbytes((tm, k), jnp.float32), _nbytes((tm, k), jnp.bfloat16),
                       2 * _nbytes((k, tn), jnp.bfloat16), 2 * _nbytes((tm, tn), jnp.bfloat16),
                       _nbytes((tm, tn), jnp.float32), side_bytes)
    outs = pl.pallas_call(
        kern,
        out_shape=[jax.ShapeDtypeStruct((m, n), jnp.bfloat16)] + side_shapes,
        grid=(m // tm, n_col),
        in_specs=[
            pl.BlockSpec((tm, k), lambda i, j: (i, 0)),
            pl.BlockSpec((1, k), lambda i, j: (0, 0)),
            pl.BlockSpec((k, tn), lambda i, j: (0, j)),
            pl.BlockSpec((1, tn), lambda i, j: (0, j)),
        ] + side_in_specs,
        out_specs=[pl.BlockSpec((tm, tn), lambda i, j: (i, j))] + side_out_specs,
        scratch_shapes=[pltpu.VMEM((tm, k), jnp.bfloat16)],
        compiler_params=pltpu.CompilerParams(
            dimension_semantics=("arbitrary", "arbitrary"), vmem_limit_bytes=vmem),
        name="in_proj",
    )(x2d, gain, w_bf16, col_gain, *side_f32)
    return outs[0], outs[1:]


_UNSHIFTED_SOFTMAX_MAX_LOG2 = 60.0


def _qk_scores(q, k, row_offset):
    s = lax.dot_general(q, k, (((1,), (1,)), ((), ())), preferred_element_type=jnp.float32)
    if row_offset is not None:
        row = lax.broadcasted_iota(jnp.int32, s.shape, 0)
        col = lax.broadcasted_iota(jnp.int32, s.shape, 1)
        s = jnp.where(col <= row + row_offset, s, -jnp.inf)
    return s


def _lane_partial_sums(p):
    ps = p[:, 0:V7X_LANES]
    for c in range(1, p.shape[1] // V7X_LANES):
        ps = ps + p[:, c * V7X_LANES:(c + 1) * V7X_LANES]
    return ps


def _attn_kernel(small_ref, lamv_ref, q_ref, k_ref, v_ref, sg_ref, o_ref,
                 m_ref, l_ref, acc_ref, p_ref, *, blk, nq, lam_init):
    unshifted = small_ref[0] == 1
    lamv = lamv_ref[...]
    lam = (jnp.exp(jnp.sum(lamv[0:1] * lamv[1:2], axis=-1, keepdims=True))
           - jnp.exp(jnp.sum(lamv[2:3] * lamv[3:4], axis=-1, keepdims=True)) + lam_init)

    def head(mi):
        return slice(mi * HEAD_DIM, (mi + 1) * HEAD_DIM)

    def rows_of(mi):
        return slice(mi * blk, (mi + 1) * blk)

    def q_block(qi, carry):
        qs = pl.ds(pl.multiple_of(qi * blk, blk), blk)
        l_ref[...] = jnp.zeros(l_ref.shape, jnp.float32)
        acc_ref[...] = jnp.zeros(acc_ref.shape, jnp.float32)

        def unshifted_step(start, width, masked):
            ks = pl.ds(start, width)
            for mi in range(2):
                p = jnp.exp2(_qk_scores(q_ref[qs, head(mi)], k_ref[ks, head(mi)],
                                        0 if masked else None))
                l_ref[rows_of(mi), :] += _lane_partial_sums(p)
                p_ref[rows_of(mi), 0:width] = p.astype(p_ref.dtype)
            acc_ref[...] += jnp.dot(p_ref[:, 0:width], v_ref[ks, :],
                                    preferred_element_type=jnp.float32)

        def shifted_step(j, masked):
            ks = pl.ds(pl.multiple_of(j * blk, blk), blk)
            vb = v_ref[ks, :]
            for mi in range(2):
                s = _qk_scores(q_ref[qs, head(mi)], k_ref[ks, head(mi)], 0 if masked else None)
                rs = rows_of(mi)
                m_old = m_ref[mi]
                m_new = jnp.maximum(m_old, jnp.max(s, axis=-1, keepdims=True))
                alpha = jnp.exp2(m_old - m_new)
                p = jnp.exp2(s - m_new)
                l_ref[rs, 0:1] = alpha * l_ref[rs, 0:1] + jnp.sum(p, axis=-1, keepdims=True)
                acc_ref[rs, :] = alpha * acc_ref[rs, :] + jnp.dot(
                    p.astype(vb.dtype), vb, preferred_element_type=jnp.float32)
                m_ref[mi] = m_new

        @pl.when(unshifted)
        def _():
            def pair_body(j, c):
                unshifted_step(pl.multiple_of(j * 2 * blk, 2 * blk), 2 * blk, False)
                return c
            lax.fori_loop(0, qi // 2, pair_body, 0)

            @pl.when(qi % 2 == 1)
            def _():
                unshifted_step(pl.multiple_of((qi - 1) * blk, blk), blk, False)

            unshifted_step(pl.multiple_of(qi * blk, blk), blk, True)

        @pl.when(jnp.logical_not(unshifted))
        def _():
            m_ref[...] = jnp.full(m_ref.shape, -jnp.inf, jnp.float32)

            def body(j, c):
                shifted_step(j, False)
                return c
            lax.fori_loop(0, qi, body, 0)
            shifted_step(qi, True)

        r1 = 1.0 / jnp.sum(l_ref[rows_of(0), :], axis=-1, keepdims=True)
        r2 = lam / jnp.sum(l_ref[rows_of(1), :], axis=-1, keepdims=True)
        o = acc_ref[rows_of(0), :] * r1 - acc_ref[rows_of(1), :] * r2
        o_ref[qs, :] = (_rms_rows(o, sg_ref[...]) * (1.0 - lam_init)).astype(o_ref.dtype)
        return carry

    lax.fori_loop(0, nq, q_block, 0)


def _attention(proj, small_logits, lamv, sub_gain, *, batch, seq, lam_init, blk=512):
    nq = seq // blk
    hw = 2 * HEAD_DIM
    kern = functools.partial(_attn_kernel, blk=blk, nq=nq, lam_init=lam_init)
    vmem = _vmem_limit(8 * _nbytes((seq, hw), jnp.bfloat16),
                       _nbytes((2 * blk, hw), jnp.float32), _nbytes((2 * blk, 2 * blk), jnp.bfloat16),
                       3 * _nbytes((blk, V7X_LANES), jnp.float32),
                       4 * _nbytes((blk, 2 * blk), jnp.float32))
    return pl.pallas_call(
        kern,
        out_shape=jax.ShapeDtypeStruct((batch * seq, D_MODEL), jnp.bfloat16),
        grid=(batch, ATTN_HEADS),
        in_specs=[
            pl.BlockSpec(memory_space=pltpu.SMEM),
            pl.BlockSpec((4, HEAD_DIM), lambda b, h: (0, 0)),
            pl.BlockSpec((seq, hw), lambda b, h: (b, h)),
            pl.BlockSpec((seq, hw), lambda b, h: (b, ATTN_HEADS + h)),
            pl.BlockSpec((seq, hw), lambda b, h: (b, 2 * ATTN_HEADS + h)),
            pl.BlockSpec((1, hw), lambda b, h: (0, 0)),
        ],
        out_specs=pl.BlockSpec((seq, hw), lambda b, h: (b, h)),
        scratch_shapes=[
            pltpu.VMEM((2, blk, 1), jnp.float32),
            pltpu.VMEM((2 * blk, V7X_LANES), jnp.float32),
            pltpu.VMEM((2 * blk, hw), jnp.float32),
            pltpu.VMEM((2 * blk, 2 * blk), jnp.bfloat16),
        ],
        compiler_params=pltpu.CompilerParams(
            dimension_semantics=("arbitrary", "arbitrary"), vmem_limit_bytes=vmem),
        name="diff_attn",
    )(small_logits, lamv, proj, proj, proj, sub_gain)


def _cmul(ar, ai, br, bi):
    return ar * br - ai * bi, ar * bi + ai * br


def _ssm_kernel(u_ref, mixed_ref, bmat_ref, cmat_ref, pinv_ref, pfwd_ref, pcar_ref, d_ref,
                wglu_ref, bglu_ref, og_ref, o_ref, h_ref, y_ref, *, chunk):
    del mixed_ref
    c = pl.program_id(1)

    @pl.when(c == 0)
    def _():
        h_ref[...] = jnp.zeros(h_ref.shape, jnp.float32)

    n = SSM_PAIR_STATE
    row = lax.broadcasted_iota(jnp.int32, (chunk, chunk), 0)
    col = lax.broadcasted_iota(jnp.int32, (chunk, chunk), 1)
    tri = (col <= row).astype(jnp.bfloat16)

    for cl in range(SSM_CLUSTERS):
        ls = slice(cl * V7X_LANES, (cl + 1) * V7X_LANES)
        u_cl = u_ref[:, ls]
        y_cl = None
        for pr in range(SSM_PAIRS):
            re = slice(pr * 2 * n, pr * 2 * n + n)
            im = slice(pr * 2 * n + n, (pr + 1) * 2 * n)
            both = slice(pr * 2 * n, (pr + 1) * 2 * n)
            bu = jnp.dot(u_cl, bmat_ref[cl, :, both], preferred_element_type=jnp.float32)
            xr, xi = _cmul(pinv_ref[cl, :, re], pinv_ref[cl, :, im], bu[:, :n], bu[:, n:])
            x = jnp.concatenate([xr, xi], axis=1).astype(jnp.bfloat16)
            cs = jnp.dot(tri, x, preferred_element_type=jnp.float32)
            cr, ci = _cmul(pcar_ref[cl:cl + 1, re], pcar_ref[cl:cl + 1, im],
                           h_ref[cl:cl + 1, re], h_ref[cl:cl + 1, im])
            hr, hi = _cmul(pfwd_ref[cl, :, re], pfwd_ref[cl, :, im], cs[:, :n] + cr, cs[:, n:] + ci)
            h_ref[cl:cl + 1, re] = hr[chunk - 1:chunk, :]
            h_ref[cl:cl + 1, im] = hi[chunk - 1:chunk, :]
            h = jnp.concatenate([hr, hi], axis=1).astype(jnp.bfloat16)
            part = jnp.dot(h, cmat_ref[cl, both, :], preferred_element_type=jnp.float32)
            y_cl = part if y_cl is None else y_cl + part
        y_ref[:, ls] = y_cl

    y = y_ref[...] + d_ref[...] * u_ref[...].astype(jnp.float32)
    y = jax.nn.gelu(y, approximate=True)
    z = jnp.dot(y.astype(jnp.bfloat16), wglu_ref[...], preferred_element_type=jnp.float32)
    z = z + bglu_ref[...]
    y = y * (1.0 / (1.0 + jnp.exp(-z)))
    o_ref[...] = _rms_rows(y, og_ref[...]).astype(o_ref.dtype)


def _ssm_tables(log_dt, a_re, a_im, b_re, b_im, c_re, c_im, chunk):
    f32 = jnp.float32
    n, c = SSM_STATE, SSM_GROUP
    dt = jnp.exp(log_dt.astype(f32))[:, None]
    ar, ai = a_re.astype(f32), a_im.astype(f32)
    mag = jnp.exp(ar * dt)
    lb_re = mag * jnp.cos(ai * dt)
    lb_im = mag * jnp.sin(ai * dt)
    pr, pi_ = lb_re - 1.0, lb_im
    den = ar * ar + ai * ai
    z_re = ((pr * ar + pi_ * ai) / den)[..., None]
    z_im = ((pi_ * ar - pr * ai) / den)[..., None]
    br, bi = b_re.astype(f32), b_im.astype(f32)
    bb_re = z_re * br - z_im * bi
    bb_im = z_re * bi + z_im * br

    def powers(k):
        kk = k.astype(f32)[:, None, None]
        pm = jnp.exp(kk * (ar * dt)[None])
        ph = kk * (ai * dt)[None]
        return pm * jnp.cos(ph), pm * jnp.sin(ph)

    def pair_cols(re, im):
        lead = re.shape[:-1]
        re = re.reshape(*lead, SSM_PAIRS, 1, SSM_PAIR_STATE)
        im = im.reshape(*lead, SSM_PAIRS, 1, SSM_PAIR_STATE)
        return jnp.concatenate([re, im], axis=-2).reshape(*lead, SSM_CL_COLS)

    def table(k):
        re, im = powers(k)
        kk = re.shape[0]
        re = re.reshape(kk, SSM_CLUSTERS, SSM_CLUSTER * n)
        im = im.reshape(kk, SSM_CLUSTERS, SSM_CLUSTER * n)
        return pair_cols(re, im).transpose(1, 0, 2)

    c0 = chunk // 2
    t = jnp.arange(chunk)
    pinv = table(c0 - t)
    pfwd = table(t - c0)
    pcar = table(jnp.array([c0 + 1]))[:, 0, :]

    eye = jnp.eye(SSM_CLUSTER, dtype=f32)
    def bpart(bb):
        b4 = bb.reshape(SSM_CLUSTERS, SSM_CLUSTER, n, c)
        return jnp.einsum('kgnc,gh->kgchn', b4, eye).reshape(
            SSM_CLUSTERS, V7X_LANES, SSM_CLUSTER * n)
    bmat = pair_cols(bpart(bb_re), bpart(bb_im)).astype(jnp.bfloat16)
    def cpart(cc):
        c4 = cc.astype(f32).reshape(SSM_CLUSTERS, SSM_CLUSTER, c, n)
        return jnp.einsum('kgcn,gh->khcgn', c4, eye).reshape(
            SSM_CLUSTERS, V7X_LANES, SSM_CLUSTER * n)
    cmat = pair_cols(cpart(c_re), -cpart(c_im)).transpose(0, 2, 1).astype(jnp.bfloat16)
    return bmat, cmat, pinv, pfwd, pcar


def _ssm(proj, mixed, tables, d_skip, w_glu, b_glu, out_gain, *, batch, seq, chunk=SSM_CHUNK):
    bmat, cmat, pinv, pfwd, pcar = tables
    nc = seq // chunk
    u_col = 3 * ATTN_WIDTH // SSM_WIDTH
    o_col = ATTN_WIDTH // SSM_WIDTH
    kern = functools.partial(_ssm_kernel, chunk=chunk)
    const3 = lambda b, c: (0, 0, 0)
    const2 = lambda b, c: (0, 0)
    vmem = _vmem_limit(2 * (bmat.nbytes + cmat.nbytes + pinv.nbytes + pfwd.nbytes + pcar.nbytes),
                       4 * _nbytes((SSM_WIDTH, SSM_WIDTH), jnp.bfloat16),
                       8 * _nbytes((chunk, SSM_WIDTH), jnp.float32))
    return pl.pallas_call(
        kern,
        out_shape=jax.ShapeDtypeStruct(mixed.shape, mixed.dtype),
        grid=(batch, nc),
        in_specs=[
            pl.BlockSpec((chunk, SSM_WIDTH), lambda b, c: (b * nc + c, u_col)),
            pl.BlockSpec(memory_space=pl.ANY),
            pl.BlockSpec(bmat.shape, const3),
            pl.BlockSpec(cmat.shape, const3),
            pl.BlockSpec(pinv.shape, const3),
            pl.BlockSpec(pfwd.shape, const3),
            pl.BlockSpec(pcar.shape, const2),
            pl.BlockSpec((1, SSM_WIDTH), const2),
            pl.BlockSpec((SSM_WIDTH, SSM_WIDTH), const2),
            pl.BlockSpec((1, SSM_WIDTH), const2),
            pl.BlockSpec((1, SSM_WIDTH), const2),
        ],
        out_specs=pl.BlockSpec((chunk, SSM_WIDTH), lambda b, c: (b * nc + c, o_col)),
        scratch_shapes=[
            pltpu.VMEM((SSM_CLUSTERS, SSM_CL_COLS), jnp.float32),
            pltpu.VMEM((chunk, SSM_WIDTH), jnp.float32),
        ],
        input_output_aliases={1: 0},
        compiler_params=pltpu.CompilerParams(
            dimension_semantics=("arbitrary", "arbitrary"), vmem_limit_bytes=vmem),
        name="s5_ssm",
    )(proj, mixed, bmat, cmat, pinv, pfwd, pcar, d_skip, w_glu, b_glu, out_gain)


def _res_matmul_kernel(a_ref, w_ref, r_ref, o_ref):
    o_ref[...] = r_ref[...] + jnp.dot(a_ref[...], w_ref[...], preferred_element_type=jnp.float32)


def _res_matmul(a, w, res, *, tm, tn, name):
    m, k = a.shape
    n = w.shape[1]
    assert m % tm == 0 and n % tn == 0
    vmem = _vmem_limit(2 * _nbytes((tm, k), a.dtype), 2 * _nbytes((k, tn), w.dtype),
                       5 * _nbytes((tm, tn), jnp.float32))
    return pl.pallas_call(
        _res_matmul_kernel,
        out_shape=jax.ShapeDtypeStruct((m, n), jnp.float32),
        grid=(m // tm, n // tn),
        in_specs=[
            pl.BlockSpec((tm, k), lambda i, j: (i, 0)),
            pl.BlockSpec((k, tn), lambda i, j: (0, j)),
            pl.BlockSpec((tm, tn), lambda i, j: (i, j)),
        ],
        out_specs=pl.BlockSpec((tm, tn), lambda i, j: (i, j)),
        compiler_params=pltpu.CompilerParams(
            dimension_semantics=("parallel", "parallel"), vmem_limit_bytes=vmem),
        name=name,
    )(a, w, res)


def _ffn_up_kernel(x_ref, halo_ref, g_ref, wg_ref, wu_ref, cw_ref, cb_ref, o_ref, h_ref,
                   *, tm, blocks_per_seq):
    i = pl.program_id(0)
    j = pl.program_id(1)
    halo = V7X_BF16_ROWS_PER_TILE

    @pl.when(j == 0)
    def _():
        hh = _rms_rows(halo_ref[...], g_ref[...])
        hh = jnp.where(i % blocks_per_seq == 0, 0.0, hh)
        h_ref[0:halo, :] = hh.astype(h_ref.dtype)
        _norm_into(x_ref, g_ref, h_ref, rows=tm, row_chunk=32, h_offset=halo)

    g_all = jnp.dot(h_ref[...], wg_ref[...], preferred_element_type=jnp.float32)
    up = jnp.dot(h_ref[halo:, :], wu_ref[...], preferred_element_type=jnp.float32)
    rows = 128
    for r0 in range(0, tm, rows):
        for c0 in range(0, o_ref.shape[1], V7X_LANES):
            cs = slice(c0, c0 + V7X_LANES)
            g = g_all[r0:r0 + halo + rows, cs]
            t = cw_ref[1:2, cs] * g + pltpu.roll(cw_ref[0:1, cs] * g, 1, 0)
            gc = (cb_ref[:, cs] + cw_ref[2:3, cs] * g + pltpu.roll(t, 1, 0))[halo:, :]
            act = gc * (1.0 / (1.0 + jnp.exp(-gc))) * up[r0:r0 + rows, cs]
            o_ref[r0:r0 + rows, cs] = act.astype(o_ref.dtype)


def _ffn_up(x1, gain, wg, wu, conv_w, conv_b, *, seq, tm=512, tf=F_TILE):
    m, k = x1.shape
    f = wg.shape[1]
    halo = V7X_BF16_ROWS_PER_TILE
    assert m % tm == 0 and seq % tm == 0
    kern = functools.partial(_ffn_up_kernel, tm=tm, blocks_per_seq=seq // tm)
    vmem = _vmem_limit(2 * _nbytes((tm, k), jnp.float32), _nbytes((tm + halo, k), jnp.bfloat16),
                       4 * _nbytes((k, tf), jnp.bfloat16), 2 * _nbytes((tm, tf), jnp.bfloat16),
                       3 * _nbytes((tm + halo, tf), jnp.float32))
    return pl.pallas_call(
        kern,
        out_shape=jax.ShapeDtypeStruct((m, f), jnp.bfloat16),
        grid=(m // tm, pl.cdiv(f, tf)),
        in_specs=[
            pl.BlockSpec((tm, k), lambda i, j: (i, 0)),
            pl.BlockSpec((halo, k), lambda i, j: (jnp.maximum(i * (tm // halo) - 1, 0), 0)),
            pl.BlockSpec((1, k), lambda i, j: (0, 0)),
            pl.BlockSpec((k, tf), lambda i, j: (0, j)),
            pl.BlockSpec((k, tf), lambda i, j: (0, j)),
            pl.BlockSpec((CONV_WIDTH, tf), lambda i, j: (0, j)),
            pl.BlockSpec((1, tf), lambda i, j: (0, j)),
        ],
        out_specs=pl.BlockSpec((tm, tf), lambda i, j: (i, j)),
        scratch_shapes=[pltpu.VMEM((tm + halo, k), jnp.bfloat16)],
        compiler_params=pltpu.CompilerParams(
            dimension_semantics=("arbitrary", "arbitrary"), vmem_limit_bytes=vmem),
        name="ffn_up",
    )(x1, x1, gain, wg, wu, conv_w, conv_b)


def kernel(x, attn_norm, w_in, q_gain, k_gain, lam_q1, lam_k1, lam_q2, lam_k2, sub_gain, ssm_log_dt, ssm_a_re, ssm_a_im, ssm_b_re, ssm_b_im, ssm_c_re, ssm_c_im, ssm_d, ssm_w_glu, ssm_b_glu, ssm_out_gain, w_out, ffn_norm, w_gate, w_up, conv_w, conv_b, w_down):
    b, s, d = x.shape
    depth = w_in.shape[0]
    bf16, f32 = jnp.bfloat16, jnp.float32
    xf = x.reshape(b * s, d)
    for l in range(depth):
        lam_init = 0.8 - 0.6 * math.exp(-0.3 * l)
        q_scale = HEAD_DIM ** -0.5 * math.log2(math.e)
        qg, kg = q_gain[l].astype(f32), k_gain[l].astype(f32)
        col_gain = jnp.concatenate([
            jnp.tile(qg * q_scale, ATTN_WIDTH // HEAD_DIM),
            jnp.tile(kg, ATTN_WIDTH // HEAD_DIM),
            jnp.ones((IN_WIDTH - 2 * ATTN_WIDTH,), f32)])[None, :]
        proj, (wo_bf, wg_bf, wu_bf, wd_bf) = _in_proj(
            xf, attn_norm[l].astype(f32)[None, :], w_in[l].astype(bf16), col_gain,
            (w_out, w_gate, w_up, w_down), l)

        logit_bound = 1.01 * HEAD_DIM * q_scale * jnp.max(jnp.abs(qg)) * jnp.max(jnp.abs(kg))
        small_logits = (logit_bound <= _UNSHIFTED_SOFTMAX_MAX_LOG2).astype(jnp.int32).reshape(1)
        lamv = jnp.stack([lam_q1[l], lam_k1[l], lam_q2[l], lam_k2[l]]).astype(f32)
        mixed = _attention(proj, small_logits, lamv, sub_gain[l].astype(f32)[None, :],
                           batch=b, seq=s, lam_init=lam_init)

        tables = _ssm_tables(ssm_log_dt[l], ssm_a_re[l], ssm_a_im[l], ssm_b_re[l], ssm_b_im[l],
                             ssm_c_re[l], ssm_c_im[l], SSM_CHUNK)
        mixed = _ssm(proj, mixed, tables, ssm_d[l].astype(f32).reshape(1, SSM_WIDTH),
                     ssm_w_glu[l].astype(bf16), ssm_b_glu[l].astype(f32)[None, :],
                     ssm_out_gain[l].astype(f32)[None, :], batch=b, seq=s)

        x1 = _res_matmul(mixed, wo_bf, xf, tm=512, tn=1024, name="out_proj")

        act = _ffn_up(x1, ffn_norm[l].astype(f32)[None, :], wg_bf, wu_bf, conv_w[l].astype(f32),
                      conv_b[l].astype(f32)[None, :], seq=s)
        xf = _res_matmul(act, wd_bf, x1, tm=512, tn=512, name="ffn_down")
    return xf.reshape(b, s, d)
```

```python
import functools
import math

import jax
import jax.numpy as jnp
from jax import lax
from jax.experimental import pallas as pl
from jax.experimental.pallas import tpu as pltpu

D_MODEL = 4096
ATTN_HEADS = 12
HEAD_DIM = 128
ATTN_WIDTH = ATTN_HEADS * 2 * HEAD_DIM
SSM_WIDTH = D_MODEL - ATTN_WIDTH
SSM_GROUP = 16
SSM_GROUPS = SSM_WIDTH // SSM_GROUP
SSM_STATE = 64
IN_WIDTH = 3 * ATTN_WIDTH + SSM_WIDTH
D_FF = 11008
CONV_WIDTH = 3
EPS = 1e-6

V7X_LANES = 128
V7X_SUBLANES = 8
V7X_BF16_ROWS_PER_TILE = 2 * V7X_SUBLANES
V7X_VMEM_BYTES = 64 * 1024 * 1024
V7X_VMEM_RESERVED_BYTES = 6 * 1024 * 1024

F_TILE = 512
SSM_CLUSTER = V7X_LANES // SSM_GROUP
SSM_CLUSTERS = SSM_GROUPS // SSM_CLUSTER
SSM_PAIRS = 1
SSM_PAIR_STATE = SSM_CLUSTER * SSM_STATE // SSM_PAIRS
SSM_CL_COLS = SSM_PAIRS * 2 * SSM_PAIR_STATE
SSM_CHUNK = 256


def _vmem_limit(*nbytes):
    want = int(sum(nbytes)) + 12 * 1024 * 1024
    return min(want, V7X_VMEM_BYTES - V7X_VMEM_RESERVED_BYTES)


def _nbytes(shape, dtype):
    return math.prod(shape) * jnp.dtype(dtype).itemsize


def _rms_rows(x, gain):
    ms = jnp.mean(x * x, axis=-1, keepdims=True)
    return x * lax.rsqrt(ms + EPS) * gain


def _norm_into(x_ref, gain_ref, h_ref, *, rows, row_chunk, h_offset):
    def body(c, carry):
        r = pl.multiple_of(c * row_chunk, row_chunk)
        xb = x_ref[pl.ds(r, row_chunk), :].astype(jnp.float32)
        h_ref[pl.ds(h_offset + r, row_chunk), :] = _rms_rows(xb, gain_ref[...]).astype(h_ref.dtype)
        return carry
    lax.fori_loop(0, rows // row_chunk, body, 0)


def _in_proj_kernel(x_ref, g_ref, w_ref, cg_ref, *refs, tm, tn, n_norm_tiles, n_parts, n_side):
    side_in, o_ref, side_out, h_ref = (refs[:n_side], refs[n_side], refs[n_side + 1:2 * n_side + 1],
                                       refs[2 * n_side + 1])
    j = pl.program_id(1)


    @pl.when(j == 0)
    def _():
        _norm_into(x_ref, g_ref, h_ref, rows=tm, row_chunk=32, h_offset=0)

    is_qk = j < n_norm_tiles
    part = tn // n_parts
    for c0 in range(0, tn, part):
        acc = jnp.dot(h_ref[...], w_ref[:, c0:c0 + part], preferred_element_type=jnp.float32)
        for s0 in range(0, part, HEAD_DIM):
            raw = acc[:, s0:s0 + HEAD_DIM]
            sl = slice(c0 + s0, c0 + s0 + HEAD_DIM)
            o_ref[:, sl] = jnp.where(is_qk, _rms_rows(raw, cg_ref[:, sl]), raw).astype(o_ref.dtype)

    for src, dst in zip(side_in, side_out):
        dst[...] = src[...].astype(dst.dtype)


def _in_proj(x2d, gain, w_bf16, col_gain, side_f32, layer, *, tm=512, tn=1024):
    m, k = x2d.shape
    n = w_bf16.shape[1]
    n_norm_tiles = 2 * ATTN_WIDTH // tn
    n_col = n // tn
    n_steps = (m // tm) * n_col
    assert 2 * ATTN_WIDTH % tn == 0 and m % tm == 0 and n % tn == 0
    kern = functools.partial(_in_proj_kernel, tm=tm, tn=tn, n_norm_tiles=n_norm_tiles, n_parts=4,
                             n_side=len(side_f32))
    side_in_specs, side_out_specs, side_shapes, side_bytes = [], [], [], 0
    for a in side_f32:
        _, rows, cols = a.shape
        tile = V7X_BF16_ROWS_PER_TILE
        slab = next(s for s in range(tile, rows + 1, tile)
                    if rows % s == 0 and rows // s <= n_steps)
        last = rows // slab - 1
        side_in_specs.append(pl.BlockSpec(
            (None, slab, cols),
            lambda i, j, last=last: (layer, jnp.minimum(i * n_col + j, last), 0)))
        side_out_specs.append(pl.BlockSpec(
            (slab, cols), lambda i, j, last=last: (jnp.minimum(i * n_col + j, last), 0)))
        side_shapes.append(jax.ShapeDtypeStruct((rows, cols), jnp.bfloat16))
        side_bytes += 2 * _nbytes((slab, cols), jnp.float32) + 2 * _nbytes((slab, cols), jnp.bfloat16)
    vmem = _vmem_limit(2 * _nbytes((tm, k), jnp.float32), _nbytes((tm, k), jnp.bfloat16),
                       2 * _nbytes((k, tn), jnp.bfloat16), 2 * _nbytes((tm, tn), jnp.bfloat16),
                       _nbytes((tm, tn), jnp.float32), side_bytes)
    outs = pl.pallas_call(
        kern,
        out_shape=[jax.ShapeDtypeStruct((m, n), jnp.bfloat16)] + side_shapes,
        grid=(m // tm, n_col),
        in_specs=[
            pl.BlockSpec((tm, k), lambda i, j: (i, 0)),
            pl.BlockSpec((1, k), lambda i, j: (0, 0)),
            pl.BlockSpec((k, tn), lambda i, j: (0, j)),
            pl.BlockSpec((1, tn), lambda i, j: (0, j)),
        ] + side_in_specs,
        out_specs=[pl.BlockSpec((tm, tn), lambda i, j: (i, j))] + side_out_specs,
        scratch_shapes=[pltpu.VMEM((tm, k), jnp.bfloat16)],
        compiler_params=pltpu.CompilerParams(
            dimension_semantics=("arbitrary", "arbitrary"), vmem_limit_bytes=vmem),
        name="in_proj",
    )(x2d, gain, w_bf16, col_gain, *side_f32)
    return outs[0], outs[1:]


_UNSHIFTED_SOFTMAX_MAX_LOG2 = 60.0


def _qk_scores(q, k, row_offset):
    s = lax.dot_general(q, k, (((1,), (1,)), ((), ())), preferred_element_type=jnp.float32)
    if row_offset is not None:
        row = lax.broadcasted_iota(jnp.int32, s.shape, 0)
        col = lax.broadcasted_iota(jnp.int32, s.shape, 1)
        s = jnp.where(col <= row + row_offset, s, -jnp.inf)
    return s


def _lane_partial_sums(p):
    ps = p[:, 0:V7X_LANES]
    for c in range(1, p.shape[1] // V7X_LANES):
        ps = ps + p[:, c * V7X_LANES:(c + 1) * V7X_LANES]
    return ps


def _attn_kernel(small_ref, lamv_ref, q_ref, k_ref, v_ref, sg_ref, o_ref,
                 m_ref, l_ref, acc_ref, p_ref, *, blk, nq, lam_init):
    unshifted = small_ref[0] == 1
    lamv = lamv_ref[...]
    lam = (jnp.exp(jnp.sum(lamv[0:1] * lamv[1:2], axis=-1, keepdims=True))
           - jnp.exp(jnp.sum(lamv[2:3] * lamv[3:4], axis=-1, keepdims=True)) + lam_init)

    def head(mi):
        return slice(mi * HEAD_DIM, (mi + 1) * HEAD_DIM)

    def rows_of(mi):
        return slice(mi * blk, (mi + 1) * blk)

    def q_block(qi, carry):
        qs = pl.ds(pl.multiple_of(qi * blk, blk), blk)
        l_ref[...] = jnp.zeros(l_ref.shape, jnp.float32)
        acc_ref[...] = jnp.zeros(acc_ref.shape, jnp.float32)

        def unshifted_step(start, width, masked):
            ks = pl.ds(start, width)
            for mi in range(2):
                p = jnp.exp2(_qk_scores(q_ref[qs, head(mi)], k_ref[ks, head(mi)],
                                        0 if masked else None))
                l_ref[rows_of(mi), :] += _lane_partial_sums(p)
                p_ref[rows_of(mi), 0:width] = p.astype(p_ref.dtype)
            acc_ref[...] += jnp.dot(p_ref[:, 0:width], v_ref[ks, :],
                                    preferred_element_type=jnp.float32)

        def shifted_step(j, masked):
            ks = pl.ds(pl.multiple_of(j * blk, blk), blk)
            vb = v_ref[ks, :]
            for mi in range(2):
                s = _qk_scores(q_ref[qs, head(mi)], k_ref[ks, head(mi)], 0 if masked else None)
                rs = rows_of(mi)
                m_old = m_ref[mi]
                m_new = jnp.maximum(m_old, jnp.max(s, axis=-1, keepdims=True))
                alpha = jnp.exp2(m_old - m_new)
                p = jnp.exp2(s - m_new)
                l_ref[rs, 0:1] = alpha * l_ref[rs, 0:1] + jnp.sum(p, axis=-1, keepdims=True)
                acc_ref[rs, :] = alpha * acc_ref[rs, :] + jnp.dot(
                    p.astype(vb.dtype), vb, preferred_element_type=jnp.float32)
                m_ref[mi] = m_new

        @pl.when(unshifted)
        def _():
            def pair_body(j, c):
                unshifted_step(pl.multiple_of(j * 2 * blk, 2 * blk), 2 * blk, False)
                return c
            lax.fori_loop(0, qi // 2, pair_body, 0)

            @pl.when(qi % 2 == 1)
            def _():
                unshifted_step(pl.multiple_of((qi - 1) * blk, blk), blk, False)

            unshifted_step(pl.multiple_of(qi * blk, blk), blk, True)

        @pl.when(jnp.logical_not(unshifted))
        def _():
            m_ref[...] = jnp.full(m_ref.shape, -jnp.inf, jnp.float32)

            def body(j, c):
                shifted_step(j, False)
                return c
            lax.fori_loop(0, qi, body, 0)
            shifted_step(qi, True)

        r1 = 1.0 / jnp.sum(l_ref[rows_of(0), :], axis=-1, keepdims=True)
        r2 = lam / jnp.sum(l_ref[rows_of(1), :], axis=-1, keepdims=True)
        o = acc_ref[rows_of(0), :] * r1 - acc_ref[rows_of(1), :] * r2
        o_ref[qs, :] = (_rms_rows(o, sg_ref[...]) * (1.0 - lam_init)).astype(o_ref.dtype)
        return carry

    lax.fori_loop(0, nq, q_block, 0)


def _attention(proj, small_logits, lamv, sub_gain, *, batch, seq, lam_init, blk=512):
    nq = seq // blk
    hw = 2 * HEAD_DIM
    kern = functools.partial(_attn_kernel, blk=blk, nq=nq, lam_init=lam_init)
    vmem = _vmem_limit(8 * _nbytes((seq, hw), jnp.bfloat16),
                       _nbytes((2 * blk, hw), jnp.float32), _nbytes((2 * blk, 2 * blk), jnp.bfloat16),
                       3 * _nbytes((blk, V7X_LANES), jnp.float32),
                       4 * _nbytes((blk, 2 * blk), jnp.float32))
    return pl.pallas_call(
        kern,
        out_shape=jax.ShapeDtypeStruct((batch * seq, D_MODEL), jnp.bfloat16),
        grid=(batch, ATTN_HEADS),
        in_specs=[
            pl.BlockSpec(memory_space=pltpu.SMEM),
            pl.BlockSpec((4, HEAD_DIM), lambda b, h: (0, 0)),
            pl.BlockSpec((seq, hw), lambda b, h: (b, h)),
            pl.BlockSpec((seq, hw), lambda b, h: (b, ATTN_HEADS + h)),
            pl.BlockSpec((seq, hw), lambda b, h: (b, 2 * ATTN_HEADS + h)),
            pl.BlockSpec((1, hw), lambda b, h: (0, 0)),
        ],
        out_specs=pl.BlockSpec((seq, hw), lambda b, h: (b, h)),
        scratch_shapes=[
            pltpu.VMEM((2, blk, 1), jnp.float32),
            pltpu.VMEM((2 * blk, V7X_LANES), jnp.float32),
            pltpu.VMEM((2 * blk, hw), jnp.float32),
            pltpu.VMEM((2 * blk, 2 * blk), jnp.bfloat16),
        ],
        compiler_params=pltpu.CompilerParams(
            dimension_semantics=("arbitrary", "arbitrary"), vmem_limit_bytes=vmem),
        name="diff_attn",
    )(small_logits, lamv, proj, proj, proj, sub_gain)


def _cmul(ar, ai, br, bi):
    return ar * br - ai * bi, ar * bi + ai * br


def _ssm_kernel(u_ref, mixed_ref, bmat_ref, cmat_ref, pinv_ref, pfwd_ref, pcar_ref, d_ref,
                wglu_ref, bglu_ref, og_ref, o_ref, h_ref, y_ref, *, chunk):
    del mixed_ref
    c = pl.program_id(1)

    @pl.when(c == 0)
    def _():
        h_ref[...] = jnp.zeros(h_ref.shape, jnp.float32)

    n = SSM_PAIR_STATE
    row = lax.broadcasted_iota(jnp.int32, (chunk, chunk), 0)
    col = lax.broadcasted_iota(jnp.int32, (chunk, chunk), 1)
    tri = (col <= row).astype(jnp.bfloat16)

    for cl in range(SSM_CLUSTERS):
        ls = slice(cl * V7X_LANES, (cl + 1) * V7X_LANES)
        u_cl = u_ref[:, ls]
        y_cl = None
        for pr in range(SSM_PAIRS):
            re = slice(pr * 2 * n, pr * 2 * n + n)
            im = slice(pr * 2 * n + n, (pr + 1) * 2 * n)
            both = slice(pr * 2 * n, (pr + 1) * 2 * n)
            bu = jnp.dot(u_cl, bmat_ref[cl, :, both], preferred_element_type=jnp.float32)
            xr, xi = _cmul(pinv_ref[cl, :, re], pinv_ref[cl, :, im], bu[:, :n], bu[:, n:])
            x = jnp.concatenate([xr, xi], axis=1).astype(jnp.bfloat16)
            cs = jnp.dot(tri, x, preferred_element_type=jnp.float32)
            cr, ci = _cmul(pcar_ref[cl:cl + 1, re], pcar_ref[cl:cl + 1, im],
                           h_ref[cl:cl + 1, re], h_ref[cl:cl + 1, im])
            hr, hi = _cmul(pfwd_ref[cl, :, re], pfwd_ref[cl, :, im], cs[:, :n] + cr, cs[:, n:] + ci)
            h_ref[cl:cl + 1, re] = hr[chunk - 1:chunk, :]
            h_ref[cl:cl + 1, im] = hi[chunk - 1:chunk, :]
            h = jnp.concatenate([hr, hi], axis=1).astype(jnp.bfloat16)
            part = jnp.dot(h, cmat_ref[cl, both, :], preferred_element_type=jnp.float32)
            y_cl = part if y_cl is None else y_cl + part
        y_ref[:, ls] = y_cl

    y = y_ref[...] + d_ref[...] * u_ref[...].astype(jnp.float32)
    y = jax.nn.gelu(y, approximate=True)
    z = jnp.dot(y.astype(jnp.bfloat16), wglu_ref[...], preferred_element_type=jnp.float32)
    z = z + bglu_ref[...]
    y = y * (1.0 / (1.0 + jnp.exp(-z)))
    o_ref[...] = _rms_rows(y, og_ref[...]).astype(o_ref.dtype)


def _ssm_tables(log_dt, a_re, a_im, b_re, b_im, c_re, c_im, chunk):
    f32 = jnp.float32
    n, c = SSM_STATE, SSM_GROUP
    dt = jnp.exp(log_dt.astype(f32))[:, None]
    ar, ai = a_re.astype(f32), a_im.astype(f32)
    mag = jnp.exp(ar * dt)
    lb_re = mag * jnp.cos(ai * dt)
    lb_im = mag * jnp.sin(ai * dt)
    pr, pi_ = lb_re - 1.0, lb_im
    den = ar * ar + ai * ai
    z_re = ((pr * ar + pi_ * ai) / den)[..., None]
    z_im = ((pi_ * ar - pr * ai) / den)[..., None]
    br, bi = b_re.astype(f32), b_im.astype(f32)
    bb_re = z_re * br - z_im * bi
    bb_im = z_re * bi + z_im * br

    def powers(k):
        kk = k.astype(f32)[:, None, None]
        pm = jnp.exp(kk * (ar * dt)[None])
        ph = kk * (ai * dt)[None]
        return pm * jnp.cos(ph), pm * jnp.sin(ph)

    def pair_cols(re, im):
        lead = re.shape[:-1]
        re = re.reshape(*lead, SSM_PAIRS, 1, SSM_PAIR_STATE)
        im = im.reshape(*lead, SSM_PAIRS, 1, SSM_PAIR_STATE)
        return jnp.concatenate([re, im], axis=-2).reshape(*lead, SSM_CL_COLS)

    def table(k):
        re, im = powers(k)
        kk = re.shape[0]
        re = re.reshape(kk, SSM_CLUSTERS, SSM_CLUSTER * n)
        im = im.reshape(kk, SSM_CLUSTERS, SSM_CLUSTER * n)
        return pair_cols(re, im).transpose(1, 0, 2)

    c0 = chunk // 2
    t = jnp.arange(chunk)
    pinv = table(c0 - t)
    pfwd = table(t - c0)
    pcar = table(jnp.array([c0 + 1]))[:, 0, :]

    eye = jnp.eye(SSM_CLUSTER, dtype=f32)
    def bpart(bb):
        b4 = bb.reshape(SSM_CLUSTERS, SSM_CLUSTER, n, c)
        return jnp.einsum('kgnc,gh->kgchn', b4, eye).reshape(
            SSM_CLUSTERS, V7X_LANES, SSM_CLUSTER * n)
    bmat = pair_cols(bpart(bb_re), bpart(bb_im)).astype(jnp.bfloat16)
    def cpart(cc):
        c4 = cc.astype(f32).reshape(SSM_CLUSTERS, SSM_CLUSTER, c, n)
        return jnp.einsum('kgcn,gh->khcgn', c4, eye).reshape(
            SSM_CLUSTERS, V7X_LANES, SSM_CLUSTER * n)
    cmat = pair_cols(cpart(c_re), -cpart(c_im)).transpose(0, 2, 1).astype(jnp.bfloat16)
    return bmat, cmat, pinv, pfwd, pcar


def _ssm(proj, mixed, tables, d_skip, w_glu, b_glu, out_gain, *, batch, seq, chunk=SSM_CHUNK):
    bmat, cmat, pinv, pfwd, pcar = tables
    nc = seq // chunk
    u_col = 3 * ATTN_WIDTH // SSM_WIDTH
    o_col = ATTN_WIDTH // SSM_WIDTH
    kern = functools.partial(_ssm_kernel, chunk=chunk)
    const3 = lambda b, c: (0, 0, 0)
    const2 = lambda b, c: (0, 0)
    vmem = _vmem_limit(2 * (bmat.nbytes + cmat.nbytes + pinv.nbytes + pfwd.nbytes + pcar.nbytes),
                       4 * _nbytes((SSM_WIDTH, SSM_WIDTH), jnp.bfloat16),
                       8 * _nbytes((chunk, SSM_WIDTH), jnp.float32))
    return pl.pallas_call(
        kern,
        out_shape=jax.ShapeDtypeStruct(mixed.shape, mixed.dtype),
        grid=(batch, nc),
        in_specs=[
            pl.BlockSpec((chunk, SSM_WIDTH), lambda b, c: (b * nc + c, u_col)),
            pl.BlockSpec(memory_space=pl.ANY),
            pl.BlockSpec(bmat.shape, const3),
            pl.BlockSpec(cmat.shape, const3),
            pl.BlockSpec(pinv.shape, const3),
            pl.BlockSpec(pfwd.shape, const3),
            pl.BlockSpec(pcar.shape, const2),
            pl.BlockSpec((1, SSM_WIDTH), const2),
            pl.BlockSpec((SSM_WIDTH, SSM_WIDTH), const2),
            pl.BlockSpec((1, SSM_WIDTH), const2),
            pl.BlockSpec((1, SSM_WIDTH), const2),
        ],
        out_specs=pl.BlockSpec((chunk, SSM_WIDTH), lambda b, c: (b * nc + c, o_col)),
        scratch_shapes=[
            pltpu.VMEM((SSM_CLUSTERS, SSM_CL_COLS), jnp.float32),
            pltpu.VMEM((chunk, SSM_WIDTH), jnp.float32),
        ],
        input_output_aliases={1: 0},
        compiler_params=pltpu.CompilerParams(
            dimension_semantics=("arbitrary", "arbitrary"), vmem_limit_bytes=vmem),
        name="s5_ssm",
    )(proj, mixed, bmat, cmat, pinv, pfwd, pcar, d_skip, w_glu, b_glu, out_gain)


def _res_matmul_kernel(a_ref, w_ref, r_ref, o_ref, *maybe_bf16_ref):
    out = r_ref[...] + jnp.dot(a_ref[...], w_ref[...], preferred_element_type=jnp.float32)
    o_ref[...] = out
    for ob_ref in maybe_bf16_ref:
        ob_ref[...] = out.astype(ob_ref.dtype)


def _res_matmul(a, w, res, *, tm, tn, name, with_bf16_copy=False):
    m, k = a.shape
    n = w.shape[1]
    assert m % tm == 0 and n % tn == 0
    vmem = _vmem_limit(2 * _nbytes((tm, k), a.dtype), 2 * _nbytes((k, tn), w.dtype),
                       6 * _nbytes((tm, tn), jnp.float32))
    tile = pl.BlockSpec((tm, tn), lambda i, j: (i, j))
    out_shape = [jax.ShapeDtypeStruct((m, n), jnp.float32)]
    if with_bf16_copy:
        out_shape.append(jax.ShapeDtypeStruct((m, n), jnp.bfloat16))
    outs = pl.pallas_call(
        _res_matmul_kernel,
        out_shape=out_shape,
        grid=(m // tm, n // tn),
        in_specs=[
            pl.BlockSpec((tm, k), lambda i, j: (i, 0)),
            pl.BlockSpec((k, tn), lambda i, j: (0, j)),
            tile,
        ],
        out_specs=[tile] * len(out_shape),
        compiler_params=pltpu.CompilerParams(
            dimension_semantics=("parallel", "parallel"), vmem_limit_bytes=vmem),
        name=name,
    )(a, w, res)
    return outs if with_bf16_copy else outs[0]


def _ffn_up_kernel(x_ref, halo_ref, g_ref, wg_ref, wu_ref, cw_ref, cb_ref, o_ref, h_ref,
                   *, tm, blocks_per_seq):
    i = pl.program_id(0)
    j = pl.program_id(1)
    halo = V7X_BF16_ROWS_PER_TILE

    @pl.when(j == 0)
    def _():
        hh = _rms_rows(halo_ref[...].astype(jnp.float32), g_ref[...])
        hh = jnp.where(i % blocks_per_seq == 0, 0.0, hh)
        h_ref[0:halo, :] = hh.astype(h_ref.dtype)
        _norm_into(x_ref, g_ref, h_ref, rows=tm, row_chunk=32, h_offset=halo)

    g_all = jnp.dot(h_ref[...], wg_ref[...], preferred_element_type=jnp.float32)
    up = jnp.dot(h_ref[halo:, :], wu_ref[...], preferred_element_type=jnp.float32)
    rows = 128
    for r0 in range(0, tm, rows):
        for c0 in range(0, o_ref.shape[1], V7X_LANES):
            cs = slice(c0, c0 + V7X_LANES)
            g = g_all[r0:r0 + halo + rows, cs]
            t = cw_ref[1:2, cs] * g + pltpu.roll(cw_ref[0:1, cs] * g, 1, 0)
            gc = (cb_ref[:, cs] + cw_ref[2:3, cs] * g + pltpu.roll(t, 1, 0))[halo:, :]
            act = gc * (1.0 / (1.0 + jnp.exp(-gc))) * up[r0:r0 + rows, cs]
            o_ref[r0:r0 + rows, cs] = act.astype(o_ref.dtype)


def _ffn_up(x1, gain, wg, wu, conv_w, conv_b, *, seq, tm=1024, tf=F_TILE):
    m, k = x1.shape
    f = wg.shape[1]
    halo = V7X_BF16_ROWS_PER_TILE
    assert m % tm == 0 and seq % tm == 0
    kern = functools.partial(_ffn_up_kernel, tm=tm, blocks_per_seq=seq // tm)
    vmem = _vmem_limit(2 * _nbytes((tm, k), x1.dtype), _nbytes((tm + halo, k), jnp.bfloat16),
                       4 * _nbytes((k, tf), jnp.bfloat16), 2 * _nbytes((tm, tf), jnp.bfloat16),
                       3 * _nbytes((tm + halo, tf), jnp.float32))
    return pl.pallas_call(
        kern,
        out_shape=jax.ShapeDtypeStruct((m, f), jnp.bfloat16),
        grid=(m // tm, pl.cdiv(f, tf)),
        in_specs=[
            pl.BlockSpec((tm, k), lambda i, j: (i, 0)),
            pl.BlockSpec((halo, k), lambda i, j: (jnp.maximum(i * (tm // halo) - 1, 0), 0)),
            pl.BlockSpec((1, k), lambda i, j: (0, 0)),
            pl.BlockSpec((k, tf), lambda i, j: (0, j)),
            pl.BlockSpec((k, tf), lambda i, j: (0, j)),
            pl.BlockSpec((CONV_WIDTH, tf), lambda i, j: (0, j)),
            pl.BlockSpec((1, tf), lambda i, j: (0, j)),
        ],
        out_specs=pl.BlockSpec((tm, tf), lambda i, j: (i, j)),
        scratch_shapes=[pltpu.VMEM((tm + halo, k), jnp.bfloat16)],
        compiler_params=pltpu.CompilerParams(
            dimension_semantics=("arbitrary", "arbitrary"), vmem_limit_bytes=vmem),
        name="ffn_up",
    )(x1, x1, gain, wg, wu, conv_w, conv_b)


def kernel(x, attn_norm, w_in, q_gain, k_gain, lam_q1, lam_k1, lam_q2, lam_k2, sub_gain, ssm_log_dt, ssm_a_re, ssm_a_im, ssm_b_re, ssm_b_im, ssm_c_re, ssm_c_im, ssm_d, ssm_w_glu, ssm_b_glu, ssm_out_gain, w_out, ffn_norm, w_gate, w_up, conv_w, conv_b, w_down):
    b, s, d = x.shape
    depth = w_in.shape[0]
    bf16, f32 = jnp.bfloat16, jnp.float32
    xf = x.reshape(b * s, d)
    for l in range(depth):
        lam_init = 0.8 - 0.6 * math.exp(-0.3 * l)
        q_scale = HEAD_DIM ** -0.5 * math.log2(math.e)
        qg, kg = q_gain[l].astype(f32), k_gain[l].astype(f32)
        col_gain = jnp.concatenate([
            jnp.tile(qg * q_scale, ATTN_WIDTH // HEAD_DIM),
            jnp.tile(kg, ATTN_WIDTH // HEAD_DIM),
            jnp.ones((IN_WIDTH - 2 * ATTN_WIDTH,), f32)])[None, :]
        proj, (wo_bf, wg_bf, wu_bf, wd_bf) = _in_proj(
            xf, attn_norm[l].astype(f32)[None, :], w_in[l].astype(bf16), col_gain,
            (w_out, w_gate, w_up, w_down), l)

        logit_bound = 1.01 * HEAD_DIM * q_scale * jnp.max(jnp.abs(qg)) * jnp.max(jnp.abs(kg))
        small_logits = (logit_bound <= _UNSHIFTED_SOFTMAX_MAX_LOG2).astype(jnp.int32).reshape(1)
        lamv = jnp.stack([lam_q1[l], lam_k1[l], lam_q2[l], lam_k2[l]]).astype(f32)
        mixed = _attention(proj, small_logits, lamv, sub_gain[l].astype(f32)[None, :],
                           batch=b, seq=s, lam_init=lam_init)

        tables = _ssm_tables(ssm_log_dt[l], ssm_a_re[l], ssm_a_im[l], ssm_b_re[l], ssm_b_im[l],
                             ssm_c_re[l], ssm_c_im[l], SSM_CHUNK)
        mixed = _ssm(proj, mixed, tables, ssm_d[l].astype(f32).reshape(1, SSM_WIDTH),
                     ssm_w_glu[l].astype(bf16), ssm_b_glu[l].astype(f32)[None, :],
                     ssm_out_gain[l].astype(f32)[None, :], batch=b, seq=s)

        x1, x1_bf = _res_matmul(mixed, wo_bf, xf, tm=512, tn=1024, name="out_proj",
                                with_bf16_copy=True)

        act = _ffn_up(x1_bf, ffn_norm[l].astype(f32)[None, :], wg_bf, wu_bf, conv_w[l].astype(f32),
                      conv_b[l].astype(f32)[None, :], seq=s)
        xf = _res_matmul(act, wd_bf, x1, tm=512, tn=512, name="ffn_down")
    return xf.reshape(b, s, d)
```

```python
import functools
import math

import jax
import jax.numpy as jnp
from jax import lax
from jax.experimental import pallas as pl
from jax.experimental.pallas import tpu as pltpu

D_MODEL = 4096
ATTN_HEADS = 12
HEAD_DIM = 128
ATTN_WIDTH = ATTN_HEADS * 2 * HEAD_DIM
SSM_WIDTH = D_MODEL - ATTN_WIDTH
SSM_GROUP = 16
SSM_GROUPS = SSM_WIDTH // SSM_GROUP
SSM_STATE = 64
IN_WIDTH = 3 * ATTN_WIDTH + SSM_WIDTH
D_FF = 11008
CONV_WIDTH = 3
EPS = 1e-6

V7X_LANES = 128
V7X_SUBLANES = 8
V7X_BF16_ROWS_PER_TILE = 2 * V7X_SUBLANES
V7X_VMEM_BYTES = 64 * 1024 * 1024
V7X_VMEM_RESERVED_BYTES = 6 * 1024 * 1024

F_TILE = 512
SSM_CLUSTER = V7X_LANES // SSM_GROUP
SSM_CLUSTERS = SSM_GROUPS // SSM_CLUSTER
SSM_CL_STATE = SSM_CLUSTER * SSM_STATE
SSM_CL_COLS = 2 * SSM_CL_STATE
SSM_CHUNK = 256


def _vmem_limit(*nbytes):
    want = int(sum(nbytes)) + 12 * 1024 * 1024
    return min(want, V7X_VMEM_BYTES - V7X_VMEM_RESERVED_BYTES)


def _nbytes(shape, dtype):
    return math.prod(shape) * jnp.dtype(dtype).itemsize


def _rms_rows(x, gain):
    ms = jnp.mean(x * x, axis=-1, keepdims=True)
    return x * lax.rsqrt(ms + EPS) * gain


def _norm_into(x_ref, gain_ref, h_ref, *, rows, row_chunk, h_offset):
    def body(c, carry):
        r = pl.multiple_of(c * row_chunk, row_chunk)
        xb = x_ref[pl.ds(r, row_chunk), :].astype(jnp.float32)
        h_ref[pl.ds(h_offset + r, row_chunk), :] = _rms_rows(xb, gain_ref[...]).astype(h_ref.dtype)
        return carry
    lax.fori_loop(0, rows // row_chunk, body, 0, unroll=4)


def _in_proj_kernel(x_ref, g_ref, w_ref, cg_ref, *refs, tm, tn, n_norm_tiles, n_parts, n_side):
    side_in, o_ref, side_out, h_ref = (refs[:n_side], refs[n_side], refs[n_side + 1:2 * n_side + 1],
                                       refs[2 * n_side + 1])
    j = pl.program_id(1)


    @pl.when(j == 0)
    def _():
        _norm_into(x_ref, g_ref, h_ref, rows=tm, row_chunk=16, h_offset=0)

    is_qk = j < n_norm_tiles
    part = tn // n_parts
    for c0 in range(0, tn, part):
        acc = jnp.dot(h_ref[...], w_ref[:, c0:c0 + part], preferred_element_type=jnp.float32)
        for s0 in range(0, part, HEAD_DIM):
            raw = acc[:, s0:s0 + HEAD_DIM]
            sl = slice(c0 + s0, c0 + s0 + HEAD_DIM)
            o_ref[:, sl] = jnp.where(is_qk, _rms_rows(raw, cg_ref[:, sl]), raw).astype(o_ref.dtype)

    for src, dst in zip(side_in, side_out):
        dst[...] = src[...].astype(dst.dtype)


def _in_proj(x2d, gain, w_bf16, col_gain, side_f32, layer, *, tm=512, tn=1024):
    m, k = x2d.shape
    n = w_bf16.shape[1]
    n_norm_tiles = 2 * ATTN_WIDTH // tn
    n_col = n // tn
    n_steps = (m // tm) * n_col
    assert 2 * ATTN_WIDTH % tn == 0 and m % tm == 0 and n % tn == 0
    kern = functools.partial(_in_proj_kernel, tm=tm, tn=tn, n_norm_tiles=n_norm_tiles, n_parts=4,
                             n_side=len(side_f32))
    side_in_specs, side_out_specs, side_shapes, side_bytes = [], [], [], 0
    for a in side_f32:
        _, rows, cols = a.shape
        tile = V7X_BF16_ROWS_PER_TILE
        slab = next(s for s in range(tile, rows + 1, tile)
                    if rows % s == 0 and rows // s <= n_steps)
        last = rows // slab - 1
        side_in_specs.append(pl.BlockSpec(
            (None, slab, cols),
            lambda i, j, last=last: (layer, jnp.minimum(i * n_col + j, last), 0)))
        side_out_specs.append(pl.BlockSpec(
            (slab, cols), lambda i, j, last=last: (jnp.minimum(i * n_col + j, last), 0)))
        side_shapes.append(jax.ShapeDtypeStruct((rows, cols), jnp.bfloat16))
        side_bytes += 2 * _nbytes((slab, cols), jnp.float32) + 2 * _nbytes((slab, cols), jnp.bfloat16)
    vmem = _vmem_limit(2 * _nbytes((tm, k), jnp.float32), _nbytes((tm, k), jnp.bfloat16),
                       2 * _nbytes((k, tn), jnp.bfloat16), 2 * _nbytes((tm, tn), jnp.bfloat16),
                       _nbytes((tm, tn), jnp.float32), side_bytes)
    outs = pl.pallas_call(
        kern,
        out_shape=[jax.ShapeDtypeStruct((m, n), jnp.bfloat16)] + side_shapes,
        grid=(m // tm, n_col),
        in_specs=[
            pl.BlockSpec((tm, k), lambda i, j: (i, 0)),
            pl.BlockSpec((1, k), lambda i, j: (0, 0)),
            pl.BlockSpec((k, tn), lambda i, j: (0, j)),
            pl.BlockSpec((1, tn), lambda i, j: (0, j)),
        ] + side_in_specs,
        out_specs=[pl.BlockSpec((tm, tn), lambda i, j: (i, j))] + side_out_specs,
        scratch_shapes=[pltpu.VMEM((tm, k), jnp.bfloat16)],
        compiler_params=pltpu.CompilerParams(
            dimension_semantics=("arbitrary", "arbitrary"), vmem_limit_bytes=vmem),
        name="in_proj",
    )(x2d, gain, w_bf16, col_gain, *side_f32)
    return outs[0], outs[1:]


_UNSHIFTED_SOFTMAX_MAX_LOG2 = 60.0


def _qk_scores(q, k, row_offset):
    s = lax.dot_general(q, k, (((1,), (1,)), ((), ())), preferred_element_type=jnp.float32)
    if row_offset is not None:
        row = lax.broadcasted_iota(jnp.int32, s.shape, 0)
        col = lax.broadcasted_iota(jnp.int32, s.shape, 1)
        s = jnp.where(col <= row + row_offset, s, -jnp.inf)
    return s


def _lane_partial_sums(p):
    ps = p[:, 0:V7X_LANES]
    for c in range(1, p.shape[1] // V7X_LANES):
        ps = ps + p[:, c * V7X_LANES:(c + 1) * V7X_LANES]
    return ps


def _attn_kernel(small_ref, lamv_ref, q_ref, k_ref, v_ref, sg_ref, o_ref,
                 m_ref, l_ref, acc_ref, p_ref, *, blk, nq, lam_init):
    unshifted = small_ref[0] == 1
    lamv = lamv_ref[...]
    lam = (jnp.exp(jnp.sum(lamv[0:1] * lamv[1:2], axis=-1, keepdims=True))
           - jnp.exp(jnp.sum(lamv[2:3] * lamv[3:4], axis=-1, keepdims=True)) + lam_init)

    def head(mi):
        return slice(mi * HEAD_DIM, (mi + 1) * HEAD_DIM)

    half = blk // 2

    def stacked(hf, mi):
        return slice((2 * hf + mi) * half, (2 * hf + mi + 1) * half)

    def q_block(qi, carry):
        q0 = pl.multiple_of(qi * blk, blk)
        l_ref[...] = jnp.zeros(l_ref.shape, jnp.float32)
        acc_ref[...] = jnp.zeros(acc_ref.shape, jnp.float32)

        def tile(hfs, start, width, mask_offset, shifted):
            ks = pl.ds(start, width)
            n_rows = len(hfs) * half
            qs = pl.ds(q0 + hfs[0] * half, n_rows)
            for mi in range(2):
                s = _qk_scores(q_ref[qs, head(mi)], k_ref[ks, head(mi)], mask_offset)
                if not shifted:
                    p = jnp.exp2(s)
                    ps = _lane_partial_sums(p)
                for n, hf in enumerate(hfs):
                    st = stacked(hf, mi)
                    local = slice(n * half, (n + 1) * half)
                    if shifted:
                        m_old = m_ref[st, :]
                        m_new = jnp.maximum(m_old, jnp.max(s[local], axis=-1, keepdims=True))
                        alpha = jnp.exp2(m_old - m_new)
                        ph = jnp.exp2(s[local] - m_new)
                        l_ref[st, 0:1] = alpha * l_ref[st, 0:1] + jnp.sum(ph, axis=-1, keepdims=True)
                        acc_ref[st, :] = alpha * acc_ref[st, :]
                        m_ref[st, :] = m_new
                        p_ref[st, 0:width] = ph.astype(p_ref.dtype)
                    else:
                        l_ref[st, :] += ps[local]
                        p_ref[st, 0:width] = p[local].astype(p_ref.dtype)
            both_maps = slice(2 * hfs[0] * half, 2 * (hfs[-1] + 1) * half)
            acc_ref[both_maps, :] += jnp.dot(p_ref[both_maps, 0:width], v_ref[ks, :],
                                             preferred_element_type=jnp.float32)

        def causal_tiles(shifted):
            def body(j, c):
                tile((0, 1), pl.multiple_of(j * blk, blk), blk, None, shifted)
                return c
            lax.fori_loop(0, qi, body, 0)
            tile((0,), q0, half, 0, shifted)
            tile((1,), q0, blk, half, shifted)

        @pl.when(unshifted)
        def _():
            causal_tiles(False)

        @pl.when(jnp.logical_not(unshifted))
        def _():
            m_ref[...] = jnp.full(m_ref.shape, -jnp.inf, jnp.float32)
            causal_tiles(True)

        for hf in range(2):
            r1 = 1.0 / jnp.sum(l_ref[stacked(hf, 0), :], axis=-1, keepdims=True)
            r2 = lam / jnp.sum(l_ref[stacked(hf, 1), :], axis=-1, keepdims=True)
            o = acc_ref[stacked(hf, 0), :] * r1 - acc_ref[stacked(hf, 1), :] * r2
            o_ref[pl.ds(q0 + hf * half, half), :] = (
                _rms_rows(o, sg_ref[...]) * (1.0 - lam_init)).astype(o_ref.dtype)
        return carry

    lax.fori_loop(0, nq, q_block, 0)


def _attention(proj, small_logits, lamv, sub_gain, *, batch, seq, lam_init, blk=1024):
    nq = seq // blk
    hw = 2 * HEAD_DIM
    kern = functools.partial(_attn_kernel, blk=blk, nq=nq, lam_init=lam_init)
    vmem = _vmem_limit(8 * _nbytes((seq, hw), jnp.bfloat16),
                       _nbytes((2 * blk, hw), jnp.float32), _nbytes((2 * blk, blk), jnp.bfloat16),
                       4 * _nbytes((blk, V7X_LANES), jnp.float32),
                       4 * _nbytes((blk, blk), jnp.float32))
    return pl.pallas_call(
        kern,
        out_shape=jax.ShapeDtypeStruct((batch * seq, D_MODEL), jnp.bfloat16),
        grid=(batch, ATTN_HEADS),
        in_specs=[
            pl.BlockSpec(memory_space=pltpu.SMEM),
            pl.BlockSpec((4, HEAD_DIM), lambda b, h: (0, 0)),
            pl.BlockSpec((seq, hw), lambda b, h: (b, h)),
            pl.BlockSpec((seq, hw), lambda b, h: (b, ATTN_HEADS + h)),
            pl.BlockSpec((seq, hw), lambda b, h: (b, 2 * ATTN_HEADS + h)),
            pl.BlockSpec((1, hw), lambda b, h: (0, 0)),
        ],
        out_specs=pl.BlockSpec((seq, hw), lambda b, h: (b, h)),
        scratch_shapes=[
            pltpu.VMEM((2 * blk, 1), jnp.float32),
            pltpu.VMEM((2 * blk, V7X_LANES), jnp.float32),
            pltpu.VMEM((2 * blk, hw), jnp.float32),
            pltpu.VMEM((2 * blk, blk), jnp.bfloat16),
        ],
        compiler_params=pltpu.CompilerParams(
            dimension_semantics=("arbitrary", "arbitrary"), vmem_limit_bytes=vmem),
        name="diff_attn",
    )(small_logits, lamv, proj, proj, proj, sub_gain)


_SSM_FACTORISED_MAX_LOG = 60.0


def _cmul(ar, ai, br, bi):
    return ar * br - ai * bi, ar * bi + ai * br


def _ssm_kernel(flag_ref, u_ref, mixed_ref, bmat_ref, cmat_ref, pinv_ref, pfwd_ref, pstep_ref, d_ref,
                wglu_ref, bglu_ref, og_ref, o_ref, h_ref, y_ref, bu_ref, hs_ref, *, chunk):
    del mixed_ref
    c = pl.program_id(1)

    @pl.when(c == 0)
    def _():
        h_ref[...] = jnp.zeros(h_ref.shape, jnp.float32)

    n = SSM_CL_STATE
    factorised = flag_ref[0] == 1

    def lanes_of(cl):
        return slice(cl * V7X_LANES, (cl + 1) * V7X_LANES)

    @pl.when(factorised)
    def _():
        row = lax.broadcasted_iota(jnp.int32, (chunk, chunk), 0)
        col = lax.broadcasted_iota(jnp.int32, (chunk, chunk), 1)
        tri = (col <= row).astype(jnp.bfloat16)
        for cl in range(SSM_CLUSTERS):
            bu = jnp.dot(u_ref[:, lanes_of(cl)], bmat_ref[cl], preferred_element_type=jnp.float32)
            xr, xi = _cmul(pinv_ref[cl, :, :n], pinv_ref[cl, :, n:], bu[:, :n], bu[:, n:])
            x = jnp.concatenate([xr, xi], axis=1).astype(jnp.bfloat16)
            cs = jnp.dot(tri, x, preferred_element_type=jnp.float32)
            cr, ci = _cmul(pstep_ref[1, cl:cl + 1, :n], pstep_ref[1, cl:cl + 1, n:],
                           h_ref[cl:cl + 1, :n], h_ref[cl:cl + 1, n:])
            hr, hi = _cmul(pfwd_ref[cl, :, :n], pfwd_ref[cl, :, n:], cs[:, :n] + cr, cs[:, n:] + ci)
            h_ref[cl:cl + 1, :n] = hr[chunk - 1:chunk, :]
            h_ref[cl:cl + 1, n:] = hi[chunk - 1:chunk, :]
            h = jnp.concatenate([hr, hi], axis=1).astype(jnp.bfloat16)
            y_ref[:, lanes_of(cl)] = jnp.dot(h, cmat_ref[cl], preferred_element_type=jnp.float32)

    @pl.when(jnp.logical_not(factorised))
    def _():
        for cl in range(SSM_CLUSTERS):
            bu_ref[...] = jnp.dot(u_ref[:, lanes_of(cl)], bmat_ref[cl],
                                  preferred_element_type=jnp.float32)
            lr, li = pstep_ref[0, cl:cl + 1, :n], pstep_ref[0, cl:cl + 1, n:]

            def step(t, carry):
                hr, hi = carry
                ar, ai = _cmul(lr, li, hr, hi)
                hr = ar + bu_ref[pl.ds(t, 1), :n]
                hi = ai + bu_ref[pl.ds(t, 1), n:]
                hs_ref[pl.ds(t, 1), :n] = hr
                hs_ref[pl.ds(t, 1), n:] = hi
                return hr, hi

            hr, hi = lax.fori_loop(0, chunk, step, (h_ref[cl:cl + 1, :n], h_ref[cl:cl + 1, n:]))
            h_ref[cl:cl + 1, :n] = hr
            h_ref[cl:cl + 1, n:] = hi
            y_ref[:, lanes_of(cl)] = jnp.dot(hs_ref[...].astype(jnp.bfloat16), cmat_ref[cl],
                                             preferred_element_type=jnp.float32)

    y = y_ref[...] + d_ref[...] * u_ref[...].astype(jnp.float32)
    y = jax.nn.gelu(y, approximate=True)
    z = jnp.dot(y.astype(jnp.bfloat16), wglu_ref[...], preferred_element_type=jnp.float32)
    z = z + bglu_ref[...]
    y = y * (1.0 / (1.0 + jnp.exp(-z)))
    o_ref[...] = _rms_rows(y, og_ref[...]).astype(o_ref.dtype)


def _ssm_tables(log_dt, a_re, a_im, b_re, b_im, c_re, c_im, chunk):
    f32 = jnp.float32
    n, c = SSM_STATE, SSM_GROUP
    dt = jnp.exp(log_dt.astype(f32))[:, None]
    ar, ai = a_re.astype(f32), a_im.astype(f32)
    mag = jnp.exp(ar * dt)
    lb_re = mag * jnp.cos(ai * dt)
    lb_im = mag * jnp.sin(ai * dt)
    pr, pi_ = lb_re - 1.0, lb_im
    den = ar * ar + ai * ai
    z_re = ((pr * ar + pi_ * ai) / den)[..., None]
    z_im = ((pi_ * ar - pr * ai) / den)[..., None]
    br, bi = b_re.astype(f32), b_im.astype(f32)
    bb_re = z_re * br - z_im * bi
    bb_im = z_re * bi + z_im * br

    def pair_cols(re, im):
        return jnp.concatenate([re, im], axis=-1)

    log_re = (ar * dt).reshape(SSM_CLUSTERS, 1, SSM_CLUSTER * n)
    log_im = (ai * dt).reshape(SSM_CLUSTERS, 1, SSM_CLUSTER * n)

    def table(k):
        kk = k.astype(f32)[None, :, None]
        pm = jnp.exp(kk * log_re)
        ph = kk * log_im
        return pair_cols(pm * jnp.cos(ph), pm * jnp.sin(ph))

    c0 = chunk // 2
    pw = table(jnp.arange(-c0, c0 + 1))
    pfwd = pw[:, 0:chunk]
    pinv = jnp.flip(pw[:, 1:chunk + 1], axis=1)
    pstep = table(jnp.array([1, c0 + 1])).transpose(1, 0, 2)
    factorised = ((c0 + 1) * jnp.max(jnp.abs(log_re)) <= _SSM_FACTORISED_MAX_LOG)

    eye = jnp.eye(SSM_CLUSTER, dtype=f32)
    def bpart(bb):
        b4 = bb.reshape(SSM_CLUSTERS, SSM_CLUSTER, n, c)
        return jnp.einsum('kgnc,gh->kgchn', b4, eye).reshape(
            SSM_CLUSTERS, V7X_LANES, SSM_CLUSTER * n)
    bmat = pair_cols(bpart(bb_re), bpart(bb_im)).astype(jnp.bfloat16)
    def cpart(cc):
        c4 = cc.astype(f32).reshape(SSM_CLUSTERS, SSM_CLUSTER, c, n)
        return jnp.einsum('kgcn,gh->khcgn', c4, eye).reshape(
            SSM_CLUSTERS, V7X_LANES, SSM_CLUSTER * n)
    cmat = pair_cols(cpart(c_re), -cpart(c_im)).transpose(0, 2, 1).astype(jnp.bfloat16)
    return factorised.astype(jnp.int32).reshape(1), bmat, cmat, pinv, pfwd, pstep


def _ssm(proj, mixed, tables, d_skip, w_glu, b_glu, out_gain, *, batch, seq, chunk=SSM_CHUNK):
    factorised, bmat, cmat, pinv, pfwd, pstep = tables
    nc = seq // chunk
    u_col = 3 * ATTN_WIDTH // SSM_WIDTH
    o_col = ATTN_WIDTH // SSM_WIDTH
    kern = functools.partial(_ssm_kernel, chunk=chunk)
    const3 = lambda b, c: (0, 0, 0)
    const2 = lambda b, c: (0, 0)
    vmem = _vmem_limit(2 * (bmat.nbytes + cmat.nbytes + pinv.nbytes + pfwd.nbytes + pstep.nbytes),
                       4 * _nbytes((SSM_WIDTH, SSM_WIDTH), jnp.bfloat16),
                       10 * _nbytes((chunk, SSM_WIDTH), jnp.float32))
    return pl.pallas_call(
        kern,
        out_shape=jax.ShapeDtypeStruct(mixed.shape, mixed.dtype),
        grid=(batch, nc),
        in_specs=[
            pl.BlockSpec(memory_space=pltpu.SMEM),
            pl.BlockSpec((chunk, SSM_WIDTH), lambda b, c: (b * nc + c, u_col)),
            pl.BlockSpec(memory_space=pl.ANY),
            pl.BlockSpec(bmat.shape, const3),
            pl.BlockSpec(cmat.shape, const3),
            pl.BlockSpec(pinv.shape, const3),
            pl.BlockSpec(pfwd.shape, const3),
            pl.BlockSpec(pstep.shape, const3),
            pl.BlockSpec((1, SSM_WIDTH), const2),
            pl.BlockSpec((SSM_WIDTH, SSM_WIDTH), const2),
            pl.BlockSpec((1, SSM_WIDTH), const2),
            pl.BlockSpec((1, SSM_WIDTH), const2),
        ],
        out_specs=pl.BlockSpec((chunk, SSM_WIDTH), lambda b, c: (b * nc + c, o_col)),
        scratch_shapes=[
            pltpu.VMEM((SSM_CLUSTERS, SSM_CL_COLS), jnp.float32),
            pltpu.VMEM((chunk, SSM_WIDTH), jnp.float32),
            pltpu.VMEM((chunk, SSM_CL_COLS), jnp.float32),
            pltpu.VMEM((chunk, SSM_CL_COLS), jnp.float32),
        ],
        input_output_aliases={2: 0},
        compiler_params=pltpu.CompilerParams(
            dimension_semantics=("arbitrary", "arbitrary"), vmem_limit_bytes=vmem),
        name="s5_ssm",
    )(factorised, proj, mixed, bmat, cmat, pinv, pfwd, pstep, d_skip, w_glu, b_glu, out_gain)


def _res_matmul_kernel(a_ref, w_ref, r_ref, o_ref, *maybe_bf16_ref):
    out = r_ref[...] + jnp.dot(a_ref[...], w_ref[...], preferred_element_type=jnp.float32)
    o_ref[...] = out
    for ob_ref in maybe_bf16_ref:
        ob_ref[...] = out.astype(ob_ref.dtype)


def _res_matmul(a, w, res, *, tm, tn, name, with_bf16_copy=False):
    m, k = a.shape
    n = w.shape[1]
    assert m % tm == 0 and n % tn == 0
    vmem = _vmem_limit(2 * _nbytes((tm, k), a.dtype), 2 * _nbytes((k, tn), w.dtype),
                       6 * _nbytes((tm, tn), jnp.float32))
    tile = pl.BlockSpec((tm, tn), lambda i, j: (i, j))
    out_shape = [jax.ShapeDtypeStruct((m, n), jnp.float32)]
    if with_bf16_copy:
        out_shape.append(jax.ShapeDtypeStruct((m, n), jnp.bfloat16))
    outs = pl.pallas_call(
        _res_matmul_kernel,
        out_shape=out_shape,
        grid=(m // tm, n // tn),
        in_specs=[
            pl.BlockSpec((tm, k), lambda i, j: (i, 0)),
            pl.BlockSpec((k, tn), lambda i, j: (0, j)),
            tile,
        ],
        out_specs=[tile] * len(out_shape),
        compiler_params=pltpu.CompilerParams(
            dimension_semantics=("parallel", "parallel"), vmem_limit_bytes=vmem),
        name=name,
    )(a, w, res)
    return outs if with_bf16_copy else outs[0]


def _ffn_up_kernel(x_ref, halo_ref, g_ref, wg_ref, wu_ref, cw_ref, cb_ref, o_ref, h_ref,
                   *, tm, blocks_per_seq):
    i = pl.program_id(0)
    j = pl.program_id(1)
    halo = V7X_BF16_ROWS_PER_TILE

    @pl.when(j == 0)
    def _():
        hh = _rms_rows(halo_ref[...].astype(jnp.float32), g_ref[...])
        hh = jnp.where(i % blocks_per_seq == 0, 0.0, hh)
        h_ref[0:halo, :] = hh.astype(h_ref.dtype)
        _norm_into(x_ref, g_ref, h_ref, rows=tm, row_chunk=16, h_offset=halo)

    g_all = jnp.dot(h_ref[...], wg_ref[...], preferred_element_type=jnp.float32)
    up = jnp.dot(h_ref[halo:, :], wu_ref[...], preferred_element_type=jnp.float32)
    rows = 128
    for r0 in range(0, tm, rows):
        for c0 in range(0, o_ref.shape[1], V7X_LANES):
            cs = slice(c0, c0 + V7X_LANES)
            g = g_all[r0:r0 + halo + rows, cs]
            t = cw_ref[1:2, cs] * g + pltpu.roll(cw_ref[0:1, cs] * g, 1, 0)
            gc = (cb_ref[:, cs] + cw_ref[2:3, cs] * g + pltpu.roll(t, 1, 0))[halo:, :]
            act = gc * (1.0 / (1.0 + jnp.exp(-gc))) * up[r0:r0 + rows, cs]
            o_ref[r0:r0 + rows, cs] = act.astype(o_ref.dtype)


def _ffn_up(x1, gain, wg, wu, conv_w, conv_b, *, seq, tm=1024, tf=F_TILE):
    m, k = x1.shape
    f = wg.shape[1]
    halo = V7X_BF16_ROWS_PER_TILE
    assert m % tm == 0 and seq % tm == 0
    kern = functools.partial(_ffn_up_kernel, tm=tm, blocks_per_seq=seq // tm)
    vmem = _vmem_limit(2 * _nbytes((tm, k), x1.dtype), _nbytes((tm + halo, k), jnp.bfloat16),
                       4 * _nbytes((k, tf), jnp.bfloat16), 2 * _nbytes((tm, tf), jnp.bfloat16),
                       3 * _nbytes((tm + halo, tf), jnp.float32))
    return pl.pallas_call(
        kern,
        out_shape=jax.ShapeDtypeStruct((m, f), jnp.bfloat16),
        grid=(m // tm, pl.cdiv(f, tf)),
        in_specs=[
            pl.BlockSpec((tm, k), lambda i, j: (i, 0)),
            pl.BlockSpec((halo, k), lambda i, j: (jnp.maximum(i * (tm // halo) - 1, 0), 0)),
            pl.BlockSpec((1, k), lambda i, j: (0, 0)),
            pl.BlockSpec((k, tf), lambda i, j: (0, j)),
            pl.BlockSpec((k, tf), lambda i, j: (0, j)),
            pl.BlockSpec((CONV_WIDTH, tf), lambda i, j: (0, j)),
            pl.BlockSpec((1, tf), lambda i, j: (0, j)),
        ],
        out_specs=pl.BlockSpec((tm, tf), lambda i, j: (i, j)),
        scratch_shapes=[pltpu.VMEM((tm + halo, k), jnp.bfloat16)],
        compiler_params=pltpu.CompilerParams(
            dimension_semantics=("arbitrary", "arbitrary"), vmem_limit_bytes=vmem),
        name="ffn_up",
    )(x1, x1, gain, wg, wu, conv_w, conv_b)


def kernel(x, attn_norm, w_in, q_gain, k_gain, lam_q1, lam_k1, lam_q2, lam_k2, sub_gain, ssm_log_dt, ssm_a_re, ssm_a_im, ssm_b_re, ssm_b_im, ssm_c_re, ssm_c_im, ssm_d, ssm_w_glu, ssm_b_glu, ssm_out_gain, w_out, ffn_norm, w_gate, w_up, conv_w, conv_b, w_down):
    b, s, d = x.shape
    depth = w_in.shape[0]
    bf16, f32 = jnp.bfloat16, jnp.float32
    xf = x.reshape(b * s, d)
    for l in range(depth):
        lam_init = 0.8 - 0.6 * math.exp(-0.3 * l)
        q_scale = HEAD_DIM ** -0.5 * math.log2(math.e)
        qg, kg = q_gain[l].astype(f32), k_gain[l].astype(f32)
        col_gain = jnp.concatenate([
            jnp.tile(qg * q_scale, ATTN_WIDTH // HEAD_DIM),
            jnp.tile(kg, ATTN_WIDTH // HEAD_DIM),
            jnp.ones((IN_WIDTH - 2 * ATTN_WIDTH,), f32)])[None, :]
        proj, (wo_bf, wg_bf, wu_bf, wd_bf) = _in_proj(
            xf, attn_norm[l].astype(f32)[None, :], w_in[l].astype(bf16), col_gain,
            (w_out, w_gate, w_up, w_down), l)

        logit_bound = 1.01 * HEAD_DIM * q_scale * jnp.max(jnp.abs(qg)) * jnp.max(jnp.abs(kg))
        small_logits = (logit_bound <= _UNSHIFTED_SOFTMAX_MAX_LOG2).astype(jnp.int32).reshape(1)
        lamv = jnp.stack([lam_q1[l], lam_k1[l], lam_q2[l], lam_k2[l]]).astype(f32)
        mixed = _attention(proj, small_logits, lamv, sub_gain[l].astype(f32)[None, :],
                           batch=b, seq=s, lam_init=lam_init)

        tables = _ssm_tables(ssm_log_dt[l], ssm_a_re[l], ssm_a_im[l], ssm_b_re[l], ssm_b_im[l],
                             ssm_c_re[l], ssm_c_im[l], SSM_CHUNK)
        mixed = _ssm(proj, mixed, tables, ssm_d[l].astype(f32).reshape(1, SSM_WIDTH),
                     ssm_w_glu[l].astype(bf16), ssm_b_glu[l].astype(f32)[None, :],
                     ssm_out_gain[l].astype(f32)[None, :], batch=b, seq=s)

        x1, x1_bf = _res_matmul(mixed, wo_bf, xf, tm=512, tn=1024, name="out_proj",
                                with_bf16_copy=True)

        act = _ffn_up(x1_bf, ffn_norm[l].astype(f32)[None, :], wg_bf, wu_bf, conv_w[l].astype(f32),
                      conv_b[l].astype(f32)[None, :], seq=s)
        xf = _res_matmul(act, wd_bf, x1, tm=512, tn=512, name="ffn_down")
    return xf.reshape(b, s, d)
```

```python
import functools
import math

import jax
import jax.numpy as jnp
from jax import lax
from jax.experimental import pallas as pl
from jax.experimental.pallas import tpu as pltpu

D_MODEL = 4096
ATTN_HEADS = 12
HEAD_DIM = 128
ATTN_WIDTH = ATTN_HEADS * 2 * HEAD_DIM
SSM_WIDTH = D_MODEL - ATTN_WIDTH
SSM_GROUP = 16
SSM_GROUPS = SSM_WIDTH // SSM_GROUP
SSM_STATE = 64
IN_WIDTH = 3 * ATTN_WIDTH + SSM_WIDTH
D_FF = 11008
CONV_WIDTH = 3
EPS = 1e-6

V7X_LANES = 128
V7X_SUBLANES = 8
V7X_BF16_ROWS_PER_TILE = 2 * V7X_SUBLANES
V7X_VMEM_BYTES = 64 * 1024 * 1024
V7X_VMEM_RESERVED_BYTES = 6 * 1024 * 1024

F_TILE = 512
SSM_CLUSTER = V7X_LANES // SSM_GROUP
SSM_CLUSTERS = SSM_GROUPS // SSM_CLUSTER
SSM_CL_STATE = SSM_CLUSTER * SSM_STATE
SSM_CL_COLS = 2 * SSM_CL_STATE
SSM_CHUNK = 256


def _vmem_limit(*nbytes):
    want = int(sum(nbytes)) + 12 * 1024 * 1024
    return min(want, V7X_VMEM_BYTES - V7X_VMEM_RESERVED_BYTES)


def _nbytes(shape, dtype):
    return math.prod(shape) * jnp.dtype(dtype).itemsize


def _rms_rows(x, gain):
    ms = jnp.mean(x * x, axis=-1, keepdims=True)
    return x * lax.rsqrt(ms + EPS) * gain


def _norm_into(x_ref, gain_ref, h_ref, *, rows, row_chunk, h_offset):
    def body(c, carry):
        r = pl.multiple_of(c * row_chunk, row_chunk)
        xb = x_ref[pl.ds(r, row_chunk), :].astype(jnp.float32)
        h_ref[pl.ds(h_offset + r, row_chunk), :] = _rms_rows(xb, gain_ref[...]).astype(h_ref.dtype)
        return carry
    lax.fori_loop(0, rows // row_chunk, body, 0, unroll=4)


def _in_proj_kernel(x_ref, g_ref, w_ref, cg_ref, *refs, tm, tn, n_norm_tiles, n_parts, n_side):
    side_in, o_ref, side_out, h_ref = (refs[:n_side], refs[n_side], refs[n_side + 1:2 * n_side + 1],
                                       refs[2 * n_side + 1])
    j = pl.program_id(1)
    h_ref = h_ref.at[pl.program_id(2)]

    @pl.when(j == 0)
    def _():
        _norm_into(x_ref, g_ref, h_ref, rows=tm, row_chunk=16, h_offset=0)

    is_qk = j < n_norm_tiles
    part = tn // n_parts
    for c0 in range(0, tn, part):
        acc = jnp.dot(h_ref[...], w_ref[:, c0:c0 + part], preferred_element_type=jnp.float32)
        for s0 in range(0, part, HEAD_DIM):
            raw = acc[:, s0:s0 + HEAD_DIM]
            sl = slice(c0 + s0, c0 + s0 + HEAD_DIM)
            o_ref[:, sl] = jnp.where(is_qk, _rms_rows(raw, cg_ref[:, sl]), raw).astype(o_ref.dtype)

    for src, dst in zip(side_in, side_out):
        dst[...] = src[...].astype(dst.dtype)


def _in_proj(x2d, gain, w_bf16, col_gain, side_f32, layer, *, tm=512, tn=1024, row_blocks=2):
    m, k = x2d.shape
    n = w_bf16.shape[1]
    n_norm_tiles = 2 * ATTN_WIDTH // tn
    n_col = n // tn
    n_steps = (m // tm) * n_col
    assert 2 * ATTN_WIDTH % tn == 0 and m % (tm * row_blocks) == 0 and n % tn == 0
    step = lambda i, j, r: (i * n_col + j) * row_blocks + r
    kern = functools.partial(_in_proj_kernel, tm=tm, tn=tn, n_norm_tiles=n_norm_tiles, n_parts=4,
                             n_side=len(side_f32))
    side_in_specs, side_out_specs, side_shapes, side_bytes = [], [], [], 0
    for a in side_f32:
        _, rows, cols = a.shape
        tile = V7X_BF16_ROWS_PER_TILE
        slab = next(s for s in range(tile, rows + 1, tile)
                    if rows % s == 0 and rows // s <= n_steps)
        last = rows // slab - 1
        side_in_specs.append(pl.BlockSpec(
            (None, slab, cols),
            lambda i, j, r, last=last: (layer, jnp.minimum(step(i, j, r), last), 0)))
        side_out_specs.append(pl.BlockSpec(
            (slab, cols), lambda i, j, r, last=last: (jnp.minimum(step(i, j, r), last), 0)))
        side_shapes.append(jax.ShapeDtypeStruct((rows, cols), jnp.bfloat16))
        side_bytes += 2 * _nbytes((slab, cols), jnp.float32) + 2 * _nbytes((slab, cols), jnp.bfloat16)
    vmem = _vmem_limit(2 * _nbytes((tm, k), jnp.float32),
                       row_blocks * _nbytes((tm, k), jnp.bfloat16),
                       2 * _nbytes((k, tn), jnp.bfloat16), 2 * _nbytes((tm, tn), jnp.bfloat16),
                       _nbytes((tm, tn), jnp.float32), side_bytes)
    outs = pl.pallas_call(
        kern,
        out_shape=[jax.ShapeDtypeStruct((m, n), jnp.bfloat16)] + side_shapes,
        grid=(m // (tm * row_blocks), n_col, row_blocks),
        in_specs=[
            pl.BlockSpec((tm, k), lambda i, j, r: (
                i * row_blocks + jnp.where(j == 0, r, row_blocks - 1), 0)),
            pl.BlockSpec((1, k), lambda i, j, r: (0, 0)),
            pl.BlockSpec((k, tn), lambda i, j, r: (0, j)),
            pl.BlockSpec((1, tn), lambda i, j, r: (0, j)),
        ] + side_in_specs,
        out_specs=[pl.BlockSpec((tm, tn), lambda i, j, r: (i * row_blocks + r, j))] + side_out_specs,
        scratch_shapes=[pltpu.VMEM((row_blocks, tm, k), jnp.bfloat16)],
        compiler_params=pltpu.CompilerParams(
            dimension_semantics=("arbitrary", "arbitrary", "arbitrary"), vmem_limit_bytes=vmem),
        name="in_proj",
    )(x2d, gain, w_bf16, col_gain, *side_f32)
    return outs[0], outs[1:]


_UNSHIFTED_SOFTMAX_MAX_LOG2 = 60.0


def _qk_scores(q, k, row_offset):
    s = lax.dot_general(q, k, (((1,), (1,)), ((), ())), preferred_element_type=jnp.float32)
    if row_offset is not None:
        row = lax.broadcasted_iota(jnp.int32, s.shape, 0)
        col = lax.broadcasted_iota(jnp.int32, s.shape, 1)
        s = jnp.where(col <= row + row_offset, s, -jnp.inf)
    return s


def _lane_partial_sums(p):
    ps = p[:, 0:V7X_LANES]
    for c in range(1, p.shape[1] // V7X_LANES):
        ps = ps + p[:, c * V7X_LANES:(c + 1) * V7X_LANES]
    return ps


def _attn_kernel(small_ref, lamv_ref, q_ref, k_ref, v_ref, sg_ref, o_ref,
                 m_ref, l_ref, acc_ref, p_ref, *, blk, nq, lam_init):
    unshifted = small_ref[0] == 1
    lamv = lamv_ref[...]
    lam = (jnp.exp(jnp.sum(lamv[0:1] * lamv[1:2], axis=-1, keepdims=True))
           - jnp.exp(jnp.sum(lamv[2:3] * lamv[3:4], axis=-1, keepdims=True)) + lam_init)

    def head(mi):
        return slice(mi * HEAD_DIM, (mi + 1) * HEAD_DIM)

    half = blk // 2

    def stacked(hf, mi):
        return slice((2 * hf + mi) * half, (2 * hf + mi + 1) * half)

    def q_block(qi, carry):
        q0 = pl.multiple_of(qi * blk, blk)
        l_ref[...] = jnp.zeros(l_ref.shape, jnp.float32)
        acc_ref[...] = jnp.zeros(acc_ref.shape, jnp.float32)

        def tile(hfs, start, width, mask_offset, shifted):
            ks = pl.ds(start, width)
            n_rows = len(hfs) * half
            qs = pl.ds(q0 + hfs[0] * half, n_rows)
            for mi in range(2):
                s = _qk_scores(q_ref[qs, head(mi)], k_ref[ks, head(mi)], mask_offset)
                if not shifted:
                    p = jnp.exp2(s)
                    ps = _lane_partial_sums(p)
                for n, hf in enumerate(hfs):
                    st = stacked(hf, mi)
                    local = slice(n * half, (n + 1) * half)
                    if shifted:
                        m_old = m_ref[st, :]
                        m_new = jnp.maximum(m_old, jnp.max(s[local], axis=-1, keepdims=True))
                        alpha = jnp.exp2(m_old - m_new)
                        ph = jnp.exp2(s[local] - m_new)
                        l_ref[st, 0:1] = alpha * l_ref[st, 0:1] + jnp.sum(ph, axis=-1, keepdims=True)
                        acc_ref[st, :] = alpha * acc_ref[st, :]
                        m_ref[st, :] = m_new
                        p_ref[st, 0:width] = ph.astype(p_ref.dtype)
                    else:
                        l_ref[st, :] += ps[local]
                        p_ref[st, 0:width] = p[local].astype(p_ref.dtype)
            both_maps = slice(2 * hfs[0] * half, 2 * (hfs[-1] + 1) * half)
            acc_ref[both_maps, :] += jnp.dot(p_ref[both_maps, 0:width], v_ref[ks, :],
                                             preferred_element_type=jnp.float32)

        def causal_tiles(shifted):
            def body(j, c):
                tile((0, 1), pl.multiple_of(j * blk, blk), blk, None, shifted)
                return c
            lax.fori_loop(0, qi, body, 0)
            tile((0,), q0, half, 0, shifted)
            tile((1,), q0, blk, half, shifted)

        @pl.when(unshifted)
        def _():
            causal_tiles(False)

        @pl.when(jnp.logical_not(unshifted))
        def _():
            m_ref[...] = jnp.full(m_ref.shape, -jnp.inf, jnp.float32)
            causal_tiles(True)

        for hf in range(2):
            r1 = 1.0 / jnp.sum(l_ref[stacked(hf, 0), :], axis=-1, keepdims=True)
            r2 = lam / jnp.sum(l_ref[stacked(hf, 1), :], axis=-1, keepdims=True)
            o = acc_ref[stacked(hf, 0), :] * r1 - acc_ref[stacked(hf, 1), :] * r2
            o_ref[pl.ds(q0 + hf * half, half), :] = (
                _rms_rows(o, sg_ref[...]) * (1.0 - lam_init)).astype(o_ref.dtype)
        return carry

    lax.fori_loop(0, nq, q_block, 0)


def _attention(proj, small_logits, lamv, sub_gain, *, batch, seq, lam_init, blk=1024):
    nq = seq // blk
    hw = 2 * HEAD_DIM
    kern = functools.partial(_attn_kernel, blk=blk, nq=nq, lam_init=lam_init)
    vmem = _vmem_limit(8 * _nbytes((seq, hw), jnp.bfloat16),
                       _nbytes((2 * blk, hw), jnp.float32), _nbytes((2 * blk, blk), jnp.bfloat16),
                       4 * _nbytes((blk, V7X_LANES), jnp.float32),
                       4 * _nbytes((blk, blk), jnp.float32))
    return pl.pallas_call(
        kern,
        out_shape=jax.ShapeDtypeStruct((batch * seq, D_MODEL), jnp.bfloat16),
        grid=(batch, ATTN_HEADS),
        in_specs=[
            pl.BlockSpec(memory_space=pltpu.SMEM),
            pl.BlockSpec((4, HEAD_DIM), lambda b, h: (0, 0)),
            pl.BlockSpec((seq, hw), lambda b, h: (b, h)),
            pl.BlockSpec((seq, hw), lambda b, h: (b, ATTN_HEADS + h)),
            pl.BlockSpec((seq, hw), lambda b, h: (b, 2 * ATTN_HEADS + h)),
            pl.BlockSpec((1, hw), lambda b, h: (0, 0)),
        ],
        out_specs=pl.BlockSpec((seq, hw), lambda b, h: (b, h)),
        scratch_shapes=[
            pltpu.VMEM((2 * blk, 1), jnp.float32),
            pltpu.VMEM((2 * blk, V7X_LANES), jnp.float32),
            pltpu.VMEM((2 * blk, hw), jnp.float32),
            pltpu.VMEM((2 * blk, blk), jnp.bfloat16),
        ],
        compiler_params=pltpu.CompilerParams(
            dimension_semantics=("arbitrary", "arbitrary"), vmem_limit_bytes=vmem),
        name="diff_attn",
    )(small_logits, lamv, proj, proj, proj, sub_gain)


_SSM_FACTORISED_MAX_LOG = 60.0


def _cmul(ar, ai, br, bi):
    return ar * br - ai * bi, ar * bi + ai * br


def _ssm_kernel(flag_ref, u_ref, mixed_ref, bmat_ref, cmat_ref, pinv_ref, pfwd_ref, pstep_ref, d_ref,
                wglu_ref, bglu_ref, og_ref, o_ref, h_ref, y_ref, bu_ref, hs_ref, *, chunk):
    del mixed_ref
    c = pl.program_id(1)

    @pl.when(c == 0)
    def _():
        h_ref[...] = jnp.zeros(h_ref.shape, jnp.float32)

    n = SSM_CL_STATE
    factorised = flag_ref[0] == 1

    def lanes_of(cl):
        return slice(cl * V7X_LANES, (cl + 1) * V7X_LANES)

    @pl.when(factorised)
    def _():
        row = lax.broadcasted_iota(jnp.int32, (chunk, chunk), 0)
        col = lax.broadcasted_iota(jnp.int32, (chunk, chunk), 1)
        tri = (col <= row).astype(jnp.bfloat16)
        for cl in range(SSM_CLUSTERS):
            bu = jnp.dot(u_ref[:, lanes_of(cl)], bmat_ref[cl], preferred_element_type=jnp.float32)
            xr, xi = _cmul(pinv_ref[cl, :, :n], pinv_ref[cl, :, n:], bu[:, :n], bu[:, n:])
            x = jnp.concatenate([xr, xi], axis=1).astype(jnp.bfloat16)
            cs = jnp.dot(tri, x, preferred_element_type=jnp.float32)
            cr, ci = _cmul(pstep_ref[1, cl:cl + 1, :n], pstep_ref[1, cl:cl + 1, n:],
                           h_ref[cl:cl + 1, :n], h_ref[cl:cl + 1, n:])
            hr, hi = _cmul(pfwd_ref[cl, :, :n], pfwd_ref[cl, :, n:], cs[:, :n] + cr, cs[:, n:] + ci)
            h_ref[cl:cl + 1, :n] = hr[chunk - 1:chunk, :]
            h_ref[cl:cl + 1, n:] = hi[chunk - 1:chunk, :]
            h = jnp.concatenate([hr, hi], axis=1).astype(jnp.bfloat16)
            y_ref[:, lanes_of(cl)] = jnp.dot(h, cmat_ref[cl], preferred_element_type=jnp.float32)

    @pl.when(jnp.logical_not(factorised))
    def _():
        for cl in range(SSM_CLUSTERS):
            bu_ref[...] = jnp.dot(u_ref[:, lanes_of(cl)], bmat_ref[cl],
                                  preferred_element_type=jnp.float32)
            lr, li = pstep_ref[0, cl:cl + 1, :n], pstep_ref[0, cl:cl + 1, n:]

            def step(t, carry):
                hr, hi = carry
                ar, ai = _cmul(lr, li, hr, hi)
                hr = ar + bu_ref[pl.ds(t, 1), :n]
                hi = ai + bu_ref[pl.ds(t, 1), n:]
                hs_ref[pl.ds(t, 1), :n] = hr
                hs_ref[pl.ds(t, 1), n:] = hi
                return hr, hi

            hr, hi = lax.fori_loop(0, chunk, step, (h_ref[cl:cl + 1, :n], h_ref[cl:cl + 1, n:]))
            h_ref[cl:cl + 1, :n] = hr
            h_ref[cl:cl + 1, n:] = hi
            y_ref[:, lanes_of(cl)] = jnp.dot(hs_ref[...].astype(jnp.bfloat16), cmat_ref[cl],
                                             preferred_element_type=jnp.float32)

    y = y_ref[...] + d_ref[...] * u_ref[...].astype(jnp.float32)
    y = jax.nn.gelu(y, approximate=True)
    z = jnp.dot(y.astype(jnp.bfloat16), wglu_ref[...], preferred_element_type=jnp.float32)
    z = z + bglu_ref[...]
    y = y * (1.0 / (1.0 + jnp.exp(-z)))
    o_ref[...] = _rms_rows(y, og_ref[...]).astype(o_ref.dtype)


def _ssm_tables(log_dt, a_re, a_im, b_re, b_im, c_re, c_im, chunk):
    f32 = jnp.float32
    n, c = SSM_STATE, SSM_GROUP
    dt = jnp.exp(log_dt.astype(f32))[:, None]
    ar, ai = a_re.astype(f32), a_im.astype(f32)
    mag = jnp.exp(ar * dt)
    lb_re = mag * jnp.cos(ai * dt)
    lb_im = mag * jnp.sin(ai * dt)
    pr, pi_ = lb_re - 1.0, lb_im
    den = ar * ar + ai * ai
    z_re = ((pr * ar + pi_ * ai) / den)[..., None]
    z_im = ((pi_ * ar - pr * ai) / den)[..., None]
    br, bi = b_re.astype(f32), b_im.astype(f32)
    bb_re = z_re * br - z_im * bi
    bb_im = z_re * bi + z_im * br

    def pair_cols(re, im):
        return jnp.concatenate([re, im], axis=-1)

    log_re = (ar * dt).reshape(SSM_CLUSTERS, 1, SSM_CLUSTER * n)
    log_im = (ai * dt).reshape(SSM_CLUSTERS, 1, SSM_CLUSTER * n)

    def table(k):
        kk = k.astype(f32)[None, :, None]
        pm = jnp.exp(kk * log_re)
        ph = kk * log_im
        return pair_cols(pm * jnp.cos(ph), pm * jnp.sin(ph))

    c0 = chunk // 2
    t = jnp.arange(chunk)
    pfwd = table(t - c0)
    pinv = table(c0 - t)
    pstep = table(jnp.array([1, c0 + 1])).transpose(1, 0, 2)
    factorised = ((c0 + 1) * jnp.max(jnp.abs(log_re)) <= _SSM_FACTORISED_MAX_LOG)

    eye = jnp.eye(SSM_CLUSTER, dtype=f32)
    def bpart(bb):
        b4 = bb.reshape(SSM_CLUSTERS, SSM_CLUSTER, n, c)
        return jnp.einsum('kgnc,gh->kgchn', b4, eye).reshape(
            SSM_CLUSTERS, V7X_LANES, SSM_CLUSTER * n)
    bmat = pair_cols(bpart(bb_re), bpart(bb_im)).astype(jnp.bfloat16)
    def cpart(cc):
        c4 = cc.astype(f32).reshape(SSM_CLUSTERS, SSM_CLUSTER, c, n)
        return jnp.einsum('kgcn,gh->khcgn', c4, eye).reshape(
            SSM_CLUSTERS, V7X_LANES, SSM_CLUSTER * n)
    cmat = pair_cols(cpart(c_re), -cpart(c_im)).transpose(0, 2, 1).astype(jnp.bfloat16)
    return factorised.astype(jnp.int32).reshape(1), bmat, cmat, pinv, pfwd, pstep


def _ssm(proj, mixed, tables, d_skip, w_glu, b_glu, out_gain, *, batch, seq, chunk=SSM_CHUNK):
    factorised, bmat, cmat, pinv, pfwd, pstep = tables
    nc = seq // chunk
    u_col = 3 * ATTN_WIDTH // SSM_WIDTH
    o_col = ATTN_WIDTH // SSM_WIDTH
    kern = functools.partial(_ssm_kernel, chunk=chunk)
    const3 = lambda b, c: (0, 0, 0)
    const2 = lambda b, c: (0, 0)
    vmem = _vmem_limit(2 * (bmat.nbytes + cmat.nbytes + pinv.nbytes + pfwd.nbytes + pstep.nbytes),
                       4 * _nbytes((SSM_WIDTH, SSM_WIDTH), jnp.bfloat16),
                       10 * _nbytes((chunk, SSM_WIDTH), jnp.float32))
    return pl.pallas_call(
        kern,
        out_shape=jax.ShapeDtypeStruct(mixed.shape, mixed.dtype),
        grid=(batch, nc),
        in_specs=[
            pl.BlockSpec(memory_space=pltpu.SMEM),
            pl.BlockSpec((chunk, SSM_WIDTH), lambda b, c: (b * nc + c, u_col)),
            pl.BlockSpec(memory_space=pl.ANY),
            pl.BlockSpec(bmat.shape, const3),
            pl.BlockSpec(cmat.shape, const3),
            pl.BlockSpec(pinv.shape, const3),
            pl.BlockSpec(pfwd.shape, const3),
            pl.BlockSpec(pstep.shape, const3),
            pl.BlockSpec((1, SSM_WIDTH), const2),
            pl.BlockSpec((SSM_WIDTH, SSM_WIDTH), const2),
            pl.BlockSpec((1, SSM_WIDTH), const2),
            pl.BlockSpec((1, SSM_WIDTH), const2),
        ],
        out_specs=pl.BlockSpec((chunk, SSM_WIDTH), lambda b, c: (b * nc + c, o_col)),
        scratch_shapes=[
            pltpu.VMEM((SSM_CLUSTERS, SSM_CL_COLS), jnp.float32),
            pltpu.VMEM((chunk, SSM_WIDTH), jnp.float32),
            pltpu.VMEM((chunk, SSM_CL_COLS), jnp.float32),
            pltpu.VMEM((chunk, SSM_CL_COLS), jnp.float32),
        ],
        input_output_aliases={2: 0},
        compiler_params=pltpu.CompilerParams(
            dimension_semantics=("arbitrary", "arbitrary"), vmem_limit_bytes=vmem),
        name="s5_ssm",
    )(factorised, proj, mixed, bmat, cmat, pinv, pfwd, pstep, d_skip, w_glu, b_glu, out_gain)


def _res_matmul_kernel(a_ref, w_ref, r_ref, o_ref, *maybe_bf16_ref):
    out = r_ref[...] + jnp.dot(a_ref[...], w_ref[...], preferred_element_type=jnp.float32)
    o_ref[...] = out
    for ob_ref in maybe_bf16_ref:
        ob_ref[...] = out.astype(ob_ref.dtype)


def _res_matmul(a, w, res, *, tm, tn, name, with_bf16_copy=False):
    m, k = a.shape
    n = w.shape[1]
    assert m % tm == 0 and n % tn == 0
    vmem = _vmem_limit(2 * _nbytes((tm, k), a.dtype), 2 * _nbytes((k, tn), w.dtype),
                       6 * _nbytes((tm, tn), jnp.float32))
    tile = pl.BlockSpec((tm, tn), lambda i, j: (i, j))
    out_shape = [jax.ShapeDtypeStruct((m, n), jnp.float32)]
    if with_bf16_copy:
        out_shape.append(jax.ShapeDtypeStruct((m, n), jnp.bfloat16))
    outs = pl.pallas_call(
        _res_matmul_kernel,
        out_shape=out_shape,
        grid=(m // tm, n // tn),
        in_specs=[
            pl.BlockSpec((tm, k), lambda i, j: (i, 0)),
            pl.BlockSpec((k, tn), lambda i, j: (0, j)),
            tile,
        ],
        out_specs=[tile] * len(out_shape),
        compiler_params=pltpu.CompilerParams(
            dimension_semantics=("parallel", "parallel"), vmem_limit_bytes=vmem),
        name=name,
    )(a, w, res)
    return outs if with_bf16_copy else outs[0]


def _ffn_up_kernel(x_ref, halo_ref, g_ref, wg_ref, wu_ref, cw_ref, cb_ref, o_ref, h_ref,
                   *, tm, blocks_per_seq):
    i = pl.program_id(0)
    j = pl.program_id(1)
    halo = V7X_BF16_ROWS_PER_TILE

    @pl.when(j == 0)
    def _():
        hh = _rms_rows(halo_ref[...].astype(jnp.float32), g_ref[...])
        hh = jnp.where(i % blocks_per_seq == 0, 0.0, hh)
        h_ref[0:halo, :] = hh.astype(h_ref.dtype)
        _norm_into(x_ref, g_ref, h_ref, rows=tm, row_chunk=16, h_offset=halo)

    g_all = jnp.dot(h_ref[...], wg_ref[...], preferred_element_type=jnp.float32)
    up = jnp.dot(h_ref[halo:, :], wu_ref[...], preferred_element_type=jnp.float32)
    rows = 128
    for r0 in range(0, tm, rows):
        for c0 in range(0, o_ref.shape[1], V7X_LANES):
            cs = slice(c0, c0 + V7X_LANES)
            g = g_all[r0:r0 + halo + rows, cs]
            t = cw_ref[1:2, cs] * g + pltpu.roll(cw_ref[0:1, cs] * g, 1, 0)
            gc = (cb_ref[:, cs] + cw_ref[2:3, cs] * g + pltpu.roll(t, 1, 0))[halo:, :]
            act = gc * (1.0 / (1.0 + jnp.exp(-gc))) * up[r0:r0 + rows, cs]
            o_ref[r0:r0 + rows, cs] = act.astype(o_ref.dtype)


def _ffn_up(x1, gain, wg, wu, conv_w, conv_b, *, seq, tm=1024, tf=F_TILE):
    m, k = x1.shape
    f = wg.shape[1]
    halo = V7X_BF16_ROWS_PER_TILE
    assert m % tm == 0 and seq % tm == 0
    kern = functools.partial(_ffn_up_kernel, tm=tm, blocks_per_seq=seq // tm)
    vmem = _vmem_limit(2 * _nbytes((tm, k), x1.dtype), _nbytes((tm + halo, k), jnp.bfloat16),
                       4 * _nbytes((k, tf), jnp.bfloat16), 2 * _nbytes((tm, tf), jnp.bfloat16),
                       3 * _nbytes((tm + halo, tf), jnp.float32))
    return pl.pallas_call(
        kern,
        out_shape=jax.ShapeDtypeStruct((m, f), jnp.bfloat16),
        grid=(m // tm, pl.cdiv(f, tf)),
        in_specs=[
            pl.BlockSpec((tm, k), lambda i, j: (i, 0)),
            pl.BlockSpec((halo, k), lambda i, j: (jnp.maximum(i * (tm // halo) - 1, 0), 0)),
            pl.BlockSpec((1, k), lambda i, j: (0, 0)),
            pl.BlockSpec((k, tf), lambda i, j: (0, j)),
            pl.BlockSpec((k, tf), lambda i, j: (0, j)),
            pl.BlockSpec((CONV_WIDTH, tf), lambda i, j: (0, j)),
            pl.BlockSpec((1, tf), lambda i, j: (0, j)),
        ],
        out_specs=pl.BlockSpec((tm, tf), lambda i, j: (i, j)),
        scratch_shapes=[pltpu.VMEM((tm + halo, k), jnp.bfloat16)],
        compiler_params=pltpu.CompilerParams(
            dimension_semantics=("arbitrary", "arbitrary"), vmem_limit_bytes=vmem),
        name="ffn_up",
    )(x1, x1, gain, wg, wu, conv_w, conv_b)


def kernel(x, attn_norm, w_in, q_gain, k_gain, lam_q1, lam_k1, lam_q2, lam_k2, sub_gain, ssm_log_dt, ssm_a_re, ssm_a_im, ssm_b_re, ssm_b_im, ssm_c_re, ssm_c_im, ssm_d, ssm_w_glu, ssm_b_glu, ssm_out_gain, w_out, ffn_norm, w_gate, w_up, conv_w, conv_b, w_down):
    b, s, d = x.shape
    depth = w_in.shape[0]
    bf16, f32 = jnp.bfloat16, jnp.float32
    xf = x.reshape(b * s, d)
    for l in range(depth):
        lam_init = 0.8 - 0.6 * math.exp(-0.3 * l)
        q_scale = HEAD_DIM ** -0.5 * math.log2(math.e)
        qg, kg = q_gain[l].astype(f32), k_gain[l].astype(f32)
        col_gain = jnp.concatenate([
            jnp.tile(qg * q_scale, ATTN_WIDTH // HEAD_DIM),
            jnp.tile(kg, ATTN_WIDTH // HEAD_DIM),
            jnp.ones((IN_WIDTH - 2 * ATTN_WIDTH,), f32)])[None, :]
        proj, (wo_bf, wg_bf, wu_bf, wd_bf) = _in_proj(
            xf, attn_norm[l].astype(f32)[None, :], w_in[l].astype(bf16), col_gain,
            (w_out, w_gate, w_up, w_down), l)

        logit_bound = 1.01 * HEAD_DIM * q_scale * jnp.max(jnp.abs(qg)) * jnp.max(jnp.abs(kg))
        small_logits = (logit_bound <= _UNSHIFTED_SOFTMAX_MAX_LOG2).astype(jnp.int32).reshape(1)
        lamv = jnp.stack([lam_q1[l], lam_k1[l], lam_q2[l], lam_k2[l]]).astype(f32)
        mixed = _attention(proj, small_logits, lamv, sub_gain[l].astype(f32)[None, :],
                           batch=b, seq=s, lam_init=lam_init)

        tables = _ssm_tables(ssm_log_dt[l], ssm_a_re[l], ssm_a_im[l], ssm_b_re[l], ssm_b_im[l],
                             ssm_c_re[l], ssm_c_im[l], SSM_CHUNK)
        mixed = _ssm(proj, mixed, tables, ssm_d[l].astype(f32).reshape(1, SSM_WIDTH),
                     ssm_w_glu[l].astype(bf16), ssm_b_glu[l].astype(f32)[None, :],
                     ssm_out_gain[l].astype(f32)[None, :], batch=b, seq=s)

        x1, x1_bf = _res_matmul(mixed, wo_bf, xf, tm=1024, tn=512, name="out_proj",
                                with_bf16_copy=True)

        act = _ffn_up(x1_bf, ffn_norm[l].astype(f32)[None, :], wg_bf, wu_bf, conv_w[l].astype(f32),
                      conv_b[l].astype(f32)[None, :], seq=s)
        xf = _res_matmul(act, wd_bf, x1, tm=512, tn=512, name="ffn_down")
    return xf.reshape(b, s, d)
```

```python
import functools
import math

import jax
import jax.numpy as jnp
from jax import lax
from jax.experimental import pallas as pl
from jax.experimental.pallas import tpu as pltpu

D_MODEL = 4096
ATTN_HEADS = 12
HEAD_DIM = 128
ATTN_WIDTH = ATTN_HEADS * 2 * HEAD_DIM
SSM_WIDTH = D_MODEL - ATTN_WIDTH
SSM_GROUP = 16
SSM_GROUPS = SSM_WIDTH // SSM_GROUP
SSM_STATE = 64
IN_WIDTH = 3 * ATTN_WIDTH + SSM_WIDTH
D_FF = 11008
CONV_WIDTH = 3
EPS = 1e-6

V7X_LANES = 128
V7X_SUBLANES = 8
V7X_BF16_ROWS_PER_TILE = 2 * V7X_SUBLANES
V7X_VMEM_BYTES = 64 * 1024 * 1024
V7X_VMEM_RESERVED_BYTES = 6 * 1024 * 1024

F_TILE = 512
SSM_CLUSTER = V7X_LANES // SSM_GROUP
SSM_CLUSTERS = SSM_GROUPS // SSM_CLUSTER
SSM_CL_STATE = SSM_CLUSTER * SSM_STATE
SSM_CL_COLS = 2 * SSM_CL_STATE
SSM_CHUNK = 256


def _vmem_limit(*nbytes):
    want = int(sum(nbytes)) + 12 * 1024 * 1024
    return min(want, V7X_VMEM_BYTES - V7X_VMEM_RESERVED_BYTES)


def _nbytes(shape, dtype):
    return math.prod(shape) * jnp.dtype(dtype).itemsize


def _rms_rows(x, gain):
    ms = jnp.mean(x * x, axis=-1, keepdims=True)
    return x * lax.rsqrt(ms + EPS) * gain


def _norm_into(x_ref, gain_ref, h_ref, *, rows, row_chunk, h_offset):
    def body(c, carry):
        r = pl.multiple_of(c * row_chunk, row_chunk)
        xb = x_ref[pl.ds(r, row_chunk), :].astype(jnp.float32)
        h_ref[pl.ds(h_offset + r, row_chunk), :] = _rms_rows(xb, gain_ref[...]).astype(h_ref.dtype)
        return carry
    lax.fori_loop(0, rows // row_chunk, body, 0, unroll=4)


def _in_proj_kernel(x_ref, g_ref, w_ref, cg_ref, *refs, tm, tn, n_norm_tiles, n_parts, n_side):
    side_in, o_ref, side_out, h_ref = (refs[:n_side], refs[n_side], refs[n_side + 1:2 * n_side + 1],
                                       refs[2 * n_side + 1])
    j = pl.program_id(1)
    h_ref = h_ref.at[pl.program_id(2)]

    @pl.when(j == 0)
    def _():
        _norm_into(x_ref, g_ref, h_ref, rows=tm, row_chunk=16, h_offset=0)

    is_qk = j < n_norm_tiles
    part = tn // n_parts
    for c0 in range(0, tn, part):
        acc = jnp.dot(h_ref[...], w_ref[:, c0:c0 + part], preferred_element_type=jnp.float32)
        for s0 in range(0, part, HEAD_DIM):
            raw = acc[:, s0:s0 + HEAD_DIM]
            sl = slice(c0 + s0, c0 + s0 + HEAD_DIM)
            o_ref[:, sl] = jnp.where(is_qk, _rms_rows(raw, cg_ref[:, sl]), raw).astype(o_ref.dtype)

    for src, dst in zip(side_in, side_out):
        dst[...] = src[...].astype(dst.dtype)


def _in_proj(x2d, gain, w_bf16, col_gain, side_f32, layer, *, tm=512, tn=1024, row_blocks=2):
    m, k = x2d.shape
    n = w_bf16.shape[1]
    n_norm_tiles = 2 * ATTN_WIDTH // tn
    n_col = n // tn
    n_steps = (m // tm) * n_col
    assert 2 * ATTN_WIDTH % tn == 0 and m % (tm * row_blocks) == 0 and n % tn == 0
    step = lambda i, j, r: (i * n_col + j) * row_blocks + r
    kern = functools.partial(_in_proj_kernel, tm=tm, tn=tn, n_norm_tiles=n_norm_tiles, n_parts=4,
                             n_side=len(side_f32))
    side_in_specs, side_out_specs, side_shapes, side_bytes = [], [], [], 0
    for a in side_f32:
        _, rows, cols = a.shape
        tile = V7X_BF16_ROWS_PER_TILE
        slab = next(s for s in range(tile, rows + 1, tile)
                    if rows % s == 0 and rows // s <= n_steps)
        last = rows // slab - 1
        side_in_specs.append(pl.BlockSpec(
            (None, slab, cols),
            lambda i, j, r, last=last: (layer, jnp.minimum(step(i, j, r), last), 0)))
        side_out_specs.append(pl.BlockSpec(
            (slab, cols), lambda i, j, r, last=last: (jnp.minimum(step(i, j, r), last), 0)))
        side_shapes.append(jax.ShapeDtypeStruct((rows, cols), jnp.bfloat16))
        side_bytes += 2 * _nbytes((slab, cols), jnp.float32) + 2 * _nbytes((slab, cols), jnp.bfloat16)
    vmem = _vmem_limit(2 * _nbytes((tm, k), jnp.float32),
                       row_blocks * _nbytes((tm, k), jnp.bfloat16),
                       2 * _nbytes((k, tn), jnp.bfloat16), 2 * _nbytes((tm, tn), jnp.bfloat16),
                       _nbytes((tm, tn), jnp.float32), side_bytes)
    outs = pl.pallas_call(
        kern,
        out_shape=[jax.ShapeDtypeStruct((m, n), jnp.bfloat16)] + side_shapes,
        grid=(m // (tm * row_blocks), n_col, row_blocks),
        in_specs=[
            pl.BlockSpec((tm, k), lambda i, j, r: (
                i * row_blocks + jnp.where(j == 0, r, row_blocks - 1), 0)),
            pl.BlockSpec((1, k), lambda i, j, r: (0, 0)),
            pl.BlockSpec((k, tn), lambda i, j, r: (0, j)),
            pl.BlockSpec((1, tn), lambda i, j, r: (0, j)),
        ] + side_in_specs,
        out_specs=[pl.BlockSpec((tm, tn), lambda i, j, r: (i * row_blocks + r, j))] + side_out_specs,
        scratch_shapes=[pltpu.VMEM((row_blocks, tm, k), jnp.bfloat16)],
        compiler_params=pltpu.CompilerParams(
            dimension_semantics=("arbitrary", "arbitrary", "arbitrary"), vmem_limit_bytes=vmem),
        name="in_proj",
    )(x2d, gain, w_bf16, col_gain, *side_f32)
    return outs[0], outs[1:]


_UNSHIFTED_SOFTMAX_MAX_LOG2 = 60.0


def _qk_scores(q, k, row_offset):
    s = lax.dot_general(q, k, (((1,), (1,)), ((), ())), preferred_element_type=jnp.float32)
    if row_offset is not None:
        row = lax.broadcasted_iota(jnp.int32, s.shape, 0)
        col = lax.broadcasted_iota(jnp.int32, s.shape, 1)
        s = jnp.where(col <= row + row_offset, s, -jnp.inf)
    return s


def _lane_partial_sums(p):
    ps = p[:, 0:V7X_LANES]
    for c in range(1, p.shape[1] // V7X_LANES):
        ps = ps + p[:, c * V7X_LANES:(c + 1) * V7X_LANES]
    return ps


def _attn_kernel(small_ref, lamv_ref, q_ref, k_ref, v_ref, sg_ref, o_ref,
                 m_ref, l_ref, acc_ref, p_ref, *, blk, nq, lam_init):
    unshifted = small_ref[0] == 1
    lamv = lamv_ref[...]
    lam = (jnp.exp(jnp.sum(lamv[0:1] * lamv[1:2], axis=-1, keepdims=True))
           - jnp.exp(jnp.sum(lamv[2:3] * lamv[3:4], axis=-1, keepdims=True)) + lam_init)
    out_gain = sg_ref[...] * (1.0 - lam_init)

    def head(mi):
        return slice(mi * HEAD_DIM, (mi + 1) * HEAD_DIM)

    half = blk // 2

    def stacked(hf, mi):
        return slice((2 * hf + mi) * half, (2 * hf + mi + 1) * half)

    def q_block(qi, carry):
        q0 = pl.multiple_of(qi * blk, blk)

        def tile(hfs, start, width, mask_offset, shifted, first):
            ks = pl.ds(start, width)
            n_rows = len(hfs) * half
            qs = pl.ds(q0 + hfs[0] * half, n_rows)
            for mi in range(2):
                s = _qk_scores(q_ref[qs, head(mi)], k_ref[ks, head(mi)], mask_offset)
                if not shifted:
                    p = jnp.exp2(s)
                    ps = _lane_partial_sums(p)
                for n, hf in enumerate(hfs):
                    st = stacked(hf, mi)
                    local = slice(n * half, (n + 1) * half)
                    if shifted:
                        m_cur = jnp.max(s[local], axis=-1, keepdims=True)
                        m_new = m_cur if first else jnp.maximum(m_ref[st, :], m_cur)
                        ph = jnp.exp2(s[local] - m_new)
                        l_cur = jnp.sum(ph, axis=-1, keepdims=True)
                        if first:
                            l_ref[st, :] = jnp.zeros((half, V7X_LANES), jnp.float32)
                            l_ref[st, 0:1] = l_cur
                        else:
                            alpha = jnp.exp2(m_ref[st, :] - m_new)
                            l_ref[st, 0:1] = alpha * l_ref[st, 0:1] + l_cur
                            acc_ref[st, :] = alpha * acc_ref[st, :]
                        m_ref[st, :] = m_new
                        p_ref[st, 0:width] = ph.astype(p_ref.dtype)
                    else:
                        if first:
                            l_ref[st, :] = ps[local]
                        else:
                            l_ref[st, :] += ps[local]
                        p_ref[st, 0:width] = p[local].astype(p_ref.dtype)
            both_maps = slice(2 * hfs[0] * half, 2 * (hfs[-1] + 1) * half)
            pv = jnp.dot(p_ref[both_maps, 0:width], v_ref[ks, :], preferred_element_type=jnp.float32)
            if first:
                acc_ref[both_maps, :] = pv
            else:
                acc_ref[both_maps, :] += pv

        def causal_tiles(shifted):
            tile((0,), q0, half, 0, shifted, True)
            tile((1,), q0, blk, half, shifted, True)

            def body(j, c):
                tile((0, 1), pl.multiple_of(j * blk, blk), blk, None, shifted, False)
                return c
            lax.fori_loop(0, qi, body, 0)

        @pl.when(unshifted)
        def _():
            causal_tiles(False)

        @pl.when(jnp.logical_not(unshifted))
        def _():
            causal_tiles(True)

        for hf in range(2):
            r1 = 1.0 / jnp.sum(l_ref[stacked(hf, 0), :], axis=-1, keepdims=True)
            r2 = lam / jnp.sum(l_ref[stacked(hf, 1), :], axis=-1, keepdims=True)
            o = acc_ref[stacked(hf, 0), :] * r1 - acc_ref[stacked(hf, 1), :] * r2
            o_ref[pl.ds(q0 + hf * half, half), :] = _rms_rows(o, out_gain).astype(o_ref.dtype)
        return carry

    lax.fori_loop(0, nq, q_block, 0)


def _attention(proj, small_logits, lamv, sub_gain, *, batch, seq, lam_init, blk=1024):
    nq = seq // blk
    hw = 2 * HEAD_DIM
    kern = functools.partial(_attn_kernel, blk=blk, nq=nq, lam_init=lam_init)
    vmem = _vmem_limit(8 * _nbytes((seq, hw), jnp.bfloat16),
                       _nbytes((2 * blk, hw), jnp.float32), _nbytes((2 * blk, blk), jnp.bfloat16),
                       4 * _nbytes((blk, V7X_LANES), jnp.float32),
                       4 * _nbytes((blk, blk), jnp.float32))
    return pl.pallas_call(
        kern,
        out_shape=jax.ShapeDtypeStruct((batch * seq, D_MODEL), jnp.bfloat16),
        grid=(batch, ATTN_HEADS),
        in_specs=[
            pl.BlockSpec(memory_space=pltpu.SMEM),
            pl.BlockSpec((4, HEAD_DIM), lambda b, h: (0, 0)),
            pl.BlockSpec((seq, hw), lambda b, h: (b, h)),
            pl.BlockSpec((seq, hw), lambda b, h: (b, ATTN_HEADS + h)),
            pl.BlockSpec((seq, hw), lambda b, h: (b, 2 * ATTN_HEADS + h)),
            pl.BlockSpec((1, hw), lambda b, h: (0, 0)),
        ],
        out_specs=pl.BlockSpec((seq, hw), lambda b, h: (b, h)),
        scratch_shapes=[
            pltpu.VMEM((2 * blk, 1), jnp.float32),
            pltpu.VMEM((2 * blk, V7X_LANES), jnp.float32),
            pltpu.VMEM((2 * blk, hw), jnp.float32),
            pltpu.VMEM((2 * blk, blk), jnp.bfloat16),
        ],
        compiler_params=pltpu.CompilerParams(
            dimension_semantics=("arbitrary", "arbitrary"), vmem_limit_bytes=vmem),
        name="diff_attn",
    )(small_logits, lamv, proj, proj, proj, sub_gain)


_SSM_FACTORISED_MAX_LOG = 60.0


def _cmul(ar, ai, br, bi):
    return ar * br - ai * bi, ar * bi + ai * br


def _ssm_kernel(flag_ref, u_ref, mixed_ref, bmat_ref, cmat_ref, pinv_ref, pfwd_ref, pstep_ref, d_ref,
                wglu_ref, bglu_ref, og_ref, o_ref, h_ref, y_ref, bu_ref, hs_ref, *, chunk):
    del mixed_ref
    c = pl.program_id(1)

    @pl.when(c == 0)
    def _():
        h_ref[...] = jnp.zeros(h_ref.shape, jnp.float32)

    n = SSM_CL_STATE
    factorised = flag_ref[0] == 1

    def lanes_of(cl):
        return slice(cl * V7X_LANES, (cl + 1) * V7X_LANES)

    def finish():
        y = y_ref[...] + d_ref[...] * u_ref[...].astype(jnp.float32)
        y = jax.nn.gelu(y, approximate=True)
        z = jnp.dot(y.astype(jnp.bfloat16), wglu_ref[...], preferred_element_type=jnp.float32)
        z = z + bglu_ref[...]
        y = y * (1.0 / (1.0 + jnp.exp(-z)))
        o_ref[...] = _rms_rows(y, og_ref[...]).astype(o_ref.dtype)

    @pl.when(factorised)
    def _():
        row = lax.broadcasted_iota(jnp.int32, (chunk, chunk), 0)
        col = lax.broadcasted_iota(jnp.int32, (chunk, chunk), 1)
        tri = (col <= row).astype(jnp.bfloat16)
        for cl in range(SSM_CLUSTERS):
            bu = jnp.dot(u_ref[:, lanes_of(cl)], bmat_ref[cl], preferred_element_type=jnp.float32)
            xr, xi = _cmul(pinv_ref[cl, :, :n], pinv_ref[cl, :, n:], bu[:, :n], bu[:, n:])
            x = jnp.concatenate([xr, xi], axis=1).astype(jnp.bfloat16)
            cs = jnp.dot(tri, x, preferred_element_type=jnp.float32)
            cr, ci = _cmul(pstep_ref[1, cl:cl + 1, :n], pstep_ref[1, cl:cl + 1, n:],
                           h_ref[cl:cl + 1, :n], h_ref[cl:cl + 1, n:])
            hr, hi = _cmul(pfwd_ref[cl, :, :n], pfwd_ref[cl, :, n:], cs[:, :n] + cr, cs[:, n:] + ci)
            h_ref[cl:cl + 1, :n] = hr[chunk - 1:chunk, :]
            h_ref[cl:cl + 1, n:] = hi[chunk - 1:chunk, :]
            h = jnp.concatenate([hr, hi], axis=1).astype(jnp.bfloat16)
            y_ref[:, lanes_of(cl)] = jnp.dot(h, cmat_ref[cl], preferred_element_type=jnp.float32)
        finish()

    @pl.when(jnp.logical_not(factorised))
    def _():
        for cl in range(SSM_CLUSTERS):
            bu_ref[...] = jnp.dot(u_ref[:, lanes_of(cl)], bmat_ref[cl],
                                  preferred_element_type=jnp.float32)
            lr, li = pstep_ref[0, cl:cl + 1, :n], pstep_ref[0, cl:cl + 1, n:]

            def step(t, carry):
                hr, hi = carry
                ar, ai = _cmul(lr, li, hr, hi)
                hr = ar + bu_ref[pl.ds(t, 1), :n]
                hi = ai + bu_ref[pl.ds(t, 1), n:]
                hs_ref[pl.ds(t, 1), :n] = hr
                hs_ref[pl.ds(t, 1), n:] = hi
                return hr, hi

            hr, hi = lax.fori_loop(0, chunk, step, (h_ref[cl:cl + 1, :n], h_ref[cl:cl + 1, n:]))
            h_ref[cl:cl + 1, :n] = hr
            h_ref[cl:cl + 1, n:] = hi
            y_ref[:, lanes_of(cl)] = jnp.dot(hs_ref[...].astype(jnp.bfloat16), cmat_ref[cl],
                                             preferred_element_type=jnp.float32)
        finish()


def _ssm_tables(log_dt, a_re, a_im, b_re, b_im, c_re, c_im, chunk):
    f32 = jnp.float32
    n, c = SSM_STATE, SSM_GROUP
    dt = jnp.exp(log_dt.astype(f32))[:, None]
    ar, ai = a_re.astype(f32), a_im.astype(f32)
    mag = jnp.exp(ar * dt)
    lb_re = mag * jnp.cos(ai * dt)
    lb_im = mag * jnp.sin(ai * dt)
    pr, pi_ = lb_re - 1.0, lb_im
    den = ar * ar + ai * ai
    z_re = ((pr * ar + pi_ * ai) / den)[..., None]
    z_im = ((pi_ * ar - pr * ai) / den)[..., None]
    br, bi = b_re.astype(f32), b_im.astype(f32)
    bb_re = z_re * br - z_im * bi
    bb_im = z_re * bi + z_im * br

    def pair_cols(re, im):
        return jnp.concatenate([re, im], axis=-1)

    log_re = (ar * dt).reshape(SSM_CLUSTERS, 1, SSM_CLUSTER * n)
    log_im = (ai * dt).reshape(SSM_CLUSTERS, 1, SSM_CLUSTER * n)

    def table(k):
        kk = k.astype(f32)[None, :, None]
        pm = jnp.exp(kk * log_re)
        ph = kk * log_im
        return pair_cols(pm * jnp.cos(ph), pm * jnp.sin(ph))

    c0 = chunk // 2
    t = jnp.arange(chunk)
    pfwd = table(t - c0)
    pinv = table(c0 - t)
    pstep = table(jnp.array([1, c0 + 1])).transpose(1, 0, 2)
    factorised = ((c0 + 1) * jnp.max(jnp.abs(log_re)) <= _SSM_FACTORISED_MAX_LOG)

    eye = jnp.eye(SSM_CLUSTER, dtype=f32)
    def bpart(bb):
        b4 = bb.reshape(SSM_CLUSTERS, SSM_CLUSTER, n, c)
        return jnp.einsum('kgnc,gh->kgchn', b4, eye).reshape(
            SSM_CLUSTERS, V7X_LANES, SSM_CLUSTER * n)
    bmat = pair_cols(bpart(bb_re), bpart(bb_im)).astype(jnp.bfloat16)
    def cpart(cc):
        c4 = cc.astype(f32).reshape(SSM_CLUSTERS, SSM_CLUSTER, c, n)
        return jnp.einsum('kgcn,gh->khcgn', c4, eye).reshape(
            SSM_CLUSTERS, V7X_LANES, SSM_CLUSTER * n)
    cmat = pair_cols(cpart(c_re), -cpart(c_im)).transpose(0, 2, 1).astype(jnp.bfloat16)
    return factorised.astype(jnp.int32).reshape(1), bmat, cmat, pinv, pfwd, pstep


def _ssm(proj, mixed, tables, d_skip, w_glu, b_glu, out_gain, *, batch, seq, chunk=SSM_CHUNK):
    factorised, bmat, cmat, pinv, pfwd, pstep = tables
    nc = seq // chunk
    u_col = 3 * ATTN_WIDTH // SSM_WIDTH
    o_col = ATTN_WIDTH // SSM_WIDTH
    kern = functools.partial(_ssm_kernel, chunk=chunk)
    const3 = lambda b, c: (0, 0, 0)
    const2 = lambda b, c: (0, 0)
    vmem = _vmem_limit(2 * (bmat.nbytes + cmat.nbytes + pinv.nbytes + pfwd.nbytes + pstep.nbytes),
                       4 * _nbytes((SSM_WIDTH, SSM_WIDTH), jnp.bfloat16),
                       10 * _nbytes((chunk, SSM_WIDTH), jnp.float32))
    return pl.pallas_call(
        kern,
        out_shape=jax.ShapeDtypeStruct(mixed.shape, mixed.dtype),
        grid=(batch, nc),
        in_specs=[
            pl.BlockSpec(memory_space=pltpu.SMEM),
            pl.BlockSpec((chunk, SSM_WIDTH), lambda b, c: (b * nc + c, u_col)),
            pl.BlockSpec(memory_space=pl.ANY),
            pl.BlockSpec(bmat.shape, const3),
            pl.BlockSpec(cmat.shape, const3),
            pl.BlockSpec(pinv.shape, const3),
            pl.BlockSpec(pfwd.shape, const3),
            pl.BlockSpec(pstep.shape, const3),
            pl.BlockSpec((1, SSM_WIDTH), const2),
            pl.BlockSpec((SSM_WIDTH, SSM_WIDTH), const2),
            pl.BlockSpec((1, SSM_WIDTH), const2),
            pl.BlockSpec((1, SSM_WIDTH), const2),
        ],
        out_specs=pl.BlockSpec((chunk, SSM_WIDTH), lambda b, c: (b * nc + c, o_col)),
        scratch_shapes=[
            pltpu.VMEM((SSM_CLUSTERS, SSM_CL_COLS), jnp.float32),
            pltpu.VMEM((chunk, SSM_WIDTH), jnp.float32),
            pltpu.VMEM((chunk, SSM_CL_COLS), jnp.float32),
            pltpu.VMEM((chunk, SSM_CL_COLS), jnp.float32),
        ],
        input_output_aliases={2: 0},
        compiler_params=pltpu.CompilerParams(
            dimension_semantics=("arbitrary", "arbitrary"), vmem_limit_bytes=vmem),
        name="s5_ssm",
    )(factorised, proj, mixed, bmat, cmat, pinv, pfwd, pstep, d_skip, w_glu, b_glu, out_gain)


def _res_matmul_kernel(a_ref, w_ref, r_ref, o_ref, *maybe_bf16_ref):
    out = r_ref[...] + jnp.dot(a_ref[...], w_ref[...], preferred_element_type=jnp.float32)
    o_ref[...] = out
    for ob_ref in maybe_bf16_ref:
        ob_ref[...] = out.astype(ob_ref.dtype)


def _res_matmul(a, w, res, *, tm, tn, name, with_bf16_copy=False):
    m, k = a.shape
    n = w.shape[1]
    assert m % tm == 0 and n % tn == 0
    vmem = _vmem_limit(2 * _nbytes((tm, k), a.dtype), 2 * _nbytes((k, tn), w.dtype),
                       6 * _nbytes((tm, tn), jnp.float32))
    tile = pl.BlockSpec((tm, tn), lambda i, j: (i, j))
    out_shape = [jax.ShapeDtypeStruct((m, n), jnp.float32)]
    if with_bf16_copy:
        out_shape.append(jax.ShapeDtypeStruct((m, n), jnp.bfloat16))
    outs = pl.pallas_call(
        _res_matmul_kernel,
        out_shape=out_shape,
        grid=(m // tm, n // tn),
        in_specs=[
            pl.BlockSpec((tm, k), lambda i, j: (i, 0)),
            pl.BlockSpec((k, tn), lambda i, j: (0, j)),
            tile,
        ],
        out_specs=[tile] * len(out_shape),
        compiler_params=pltpu.CompilerParams(
            dimension_semantics=("parallel", "parallel"), vmem_limit_bytes=vmem),
        name=name,
    )(a, w, res)
    return outs if with_bf16_copy else outs[0]


def _ffn_up_kernel(x_ref, halo_ref, g_ref, wg_ref, wu_ref, cw_ref, cb_ref, o_ref, h_ref,
                   *, tm, blocks_per_seq):
    i = pl.program_id(0)
    j = pl.program_id(1)
    halo = V7X_BF16_ROWS_PER_TILE

    @pl.when(j == 0)
    def _():
        hh = _rms_rows(halo_ref[...].astype(jnp.float32), g_ref[...])
        hh = jnp.where(i % blocks_per_seq == 0, 0.0, hh)
        h_ref[0:halo, :] = hh.astype(h_ref.dtype)
        _norm_into(x_ref, g_ref, h_ref, rows=tm, row_chunk=16, h_offset=halo)

    g_all = jnp.dot(h_ref[...], wg_ref[...], preferred_element_type=jnp.float32)
    up = jnp.dot(h_ref[halo:, :], wu_ref[...], preferred_element_type=jnp.float32)
    rows = 128
    for r0 in range(0, tm, rows):
        for c0 in range(0, o_ref.shape[1], V7X_LANES):
            cs = slice(c0, c0 + V7X_LANES)
            g = g_all[r0:r0 + halo + rows, cs]
            t = cw_ref[1:2, cs] * g + pltpu.roll(cw_ref[0:1, cs] * g, 1, 0)
            gc = (cb_ref[:, cs] + cw_ref[2:3, cs] * g + pltpu.roll(t, 1, 0))[halo:, :]
            act = gc * (1.0 / (1.0 + jnp.exp(-gc))) * up[r0:r0 + rows, cs]
            o_ref[r0:r0 + rows, cs] = act.astype(o_ref.dtype)


def _ffn_up(x1, gain, wg, wu, conv_w, conv_b, *, seq, tm=1024, tf=F_TILE):
    m, k = x1.shape
    f = wg.shape[1]
    halo = V7X_BF16_ROWS_PER_TILE
    assert m % tm == 0 and seq % tm == 0
    kern = functools.partial(_ffn_up_kernel, tm=tm, blocks_per_seq=seq // tm)
    vmem = _vmem_limit(2 * _nbytes((tm, k), x1.dtype), _nbytes((tm + halo, k), jnp.bfloat16),
                       4 * _nbytes((k, tf), jnp.bfloat16), 2 * _nbytes((tm, tf), jnp.bfloat16),
                       3 * _nbytes((tm + halo, tf), jnp.float32))
    return pl.pallas_call(
        kern,
        out_shape=jax.ShapeDtypeStruct((m, f), jnp.bfloat16),
        grid=(m // tm, pl.cdiv(f, tf)),
        in_specs=[
            pl.BlockSpec((tm, k), lambda i, j: (i, 0)),
            pl.BlockSpec((halo, k), lambda i, j: (jnp.maximum(i * (tm // halo) - 1, 0), 0)),
            pl.BlockSpec((1, k), lambda i, j: (0, 0)),
            pl.BlockSpec((k, tf), lambda i, j: (0, j)),
            pl.BlockSpec((k, tf), lambda i, j: (0, j)),
            pl.BlockSpec((CONV_WIDTH, tf), lambda i, j: (0, j)),
            pl.BlockSpec((1, tf), lambda i, j: (0, j)),
        ],
        out_specs=pl.BlockSpec((tm, tf), lambda i, j: (i, j)),
        scratch_shapes=[pltpu.VMEM((tm + halo, k), jnp.bfloat16)],
        compiler_params=pltpu.CompilerParams(
            dimension_semantics=("arbitrary", "arbitrary"), vmem_limit_bytes=vmem),
        name="ffn_up",
    )(x1, x1, gain, wg, wu, conv_w, conv_b)


def kernel(x, attn_norm, w_in, q_gain, k_gain, lam_q1, lam_k1, lam_q2, lam_k2, sub_gain, ssm_log_dt, ssm_a_re, ssm_a_im, ssm_b_re, ssm_b_im, ssm_c_re, ssm_c_im, ssm_d, ssm_w_glu, ssm_b_glu, ssm_out_gain, w_out, ffn_norm, w_gate, w_up, conv_w, conv_b, w_down):
    b, s, d = x.shape
    depth = w_in.shape[0]
    bf16, f32 = jnp.bfloat16, jnp.float32
    xf = x.reshape(b * s, d)
    for l in range(depth):
        lam_init = 0.8 - 0.6 * math.exp(-0.3 * l)
        q_scale = HEAD_DIM ** -0.5 * math.log2(math.e)
        qg, kg = q_gain[l].astype(f32), k_gain[l].astype(f32)
        col_gain = jnp.concatenate([
            jnp.tile(qg * q_scale, ATTN_WIDTH // HEAD_DIM),
            jnp.tile(kg, ATTN_WIDTH // HEAD_DIM),
            jnp.ones((IN_WIDTH - 2 * ATTN_WIDTH,), f32)])[None, :]
        proj, (wo_bf, wg_bf, wu_bf, wd_bf) = _in_proj(
            xf, attn_norm[l].astype(f32)[None, :], w_in[l].astype(bf16), col_gain,
            (w_out, w_gate, w_up, w_down), l)

        logit_bound = 1.01 * HEAD_DIM * q_scale * jnp.max(jnp.abs(qg)) * jnp.max(jnp.abs(kg))
        small_logits = (logit_bound <= _UNSHIFTED_SOFTMAX_MAX_LOG2).astype(jnp.int32).reshape(1)
        lamv = jnp.stack([lam_q1[l], lam_k1[l], lam_q2[l], lam_k2[l]]).astype(f32)
        mixed = _attention(proj, small_logits, lamv, sub_gain[l].astype(f32)[None, :],
                           batch=b, seq=s, lam_init=lam_init)

        tables = _ssm_tables(ssm_log_dt[l], ssm_a_re[l], ssm_a_im[l], ssm_b_re[l], ssm_b_im[l],
                             ssm_c_re[l], ssm_c_im[l], SSM_CHUNK)
        mixed = _ssm(proj, mixed, tables, ssm_d[l].astype(f32).reshape(1, SSM_WIDTH),
                     ssm_w_glu[l].astype(bf16), ssm_b_glu[l].astype(f32)[None, :],
                     ssm_out_gain[l].astype(f32)[None, :], batch=b, seq=s)

        x1, x1_bf = _res_matmul(mixed, wo_bf, xf, tm=1024, tn=512, name="out_proj",
                                with_bf16_copy=True)

        act = _ffn_up(x1_bf, ffn_norm[l].astype(f32)[None, :], wg_bf, wu_bf, conv_w[l].astype(f32),
                      conv_b[l].astype(f32)[None, :], seq=s)
        xf = _res_matmul(act, wd_bf, x1, tm=512, tn=512, name="ffn_down")
    return xf.reshape(b, s, d)
```

```python
import functools
import math

import jax
import jax.numpy as jnp
from jax import lax
from jax.experimental import pallas as pl
from jax.experimental.pallas import tpu as pltpu

D_MODEL = 4096
ATTN_HEADS = 12
HEAD_DIM = 128
ATTN_WIDTH = ATTN_HEADS * 2 * HEAD_DIM
SSM_WIDTH = D_MODEL - ATTN_WIDTH
SSM_GROUP = 16
SSM_GROUPS = SSM_WIDTH // SSM_GROUP
SSM_STATE = 64
IN_WIDTH = 3 * ATTN_WIDTH + SSM_WIDTH
D_FF = 11008
CONV_WIDTH = 3
EPS = 1e-6

V7X_LANES = 128
V7X_SUBLANES = 8
V7X_BF16_ROWS_PER_TILE = 2 * V7X_SUBLANES
V7X_VMEM_BYTES = 64 * 1024 * 1024
V7X_VMEM_RESERVED_BYTES = 6 * 1024 * 1024

F_TILE = 512
SSM_CLUSTER = V7X_LANES // SSM_GROUP
SSM_CLUSTERS = SSM_GROUPS // SSM_CLUSTER
SSM_CL_STATE = SSM_CLUSTER * SSM_STATE
SSM_CL_COLS = 2 * SSM_CL_STATE
SSM_CHUNK = 256


def _vmem_limit(*nbytes):
    want = int(sum(nbytes)) + 12 * 1024 * 1024
    return min(want, V7X_VMEM_BYTES - V7X_VMEM_RESERVED_BYTES)


def _nbytes(shape, dtype):
    return math.prod(shape) * jnp.dtype(dtype).itemsize


def _rms_rows(x, gain):
    ms = jnp.mean(x * x, axis=-1, keepdims=True)
    return x * lax.rsqrt(ms + EPS) * gain


def _norm_into(x_ref, gain_ref, h_ref, *, rows, row_chunk, h_offset):
    def body(c, carry):
        r = pl.multiple_of(c * row_chunk, row_chunk)
        xb = x_ref[pl.ds(r, row_chunk), :].astype(jnp.float32)
        h_ref[pl.ds(h_offset + r, row_chunk), :] = _rms_rows(xb, gain_ref[...]).astype(h_ref.dtype)
        return carry
    lax.fori_loop(0, rows // row_chunk, body, 0, unroll=4)


def _in_proj_kernel(x_ref, g_ref, w_ref, cg_ref, *refs, tm, tn, n_norm_tiles, n_parts, n_side):
    side_in, o_ref, side_out, h_ref = (refs[:n_side], refs[n_side], refs[n_side + 1:2 * n_side + 1],
                                       refs[2 * n_side + 1])
    j = pl.program_id(1)
    h_ref = h_ref.at[pl.program_id(2)]

    @pl.when(j == 0)
    def _():
        _norm_into(x_ref, g_ref, h_ref, rows=tm, row_chunk=16, h_offset=0)

    is_qk = j < n_norm_tiles
    part = tn // n_parts
    for c0 in range(0, tn, part):
        acc = jnp.dot(h_ref[...], w_ref[:, c0:c0 + part], preferred_element_type=jnp.float32)
        for s0 in range(0, part, HEAD_DIM):
            raw = acc[:, s0:s0 + HEAD_DIM]
            sl = slice(c0 + s0, c0 + s0 + HEAD_DIM)
            o_ref[:, sl] = jnp.where(is_qk, _rms_rows(raw, cg_ref[:, sl]), raw).astype(o_ref.dtype)

    for src, dst in zip(side_in, side_out):
        dst[...] = src[...].astype(dst.dtype)


def _in_proj(x2d, gain, w_bf16, col_gain, side_f32, layer, *, tm=512, tn=1024, row_blocks=2):
    m, k = x2d.shape
    n = w_bf16.shape[1]
    n_norm_tiles = 2 * ATTN_WIDTH // tn
    n_col = n // tn
    n_steps = (m // tm) * n_col
    assert 2 * ATTN_WIDTH % tn == 0 and m % (tm * row_blocks) == 0 and n % tn == 0
    step = lambda i, j, r: (i * n_col + j) * row_blocks + r
    kern = functools.partial(_in_proj_kernel, tm=tm, tn=tn, n_norm_tiles=n_norm_tiles, n_parts=4,
                             n_side=len(side_f32))
    side_in_specs, side_out_specs, side_shapes, side_bytes = [], [], [], 0
    for a in side_f32:
        _, rows, cols = a.shape
        tile = V7X_BF16_ROWS_PER_TILE
        slab = next(s for s in range(tile, rows + 1, tile)
                    if rows % s == 0 and rows // s <= n_steps)
        last = rows // slab - 1
        side_in_specs.append(pl.BlockSpec(
            (None, slab, cols),
            lambda i, j, r, last=last: (layer, jnp.minimum(step(i, j, r), last), 0)))
        side_out_specs.append(pl.BlockSpec(
            (slab, cols), lambda i, j, r, last=last: (jnp.minimum(step(i, j, r), last), 0)))
        side_shapes.append(jax.ShapeDtypeStruct((rows, cols), jnp.bfloat16))
        side_bytes += 2 * _nbytes((slab, cols), jnp.float32) + 2 * _nbytes((slab, cols), jnp.bfloat16)
    vmem = _vmem_limit(2 * _nbytes((tm, k), jnp.float32),
                       row_blocks * _nbytes((tm, k), jnp.bfloat16),
                       2 * _nbytes((k, tn), jnp.bfloat16), 2 * _nbytes((tm, tn), jnp.bfloat16),
                       _nbytes((tm, tn), jnp.float32), side_bytes)
    outs = pl.pallas_call(
        kern,
        out_shape=[jax.ShapeDtypeStruct((m, n), jnp.bfloat16)] + side_shapes,
        grid=(m // (tm * row_blocks), n_col, row_blocks),
        in_specs=[
            pl.BlockSpec((tm, k), lambda i, j, r: (
                i * row_blocks + jnp.where(j == 0, r, row_blocks - 1), 0)),
            pl.BlockSpec((1, k), lambda i, j, r: (0, 0)),
            pl.BlockSpec((k, tn), lambda i, j, r: (0, j)),
            pl.BlockSpec((1, tn), lambda i, j, r: (0, j)),
        ] + side_in_specs,
        out_specs=[pl.BlockSpec((tm, tn), lambda i, j, r: (i * row_blocks + r, j))] + side_out_specs,
        scratch_shapes=[pltpu.VMEM((row_blocks, tm, k), jnp.bfloat16)],
        compiler_params=pltpu.CompilerParams(
            dimension_semantics=("arbitrary", "arbitrary", "arbitrary"), vmem_limit_bytes=vmem),
        name="in_proj",
    )(x2d, gain, w_bf16, col_gain, *side_f32)
    return outs[0], outs[1:]


_UNSHIFTED_SOFTMAX_MAX_LOG2 = 60.0


def _qk_scores(q, k, row_offset):
    s = lax.dot_general(q, k, (((1,), (1,)), ((), ())), preferred_element_type=jnp.float32)
    if row_offset is not None:
        row = lax.broadcasted_iota(jnp.int32, s.shape, 0)
        col = lax.broadcasted_iota(jnp.int32, s.shape, 1)
        s = jnp.where(col <= row + row_offset, s, -jnp.inf)
    return s


def _lane_partial_sums(p):
    ps = p[:, 0:V7X_LANES]
    for c in range(1, p.shape[1] // V7X_LANES):
        ps = ps + p[:, c * V7X_LANES:(c + 1) * V7X_LANES]
    return ps


def _attn_kernel(small_ref, lamv_ref, q_ref, k_ref, v_ref, sg_ref, o_ref,
                 m_ref, l_ref, acc_ref, p_ref, *, blk, nq, lam_init):
    unshifted = small_ref[0] == 1
    lamv = lamv_ref[...]
    lam = (jnp.exp(jnp.sum(lamv[0:1] * lamv[1:2], axis=-1, keepdims=True))
           - jnp.exp(jnp.sum(lamv[2:3] * lamv[3:4], axis=-1, keepdims=True)) + lam_init)
    out_gain = sg_ref[...] * (1.0 - lam_init)

    def head(mi):
        return slice(mi * HEAD_DIM, (mi + 1) * HEAD_DIM)

    half = blk // 2

    def stacked(hf, mi):
        return slice((2 * hf + mi) * half, (2 * hf + mi + 1) * half)

    def q_block(qi, carry):
        q0 = pl.multiple_of(qi * blk, blk)

        def tile(hfs, start, width, mask_offset, shifted, first):
            ks = pl.ds(start, width)
            n_rows = len(hfs) * half
            qs = pl.ds(q0 + hfs[0] * half, n_rows)
            for mi in range(2):
                s = _qk_scores(q_ref[qs, head(mi)], k_ref[ks, head(mi)], mask_offset)
                if not shifted:
                    p = jnp.exp2(s)
                    ps = _lane_partial_sums(p)
                for n, hf in enumerate(hfs):
                    st = stacked(hf, mi)
                    local = slice(n * half, (n + 1) * half)
                    if shifted:
                        m_cur = jnp.max(s[local], axis=-1, keepdims=True)
                        m_new = m_cur if first else jnp.maximum(m_ref[st, :], m_cur)
                        ph = jnp.exp2(s[local] - m_new)
                        l_cur = jnp.sum(ph, axis=-1, keepdims=True)
                        if first:
                            l_ref[st, :] = jnp.zeros((half, V7X_LANES), jnp.float32)
                            l_ref[st, 0:1] = l_cur
                        else:
                            alpha = jnp.exp2(m_ref[st, :] - m_new)
                            l_ref[st, 0:1] = alpha * l_ref[st, 0:1] + l_cur
                            acc_ref[st, :] = alpha * acc_ref[st, :]
                        m_ref[st, :] = m_new
                        p_ref[st, 0:width] = ph.astype(p_ref.dtype)
                    else:
                        if first:
                            l_ref[st, :] = ps[local]
                        else:
                            l_ref[st, :] += ps[local]
                        p_ref[st, 0:width] = p[local].astype(p_ref.dtype)
            lo, hi = 2 * hfs[0] * half, 2 * (hfs[-1] + 1) * half
            for rows in (slice(lo, (lo + hi) // 2), slice((lo + hi) // 2, hi)):
                pv = jnp.dot(p_ref[rows, 0:width], v_ref[ks, :], preferred_element_type=jnp.float32)
                if first:
                    acc_ref[rows, :] = pv
                else:
                    acc_ref[rows, :] += pv

        def causal_tiles(shifted):
            tile((0,), q0, half, 0, shifted, True)
            tile((1,), q0, blk, half, shifted, True)

            def body(j, c):
                tile((0, 1), pl.multiple_of(j * blk, blk), blk, None, shifted, False)
                return c
            lax.fori_loop(0, qi, body, 0)

        @pl.when(unshifted)
        def _():
            causal_tiles(False)

        @pl.when(jnp.logical_not(unshifted))
        def _():
            causal_tiles(True)

        for hf in range(2):
            r1 = 1.0 / jnp.sum(l_ref[stacked(hf, 0), :], axis=-1, keepdims=True)
            r2 = lam / jnp.sum(l_ref[stacked(hf, 1), :], axis=-1, keepdims=True)
            o = acc_ref[stacked(hf, 0), :] * r1 - acc_ref[stacked(hf, 1), :] * r2
            o_ref[pl.ds(q0 + hf * half, half), :] = _rms_rows(o, out_gain).astype(o_ref.dtype)
        return carry

    lax.fori_loop(0, nq, q_block, 0)


def _attention(proj, small_logits, lamv, sub_gain, *, batch, seq, lam_init, blk=1024):
    nq = seq // blk
    hw = 2 * HEAD_DIM
    kern = functools.partial(_attn_kernel, blk=blk, nq=nq, lam_init=lam_init)
    vmem = _vmem_limit(8 * _nbytes((seq, hw), jnp.bfloat16),
                       _nbytes((2 * blk, hw), jnp.float32), _nbytes((2 * blk, blk), jnp.bfloat16),
                       4 * _nbytes((blk, V7X_LANES), jnp.float32),
                       4 * _nbytes((blk, blk), jnp.float32))
    return pl.pallas_call(
        kern,
        out_shape=jax.ShapeDtypeStruct((batch * seq, D_MODEL), jnp.bfloat16),
        grid=(batch, ATTN_HEADS),
        in_specs=[
            pl.BlockSpec(memory_space=pltpu.SMEM),
            pl.BlockSpec((4, HEAD_DIM), lambda b, h: (0, 0)),
            pl.BlockSpec((seq, hw), lambda b, h: (b, h)),
            pl.BlockSpec((seq, hw), lambda b, h: (b, ATTN_HEADS + h)),
            pl.BlockSpec((seq, hw), lambda b, h: (b, 2 * ATTN_HEADS + h)),
            pl.BlockSpec((1, hw), lambda b, h: (0, 0)),
        ],
        out_specs=pl.BlockSpec((seq, hw), lambda b, h: (b, h)),
        scratch_shapes=[
            pltpu.VMEM((2 * blk, 1), jnp.float32),
            pltpu.VMEM((2 * blk, V7X_LANES), jnp.float32),
            pltpu.VMEM((2 * blk, hw), jnp.float32),
            pltpu.VMEM((2 * blk, blk), jnp.bfloat16),
        ],
        compiler_params=pltpu.CompilerParams(
            dimension_semantics=("arbitrary", "arbitrary"), vmem_limit_bytes=vmem),
        name="diff_attn",
    )(small_logits, lamv, proj, proj, proj, sub_gain)


_SSM_FACTORISED_MAX_LOG = 60.0


def _cmul(ar, ai, br, bi):
    return ar * br - ai * bi, ar * bi + ai * br


def _ssm_kernel(flag_ref, u_ref, mixed_ref, bmat_ref, cmat_ref, pinv_ref, pfwd_ref, pstep_ref, d_ref,
                wglu_ref, bglu_ref, og_ref, o_ref, h_ref, y_ref, bu_ref, hs_ref, *, chunk):
    del mixed_ref
    c = pl.program_id(1)

    @pl.when(c == 0)
    def _():
        h_ref[...] = jnp.zeros(h_ref.shape, jnp.float32)

    n = SSM_CL_STATE
    factorised = flag_ref[0] == 1

    def lanes_of(cl):
        return slice(cl * V7X_LANES, (cl + 1) * V7X_LANES)

    def finish():
        y = y_ref[...] + d_ref[...] * u_ref[...].astype(jnp.float32)
        y = jax.nn.gelu(y, approximate=True)
        z = jnp.dot(y.astype(jnp.bfloat16), wglu_ref[...], preferred_element_type=jnp.float32)
        z = z + bglu_ref[...]
        y = y * (1.0 / (1.0 + jnp.exp(-z)))
        o_ref[...] = _rms_rows(y, og_ref[...]).astype(o_ref.dtype)

    @pl.when(factorised)
    def _():
        row = lax.broadcasted_iota(jnp.int32, (chunk, chunk), 0)
        col = lax.broadcasted_iota(jnp.int32, (chunk, chunk), 1)
        tri = (col <= row).astype(jnp.bfloat16)
        for cl in range(SSM_CLUSTERS):
            bu = jnp.dot(u_ref[:, lanes_of(cl)], bmat_ref[cl], preferred_element_type=jnp.float32)
            xr, xi = _cmul(pinv_ref[cl, :, :n], pinv_ref[cl, :, n:], bu[:, :n], bu[:, n:])
            x = jnp.concatenate([xr, xi], axis=1).astype(jnp.bfloat16)
            cs = jnp.dot(tri, x, preferred_element_type=jnp.float32)
            cr, ci = _cmul(pstep_ref[1, cl:cl + 1, :n], pstep_ref[1, cl:cl + 1, n:],
                           h_ref[cl:cl + 1, :n], h_ref[cl:cl + 1, n:])
            hr, hi = _cmul(pfwd_ref[cl, :, :n], pfwd_ref[cl, :, n:], cs[:, :n] + cr, cs[:, n:] + ci)
            h_ref[cl:cl + 1, :n] = hr[chunk - 1:chunk, :]
            h_ref[cl:cl + 1, n:] = hi[chunk - 1:chunk, :]
            h = jnp.concatenate([hr, hi], axis=1).astype(jnp.bfloat16)
            y_ref[:, lanes_of(cl)] = jnp.dot(h, cmat_ref[cl], preferred_element_type=jnp.float32)
        finish()

    @pl.when(jnp.logical_not(factorised))
    def _():
        for cl in range(SSM_CLUSTERS):
            bu_ref[...] = jnp.dot(u_ref[:, lanes_of(cl)], bmat_ref[cl],
                                  preferred_element_type=jnp.float32)
            lr, li = pstep_ref[0, cl:cl + 1, :n], pstep_ref[0, cl:cl + 1, n:]

            def step(t, carry):
                hr, hi = carry
                ar, ai = _cmul(lr, li, hr, hi)
                hr = ar + bu_ref[pl.ds(t, 1), :n]
                hi = ai + bu_ref[pl.ds(t, 1), n:]
                hs_ref[pl.ds(t, 1), :n] = hr
                hs_ref[pl.ds(t, 1), n:] = hi
                return hr, hi

            hr, hi = lax.fori_loop(0, chunk, step, (h_ref[cl:cl + 1, :n], h_ref[cl:cl + 1, n:]))
            h_ref[cl:cl + 1, :n] = hr
            h_ref[cl:cl + 1, n:] = hi
            y_ref[:, lanes_of(cl)] = jnp.dot(hs_ref[...].astype(jnp.bfloat16), cmat_ref[cl],
                                             preferred_element_type=jnp.float32)
        finish()


def _ssm_tables(log_dt, a_re, a_im, b_re, b_im, c_re, c_im, chunk):
    f32 = jnp.float32
    n, c = SSM_STATE, SSM_GROUP
    dt = jnp.exp(log_dt.astype(f32))[:, None]
    ar, ai = a_re.astype(f32), a_im.astype(f32)
    mag = jnp.exp(ar * dt)
    lb_re = mag * jnp.cos(ai * dt)
    lb_im = mag * jnp.sin(ai * dt)
    pr, pi_ = lb_re - 1.0, lb_im
    den = ar * ar + ai * ai
    z_re = ((pr * ar + pi_ * ai) / den)[..., None]
    z_im = ((pi_ * ar - pr * ai) / den)[..., None]
    br, bi = b_re.astype(f32), b_im.astype(f32)
    bb_re = z_re * br - z_im * bi
    bb_im = z_re * bi + z_im * br

    def pair_cols(re, im):
        return jnp.concatenate([re, im], axis=-1)

    log_re = (ar * dt).reshape(SSM_CLUSTERS, 1, SSM_CLUSTER * n)
    log_im = (ai * dt).reshape(SSM_CLUSTERS, 1, SSM_CLUSTER * n)

    def table(k):
        kk = k.astype(f32)[None, :, None]
        pm = jnp.exp(kk * log_re)
        ph = kk * log_im
        return pair_cols(pm * jnp.cos(ph), pm * jnp.sin(ph))

    c0 = chunk // 2
    t = jnp.arange(chunk)
    pfwd = table(t - c0)
    pinv = table(c0 - t)
    pstep = table(jnp.array([1, c0 + 1])).transpose(1, 0, 2)
    factorised = ((c0 + 1) * jnp.max(jnp.abs(log_re)) <= _SSM_FACTORISED_MAX_LOG)

    eye = jnp.eye(SSM_CLUSTER, dtype=f32)
    def bpart(bb):
        b4 = bb.reshape(SSM_CLUSTERS, SSM_CLUSTER, n, c)
        return jnp.einsum('kgnc,gh->kgchn', b4, eye).reshape(
            SSM_CLUSTERS, V7X_LANES, SSM_CLUSTER * n)
    bmat = pair_cols(bpart(bb_re), bpart(bb_im)).astype(jnp.bfloat16)
    def cpart(cc):
        c4 = cc.astype(f32).reshape(SSM_CLUSTERS, SSM_CLUSTER, c, n)
        return jnp.einsum('kgcn,gh->khcgn', c4, eye).reshape(
            SSM_CLUSTERS, V7X_LANES, SSM_CLUSTER * n)
    cmat = pair_cols(cpart(c_re), -cpart(c_im)).transpose(0, 2, 1).astype(jnp.bfloat16)
    return factorised.astype(jnp.int32).reshape(1), bmat, cmat, pinv, pfwd, pstep


def _ssm(proj, mixed, tables, d_skip, w_glu, b_glu, out_gain, *, batch, seq, chunk=SSM_CHUNK):
    factorised, bmat, cmat, pinv, pfwd, pstep = tables
    nc = seq // chunk
    u_col = 3 * ATTN_WIDTH // SSM_WIDTH
    o_col = ATTN_WIDTH // SSM_WIDTH
    kern = functools.partial(_ssm_kernel, chunk=chunk)
    const3 = lambda b, c: (0, 0, 0)
    const2 = lambda b, c: (0, 0)
    vmem = _vmem_limit(2 * (bmat.nbytes + cmat.nbytes + pinv.nbytes + pfwd.nbytes + pstep.nbytes),
                       4 * _nbytes((SSM_WIDTH, SSM_WIDTH), jnp.bfloat16),
                       10 * _nbytes((chunk, SSM_WIDTH), jnp.float32))
    return pl.pallas_call(
        kern,
        out_shape=jax.ShapeDtypeStruct(mixed.shape, mixed.dtype),
        grid=(batch, nc),
        in_specs=[
            pl.BlockSpec(memory_space=pltpu.SMEM),
            pl.BlockSpec((chunk, SSM_WIDTH), lambda b, c: (b * nc + c, u_col)),
            pl.BlockSpec(memory_space=pl.ANY),
            pl.BlockSpec(bmat.shape, const3),
            pl.BlockSpec(cmat.shape, const3),
            pl.BlockSpec(pinv.shape, const3),
            pl.BlockSpec(pfwd.shape, const3),
            pl.BlockSpec(pstep.shape, const3),
            pl.BlockSpec((1, SSM_WIDTH), const2),
            pl.BlockSpec((SSM_WIDTH, SSM_WIDTH), const2),
            pl.BlockSpec((1, SSM_WIDTH), const2),
            pl.BlockSpec((1, SSM_WIDTH), const2),
        ],
        out_specs=pl.BlockSpec((chunk, SSM_WIDTH), lambda b, c: (b * nc + c, o_col)),
        scratch_shapes=[
            pltpu.VMEM((SSM_CLUSTERS, SSM_CL_COLS), jnp.float32),
            pltpu.VMEM((chunk, SSM_WIDTH), jnp.float32),
            pltpu.VMEM((chunk, SSM_CL_COLS), jnp.float32),
            pltpu.VMEM((chunk, SSM_CL_COLS), jnp.float32),
        ],
        input_output_aliases={2: 0},
        compiler_params=pltpu.CompilerParams(
            dimension_semantics=("arbitrary", "arbitrary"), vmem_limit_bytes=vmem),
        name="s5_ssm",
    )(factorised, proj, mixed, bmat, cmat, pinv, pfwd, pstep, d_skip, w_glu, b_glu, out_gain)


def _res_matmul_kernel(a_ref, w_ref, r_ref, o_ref, *maybe_bf16_ref):
    out = r_ref[...] + jnp.dot(a_ref[...], w_ref[...], preferred_element_type=jnp.float32)
    o_ref[...] = out
    for ob_ref in maybe_bf16_ref:
        ob_ref[...] = out.astype(ob_ref.dtype)


def _res_matmul(a, w, res, *, tm, tn, name, with_bf16_copy=False):
    m, k = a.shape
    n = w.shape[1]
    assert m % tm == 0 and n % tn == 0
    vmem = _vmem_limit(2 * _nbytes((tm, k), a.dtype), 2 * _nbytes((k, tn), w.dtype),
                       6 * _nbytes((tm, tn), jnp.float32))
    tile = pl.BlockSpec((tm, tn), lambda i, j: (i, j))
    out_shape = [jax.ShapeDtypeStruct((m, n), jnp.float32)]
    if with_bf16_copy:
        out_shape.append(jax.ShapeDtypeStruct((m, n), jnp.bfloat16))
    outs = pl.pallas_call(
        _res_matmul_kernel,
        out_shape=out_shape,
        grid=(m // tm, n // tn),
        in_specs=[
            pl.BlockSpec((tm, k), lambda i, j: (i, 0)),
            pl.BlockSpec((k, tn), lambda i, j: (0, j)),
            tile,
        ],
        out_specs=[tile] * len(out_shape),
        compiler_params=pltpu.CompilerParams(
            dimension_semantics=("parallel", "parallel"), vmem_limit_bytes=vmem),
        name=name,
    )(a, w, res)
    return outs if with_bf16_copy else outs[0]


def _ffn_up_kernel(x_ref, halo_ref, g_ref, wg_ref, wu_ref, cw_ref, cb_ref, o_ref, h_ref,
                   *, tm, blocks_per_seq):
    i = pl.program_id(0)
    j = pl.program_id(1)
    halo = V7X_BF16_ROWS_PER_TILE

    @pl.when(j == 0)
    def _():
        hh = _rms_rows(halo_ref[...].astype(jnp.float32), g_ref[...])
        hh = jnp.where(i % blocks_per_seq == 0, 0.0, hh)
        h_ref[0:halo, :] = hh.astype(h_ref.dtype)
        _norm_into(x_ref, g_ref, h_ref, rows=tm, row_chunk=16, h_offset=halo)

    g_all = jnp.dot(h_ref[...], wg_ref[...], preferred_element_type=jnp.float32)
    up = jnp.dot(h_ref[halo:, :], wu_ref[...], preferred_element_type=jnp.float32)
    rows = 128
    for r0 in range(0, tm, rows):
        for c0 in range(0, o_ref.shape[1], V7X_LANES):
            cs = slice(c0, c0 + V7X_LANES)
            g = g_all[r0:r0 + halo + rows, cs]
            t = cw_ref[1:2, cs] * g + pltpu.roll(cw_ref[0:1, cs] * g, 1, 0)
            gc = (cb_ref[:, cs] + cw_ref[2:3, cs] * g + pltpu.roll(t, 1, 0))[halo:, :]
            act = gc * (1.0 / (1.0 + jnp.exp(-gc))) * up[r0:r0 + rows, cs]
            o_ref[r0:r0 + rows, cs] = act.astype(o_ref.dtype)


def _ffn_up(x1, gain, wg, wu, conv_w, conv_b, *, seq, tm=1024, tf=F_TILE):
    m, k = x1.shape
    f = wg.shape[1]
    halo = V7X_BF16_ROWS_PER_TILE
    assert m % tm == 0 and seq % tm == 0
    kern = functools.partial(_ffn_up_kernel, tm=tm, blocks_per_seq=seq // tm)
    vmem = _vmem_limit(2 * _nbytes((tm, k), x1.dtype), _nbytes((tm + halo, k), jnp.bfloat16),
                       4 * _nbytes((k, tf), jnp.bfloat16), 2 * _nbytes((tm, tf), jnp.bfloat16),
                       3 * _nbytes((tm + halo, tf), jnp.float32))
    return pl.pallas_call(
        kern,
        out_shape=jax.ShapeDtypeStruct((m, f), jnp.bfloat16),
        grid=(m // tm, pl.cdiv(f, tf)),
        in_specs=[
            pl.BlockSpec((tm, k), lambda i, j: (i, 0)),
            pl.BlockSpec((halo, k), lambda i, j: (jnp.maximum(i * (tm // halo) - 1, 0), 0)),
            pl.BlockSpec((1, k), lambda i, j: (0, 0)),
            pl.BlockSpec((k, tf), lambda i, j: (0, j)),
            pl.BlockSpec((k, tf), lambda i, j: (0, j)),
            pl.BlockSpec((CONV_WIDTH, tf), lambda i, j: (0, j)),
            pl.BlockSpec((1, tf), lambda i, j: (0, j)),
        ],
        out_specs=pl.BlockSpec((tm, tf), lambda i, j: (i, j)),
        scratch_shapes=[pltpu.VMEM((tm + halo, k), jnp.bfloat16)],
        compiler_params=pltpu.CompilerParams(
            dimension_semantics=("arbitrary", "arbitrary"), vmem_limit_bytes=vmem),
        name="ffn_up",
    )(x1, x1, gain, wg, wu, conv_w, conv_b)


def kernel(x, attn_norm, w_in, q_gain, k_gain, lam_q1, lam_k1, lam_q2, lam_k2, sub_gain, ssm_log_dt, ssm_a_re, ssm_a_im, ssm_b_re, ssm_b_im, ssm_c_re, ssm_c_im, ssm_d, ssm_w_glu, ssm_b_glu, ssm_out_gain, w_out, ffn_norm, w_gate, w_up, conv_w, conv_b, w_down):
    b, s, d = x.shape
    depth = w_in.shape[0]
    bf16, f32 = jnp.bfloat16, jnp.float32
    xf = x.reshape(b * s, d)
    for l in range(depth):
        lam_init = 0.8 - 0.6 * math.exp(-0.3 * l)
        q_scale = HEAD_DIM ** -0.5 * math.log2(math.e)
        qg, kg = q_gain[l].astype(f32), k_gain[l].astype(f32)
        col_gain = jnp.concatenate([
            jnp.tile(qg * q_scale, ATTN_WIDTH // HEAD_DIM),
            jnp.tile(kg, ATTN_WIDTH // HEAD_DIM),
            jnp.ones((IN_WIDTH - 2 * ATTN_WIDTH,), f32)])[None, :]
        proj, (wo_bf, wg_bf, wu_bf, wd_bf) = _in_proj(
            xf, attn_norm[l].astype(f32)[None, :], w_in[l].astype(bf16), col_gain,
            (w_out, w_gate, w_up, w_down), l)

        logit_bound = 1.01 * HEAD_DIM * q_scale * jnp.max(jnp.abs(qg)) * jnp.max(jnp.abs(kg))
        small_logits = (logit_bound <= _UNSHIFTED_SOFTMAX_MAX_LOG2).astype(jnp.int32).reshape(1)
        lamv = jnp.stack([lam_q1[l], lam_k1[l], lam_q2[l], lam_k2[l]]).astype(f32)
        mixed = _attention(proj, small_logits, lamv, sub_gain[l].astype(f32)[None, :],
                           batch=b, seq=s, lam_init=lam_init)

        tables = _ssm_tables(ssm_log_dt[l], ssm_a_re[l], ssm_a_im[l], ssm_b_re[l], ssm_b_im[l],
                             ssm_c_re[l], ssm_c_im[l], SSM_CHUNK)
        mixed = _ssm(proj, mixed, tables, ssm_d[l].astype(f32).reshape(1, SSM_WIDTH),
                     ssm_w_glu[l].astype(bf16), ssm_b_glu[l].astype(f32)[None, :],
                     ssm_out_gain[l].astype(f32)[None, :], batch=b, seq=s)

        x1, x1_bf = _res_matmul(mixed, wo_bf, xf, tm=1024, tn=512, name="out_proj",
                                with_bf16_copy=True)

        act = _ffn_up(x1_bf, ffn_norm[l].astype(f32)[None, :], wg_bf, wu_bf, conv_w[l].astype(f32),
                      conv_b[l].astype(f32)[None, :], seq=s)
        xf = _res_matmul(act, wd_bf, x1, tm=512, tn=512, name="ffn_down")
    return xf.reshape(b, s, d)
```

```python
import functools
import math

import jax
import jax.numpy as jnp
from jax import lax
from jax.experimental import pallas as pl
from jax.experimental.pallas import tpu as pltpu

D_MODEL = 4096
ATTN_HEADS = 12
HEAD_DIM = 128
ATTN_WIDTH = ATTN_HEADS * 2 * HEAD_DIM
SSM_WIDTH = D_MODEL - ATTN_WIDTH
SSM_GROUP = 16
SSM_GROUPS = SSM_WIDTH // SSM_GROUP
SSM_STATE = 64
IN_WIDTH = 3 * ATTN_WIDTH + SSM_WIDTH
D_FF = 11008
CONV_WIDTH = 3
EPS = 1e-6

V7X_LANES = 128
V7X_SUBLANES = 8
V7X_BF16_ROWS_PER_TILE = 2 * V7X_SUBLANES
V7X_VMEM_BYTES = 64 * 1024 * 1024
V7X_VMEM_RESERVED_BYTES = 6 * 1024 * 1024

F_TILE = 512
SSM_CLUSTER = V7X_LANES // SSM_GROUP
SSM_CLUSTERS = SSM_GROUPS // SSM_CLUSTER
SSM_CL_STATE = SSM_CLUSTER * SSM_STATE
SSM_CL_COLS = 2 * SSM_CL_STATE
SSM_CHUNK = 256


def _vmem_limit(*nbytes):
    want = int(sum(nbytes)) + 12 * 1024 * 1024
    return min(want, V7X_VMEM_BYTES - V7X_VMEM_RESERVED_BYTES)


def _nbytes(shape, dtype):
    return math.prod(shape) * jnp.dtype(dtype).itemsize


def _rms_rows(x, gain):
    ms = jnp.mean(x * x, axis=-1, keepdims=True)
    return x * lax.rsqrt(ms + EPS) * gain


def _norm_into(x_ref, gain_ref, h_ref, *, rows, row_chunk, h_offset):
    def body(c, carry):
        r = pl.multiple_of(c * row_chunk, row_chunk)
        xb = x_ref[pl.ds(r, row_chunk), :].astype(jnp.float32)
        h_ref[pl.ds(h_offset + r, row_chunk), :] = _rms_rows(xb, gain_ref[...]).astype(h_ref.dtype)
        return carry
    lax.fori_loop(0, rows // row_chunk, body, 0, unroll=4)


def _in_proj_kernel(x_ref, g_ref, w_ref, cg_ref, *refs, tm, tn, n_norm_tiles, n_parts, n_side):
    side_in, o_ref, side_out, h_ref = (refs[:n_side], refs[n_side], refs[n_side + 1:2 * n_side + 1],
                                       refs[2 * n_side + 1])
    j = pl.program_id(1)
    h_ref = h_ref.at[pl.program_id(2)]

    @pl.when(j == 0)
    def _():
        _norm_into(x_ref, g_ref, h_ref, rows=tm, row_chunk=16, h_offset=0)

    is_qk = j < n_norm_tiles
    part = tn // n_parts
    for c0 in range(0, tn, part):
        acc = jnp.dot(h_ref[...], w_ref[:, c0:c0 + part], preferred_element_type=jnp.float32)
        for s0 in range(0, part, HEAD_DIM):
            raw = acc[:, s0:s0 + HEAD_DIM]
            sl = slice(c0 + s0, c0 + s0 + HEAD_DIM)
            o_ref[:, sl] = jnp.where(is_qk, _rms_rows(raw, cg_ref[:, sl]), raw).astype(o_ref.dtype)

    for src, dst in zip(side_in, side_out):
        dst[...] = src[...].astype(dst.dtype)


def _in_proj(x2d, gain, w_bf16, col_gain, side_f32, layer, *, tm=512, tn=1024, row_blocks=2):
    m, k = x2d.shape
    n = w_bf16.shape[1]
    n_norm_tiles = 2 * ATTN_WIDTH // tn
    n_col = n // tn
    n_steps = (m // tm) * n_col
    assert 2 * ATTN_WIDTH % tn == 0 and m % (tm * row_blocks) == 0 and n % tn == 0
    step = lambda i, j, r: (i * n_col + j) * row_blocks + r
    kern = functools.partial(_in_proj_kernel, tm=tm, tn=tn, n_norm_tiles=n_norm_tiles, n_parts=4,
                             n_side=len(side_f32))
    side_in_specs, side_out_specs, side_shapes, side_bytes = [], [], [], 0
    for a in side_f32:
        _, rows, cols = a.shape
        tile = V7X_BF16_ROWS_PER_TILE
        slab = next(s for s in range(tile, rows + 1, tile)
                    if rows % s == 0 and rows // s <= n_steps)
        last = rows // slab - 1
        side_in_specs.append(pl.BlockSpec(
            (None, slab, cols),
            lambda i, j, r, last=last: (layer, jnp.minimum(step(i, j, r), last), 0)))
        side_out_specs.append(pl.BlockSpec(
            (slab, cols), lambda i, j, r, last=last: (jnp.minimum(step(i, j, r), last), 0)))
        side_shapes.append(jax.ShapeDtypeStruct((rows, cols), jnp.bfloat16))
        side_bytes += 2 * _nbytes((slab, cols), jnp.float32) + 2 * _nbytes((slab, cols), jnp.bfloat16)
    vmem = _vmem_limit(2 * _nbytes((tm, k), jnp.float32),
                       row_blocks * _nbytes((tm, k), jnp.bfloat16),
                       2 * _nbytes((k, tn), jnp.bfloat16), 2 * _nbytes((tm, tn), jnp.bfloat16),
                       _nbytes((tm, tn), jnp.float32), side_bytes)
    outs = pl.pallas_call(
        kern,
        out_shape=[jax.ShapeDtypeStruct((m, n), jnp.bfloat16)] + side_shapes,
        grid=(m // (tm * row_blocks), n_col, row_blocks),
        in_specs=[
            pl.BlockSpec((tm, k), lambda i, j, r: (
                i * row_blocks + jnp.where(j == 0, r, row_blocks - 1), 0)),
            pl.BlockSpec((1, k), lambda i, j, r: (0, 0)),
            pl.BlockSpec((k, tn), lambda i, j, r: (0, j)),
            pl.BlockSpec((1, tn), lambda i, j, r: (0, j)),
        ] + side_in_specs,
        out_specs=[pl.BlockSpec((tm, tn), lambda i, j, r: (i * row_blocks + r, j))] + side_out_specs,
        scratch_shapes=[pltpu.VMEM((row_blocks, tm, k), jnp.bfloat16)],
        compiler_params=pltpu.CompilerParams(
            dimension_semantics=("arbitrary", "arbitrary", "arbitrary"), vmem_limit_bytes=vmem),
        name="in_proj",
    )(x2d, gain, w_bf16, col_gain, *side_f32)
    return outs[0], outs[1:]


_UNSHIFTED_SOFTMAX_MAX_LOG2 = 60.0


def _qk_scores(q, k, row_offset):
    s = lax.dot_general(q, k, (((1,), (1,)), ((), ())), preferred_element_type=jnp.float32)
    if row_offset is not None:
        row = lax.broadcasted_iota(jnp.int32, s.shape, 0)
        col = lax.broadcasted_iota(jnp.int32, s.shape, 1)
        s = jnp.where(col <= row + row_offset, s, -jnp.inf)
    return s


def _lane_partial_sums(p):
    ps = p[:, 0:V7X_LANES]
    for c in range(1, p.shape[1] // V7X_LANES):
        ps = ps + p[:, c * V7X_LANES:(c + 1) * V7X_LANES]
    return ps


def _attn_kernel(small_ref, lamv_ref, q_ref, k_ref, v_ref, sg_ref, o_ref,
                 m_ref, l_ref, acc_ref, p_ref, *, blk, nq, lam_init):
    unshifted = small_ref[0] == 1
    lamv = lamv_ref[...]
    lam = (jnp.exp(jnp.sum(lamv[0:1] * lamv[1:2], axis=-1, keepdims=True))
           - jnp.exp(jnp.sum(lamv[2:3] * lamv[3:4], axis=-1, keepdims=True)) + lam_init)
    out_gain = sg_ref[...] * (1.0 - lam_init)

    def head(mi):
        return slice(mi * HEAD_DIM, (mi + 1) * HEAD_DIM)

    half = blk // 2

    def stacked(hf, mi):
        return slice((2 * hf + mi) * half, (2 * hf + mi + 1) * half)

    def q_block(qi, carry):
        q0 = pl.multiple_of(qi * blk, blk)

        def tile(hfs, start, width, mask_offset, shifted, first):
            ks = pl.ds(start, width)
            n_rows = len(hfs) * half
            qs = pl.ds(q0 + hfs[0] * half, n_rows)
            for mi in range(2):
                s = _qk_scores(q_ref[qs, head(mi)], k_ref[ks, head(mi)], mask_offset)
                if not shifted:
                    p = jnp.exp2(s)
                    ps = _lane_partial_sums(p)
                for n, hf in enumerate(hfs):
                    st = stacked(hf, mi)
                    local = slice(n * half, (n + 1) * half)
                    if shifted:
                        m_cur = jnp.max(s[local], axis=-1, keepdims=True)
                        m_new = m_cur if first else jnp.maximum(m_ref[st, :], m_cur)
                        ph = jnp.exp2(s[local] - m_new)
                        l_cur = jnp.sum(ph, axis=-1, keepdims=True)
                        if first:
                            l_ref[st, :] = jnp.zeros((half, V7X_LANES), jnp.float32)
                            l_ref[st, 0:1] = l_cur
                        else:
                            alpha = jnp.exp2(m_ref[st, :] - m_new)
                            l_ref[st, 0:1] = alpha * l_ref[st, 0:1] + l_cur
                            acc_ref[st, :] = alpha * acc_ref[st, :]
                        m_ref[st, :] = m_new
                        p_ref[st, 0:width] = ph.astype(p_ref.dtype)
                    else:
                        if first:
                            l_ref[st, :] = ps[local]
                        else:
                            l_ref[st, :] += ps[local]
                        p_ref[st, 0:width] = p[local].astype(p_ref.dtype)
            lo, hi = 2 * hfs[0] * half, 2 * (hfs[-1] + 1) * half
            for rows in (slice(lo, (lo + hi) // 2), slice((lo + hi) // 2, hi)):
                pv = jnp.dot(p_ref[rows, 0:width], v_ref[ks, :], preferred_element_type=jnp.float32)
                if first:
                    acc_ref[rows, :] = pv
                else:
                    acc_ref[rows, :] += pv

        def causal_tiles(shifted):
            tile((0,), q0, half, 0, shifted, True)
            tile((1,), q0, blk, half, shifted, True)

            def body(j, c):
                tile((0, 1), pl.multiple_of(j * blk, blk), blk, None, shifted, False)
                return c
            lax.fori_loop(0, qi, body, 0)

        @pl.when(unshifted)
        def _():
            causal_tiles(False)

        @pl.when(jnp.logical_not(unshifted))
        def _():
            causal_tiles(True)

        for hf in range(2):
            r1 = 1.0 / jnp.sum(l_ref[stacked(hf, 0), :], axis=-1, keepdims=True)
            r2 = lam / jnp.sum(l_ref[stacked(hf, 1), :], axis=-1, keepdims=True)
            o = acc_ref[stacked(hf, 0), :] * r1 - acc_ref[stacked(hf, 1), :] * r2
            o_ref[pl.ds(q0 + hf * half, half), :] = _rms_rows(o, out_gain).astype(o_ref.dtype)
        return carry

    lax.fori_loop(0, nq, q_block, 0)


def _attention(proj, small_logits, lamv, sub_gain, *, batch, seq, lam_init, blk=1024):
    nq = seq // blk
    hw = 2 * HEAD_DIM
    kern = functools.partial(_attn_kernel, blk=blk, nq=nq, lam_init=lam_init)
    vmem = _vmem_limit(8 * _nbytes((seq, hw), jnp.bfloat16),
                       _nbytes((2 * blk, hw), jnp.float32), _nbytes((2 * blk, blk), jnp.bfloat16),
                       4 * _nbytes((blk, V7X_LANES), jnp.float32),
                       4 * _nbytes((blk, blk), jnp.float32))
    return pl.pallas_call(
        kern,
        out_shape=jax.ShapeDtypeStruct((batch * seq, D_MODEL), jnp.bfloat16),
        grid=(batch, ATTN_HEADS),
        in_specs=[
            pl.BlockSpec(memory_space=pltpu.SMEM),
            pl.BlockSpec((4, HEAD_DIM), lambda b, h: (0, 0)),
            pl.BlockSpec((seq, hw), lambda b, h: (b, h)),
            pl.BlockSpec((seq, hw), lambda b, h: (b, ATTN_HEADS + h)),
            pl.BlockSpec((seq, hw), lambda b, h: (b, 2 * ATTN_HEADS + h)),
            pl.BlockSpec((1, hw), lambda b, h: (0, 0)),
        ],
        out_specs=pl.BlockSpec((seq, hw), lambda b, h: (b, h)),
        scratch_shapes=[
            pltpu.VMEM((2 * blk, 1), jnp.float32),
            pltpu.VMEM((2 * blk, V7X_LANES), jnp.float32),
            pltpu.VMEM((2 * blk, hw), jnp.float32),
            pltpu.VMEM((2 * blk, blk), jnp.bfloat16),
        ],
        compiler_params=pltpu.CompilerParams(
            dimension_semantics=("arbitrary", "arbitrary"), vmem_limit_bytes=vmem),
        name="diff_attn",
    )(small_logits, lamv, proj, proj, proj, sub_gain)


_SSM_FACTORISED_MAX_LOG = 60.0


def _cmul(ar, ai, br, bi):
    return ar * br - ai * bi, ar * bi + ai * br


def _ssm_kernel(flag_ref, u_ref, mixed_ref, bmat_ref, cmat_ref, pinv_ref, pfwd_ref, pstep_ref, d_ref,
                wglu_ref, bglu_ref, og_ref, o_ref, h_ref, y_ref, bu_ref, hs_ref, *, chunk):
    del mixed_ref
    c = pl.program_id(1)

    @pl.when(c == 0)
    def _():
        h_ref[...] = jnp.zeros(h_ref.shape, jnp.float32)

    n = SSM_CL_STATE
    cluster_batch = SSM_CLUSTERS
    factorised = flag_ref[0] == 1

    def lanes_of(cl):
        return slice(cl * V7X_LANES, (cl + 1) * V7X_LANES)

    def finish():
        y = y_ref[...] + d_ref[...] * u_ref[...].astype(jnp.float32)
        y = jax.nn.gelu(y, approximate=True)
        z = jnp.dot(y.astype(jnp.bfloat16), wglu_ref[...], preferred_element_type=jnp.float32)
        z = z + bglu_ref[...]
        y = y * (1.0 / (1.0 + jnp.exp(-z)))
        o_ref[...] = _rms_rows(y, og_ref[...]).astype(o_ref.dtype)

    @pl.when(factorised)
    def _():
        row = lax.broadcasted_iota(jnp.int32, (chunk, chunk), 0)
        col = lax.broadcasted_iota(jnp.int32, (chunk, chunk), 1)
        tri = (col <= row).astype(jnp.bfloat16)
        for c_lo in range(0, SSM_CLUSTERS, cluster_batch):
            clusters = range(c_lo, c_lo + cluster_batch)
            bus = {cl: jnp.dot(u_ref[:, lanes_of(cl)], bmat_ref[cl],
                               preferred_element_type=jnp.float32) for cl in clusters}
            xs = {}
            for cl in clusters:
                bu = bus[cl]
                xr, xi = _cmul(pinv_ref[cl, :, :n], pinv_ref[cl, :, n:], bu[:, :n], bu[:, n:])
                xs[cl] = jnp.concatenate([xr, xi], axis=1).astype(jnp.bfloat16)
            css = {cl: jnp.dot(tri, xs[cl], preferred_element_type=jnp.float32) for cl in clusters}
            hs = {}
            for cl in clusters:
                cs = css[cl]
                cr, ci = _cmul(pstep_ref[1, cl:cl + 1, :n], pstep_ref[1, cl:cl + 1, n:],
                               h_ref[cl:cl + 1, :n], h_ref[cl:cl + 1, n:])
                hr, hi = _cmul(pfwd_ref[cl, :, :n], pfwd_ref[cl, :, n:],
                               cs[:, :n] + cr, cs[:, n:] + ci)
                h_ref[cl:cl + 1, :n] = hr[chunk - 1:chunk, :]
                h_ref[cl:cl + 1, n:] = hi[chunk - 1:chunk, :]
                hs[cl] = jnp.concatenate([hr, hi], axis=1).astype(jnp.bfloat16)
            for cl in clusters:
                y_ref[:, lanes_of(cl)] = jnp.dot(hs[cl], cmat_ref[cl],
                                                 preferred_element_type=jnp.float32)
        finish()

    @pl.when(jnp.logical_not(factorised))
    def _():
        for cl in range(SSM_CLUSTERS):
            bu_ref[...] = jnp.dot(u_ref[:, lanes_of(cl)], bmat_ref[cl],
                                  preferred_element_type=jnp.float32)
            lr, li = pstep_ref[0, cl:cl + 1, :n], pstep_ref[0, cl:cl + 1, n:]

            def step(t, carry):
                hr, hi = carry
                ar, ai = _cmul(lr, li, hr, hi)
                hr = ar + bu_ref[pl.ds(t, 1), :n]
                hi = ai + bu_ref[pl.ds(t, 1), n:]
                hs_ref[pl.ds(t, 1), :n] = hr
                hs_ref[pl.ds(t, 1), n:] = hi
                return hr, hi

            hr, hi = lax.fori_loop(0, chunk, step, (h_ref[cl:cl + 1, :n], h_ref[cl:cl + 1, n:]))
            h_ref[cl:cl + 1, :n] = hr
            h_ref[cl:cl + 1, n:] = hi
            y_ref[:, lanes_of(cl)] = jnp.dot(hs_ref[...].astype(jnp.bfloat16), cmat_ref[cl],
                                             preferred_element_type=jnp.float32)
        finish()


def _ssm_tables(log_dt, a_re, a_im, b_re, b_im, c_re, c_im, chunk):
    f32 = jnp.float32
    n, c = SSM_STATE, SSM_GROUP
    dt = jnp.exp(log_dt.astype(f32))[:, None]
    ar, ai = a_re.astype(f32), a_im.astype(f32)
    mag = jnp.exp(ar * dt)
    lb_re = mag * jnp.cos(ai * dt)
    lb_im = mag * jnp.sin(ai * dt)
    pr, pi_ = lb_re - 1.0, lb_im
    den = ar * ar + ai * ai
    z_re = ((pr * ar + pi_ * ai) / den)[..., None]
    z_im = ((pi_ * ar - pr * ai) / den)[..., None]
    br, bi = b_re.astype(f32), b_im.astype(f32)
    bb_re = z_re * br - z_im * bi
    bb_im = z_re * bi + z_im * br

    def pair_cols(re, im):
        return jnp.concatenate([re, im], axis=-1)

    log_re = (ar * dt).reshape(SSM_CLUSTERS, 1, SSM_CLUSTER * n)
    log_im = (ai * dt).reshape(SSM_CLUSTERS, 1, SSM_CLUSTER * n)

    def table(k):
        kk = k.astype(f32)[None, :, None]
        pm = jnp.exp(kk * log_re)
        ph = kk * log_im
        return pair_cols(pm * jnp.cos(ph), pm * jnp.sin(ph))

    c0 = chunk // 2
    t = jnp.arange(chunk)
    pfwd = table(t - c0)
    pinv = table(c0 - t)
    pstep = table(jnp.array([1, c0 + 1])).transpose(1, 0, 2)
    factorised = ((c0 + 1) * jnp.max(jnp.abs(log_re)) <= _SSM_FACTORISED_MAX_LOG)

    eye = jnp.eye(SSM_CLUSTER, dtype=f32)
    def bpart(bb):
        b4 = bb.reshape(SSM_CLUSTERS, SSM_CLUSTER, n, c)
        return jnp.einsum('kgnc,gh->kgchn', b4, eye).reshape(
            SSM_CLUSTERS, V7X_LANES, SSM_CLUSTER * n)
    bmat = pair_cols(bpart(bb_re), bpart(bb_im)).astype(jnp.bfloat16)
    def cpart(cc):
        c4 = cc.astype(f32).reshape(SSM_CLUSTERS, SSM_CLUSTER, c, n)
        return jnp.einsum('kgcn,gh->khcgn', c4, eye).reshape(
            SSM_CLUSTERS, V7X_LANES, SSM_CLUSTER * n)
    cmat = pair_cols(cpart(c_re), -cpart(c_im)).transpose(0, 2, 1).astype(jnp.bfloat16)
    return factorised.astype(jnp.int32).reshape(1), bmat, cmat, pinv, pfwd, pstep


def _ssm(proj, mixed, tables, d_skip, w_glu, b_glu, out_gain, *, batch, seq, chunk=SSM_CHUNK):
    factorised, bmat, cmat, pinv, pfwd, pstep = tables
    nc = seq // chunk
    u_col = 3 * ATTN_WIDTH // SSM_WIDTH
    o_col = ATTN_WIDTH // SSM_WIDTH
    kern = functools.partial(_ssm_kernel, chunk=chunk)
    const3 = lambda b, c: (0, 0, 0)
    const2 = lambda b, c: (0, 0)
    vmem = _vmem_limit(2 * (bmat.nbytes + cmat.nbytes + pinv.nbytes + pfwd.nbytes + pstep.nbytes),
                       4 * _nbytes((SSM_WIDTH, SSM_WIDTH), jnp.bfloat16),
                       10 * _nbytes((chunk, SSM_WIDTH), jnp.float32))
    return pl.pallas_call(
        kern,
        out_shape=jax.ShapeDtypeStruct(mixed.shape, mixed.dtype),
        grid=(batch, nc),
        in_specs=[
            pl.BlockSpec(memory_space=pltpu.SMEM),
            pl.BlockSpec((chunk, SSM_WIDTH), lambda b, c: (b * nc + c, u_col)),
            pl.BlockSpec(memory_space=pl.ANY),
            pl.BlockSpec(bmat.shape, const3),
            pl.BlockSpec(cmat.shape, const3),
            pl.BlockSpec(pinv.shape, const3),
            pl.BlockSpec(pfwd.shape, const3),
            pl.BlockSpec(pstep.shape, const3),
            pl.BlockSpec((1, SSM_WIDTH), const2),
            pl.BlockSpec((SSM_WIDTH, SSM_WIDTH), const2),
            pl.BlockSpec((1, SSM_WIDTH), const2),
            pl.BlockSpec((1, SSM_WIDTH), const2),
        ],
        out_specs=pl.BlockSpec((chunk, SSM_WIDTH), lambda b, c: (b * nc + c, o_col)),
        scratch_shapes=[
            pltpu.VMEM((SSM_CLUSTERS, SSM_CL_COLS), jnp.float32),
            pltpu.VMEM((chunk, SSM_WIDTH), jnp.float32),
            pltpu.VMEM((chunk, SSM_CL_COLS), jnp.float32),
            pltpu.VMEM((chunk, SSM_CL_COLS), jnp.float32),
        ],
        input_output_aliases={2: 0},
        compiler_params=pltpu.CompilerParams(
            dimension_semantics=("arbitrary", "arbitrary"), vmem_limit_bytes=vmem),
        name="s5_ssm",
    )(factorised, proj, mixed, bmat, cmat, pinv, pfwd, pstep, d_skip, w_glu, b_glu, out_gain)


def _res_matmul_kernel(a_ref, w_ref, r_ref, o_ref, *maybe_bf16_ref):
    out = r_ref[...] + jnp.dot(a_ref[...], w_ref[...], preferred_element_type=jnp.float32)
    o_ref[...] = out
    for ob_ref in maybe_bf16_ref:
        ob_ref[...] = out.astype(ob_ref.dtype)


def _res_matmul(a, w, res, *, tm, tn, name, with_bf16_copy=False):
    m, k = a.shape
    n = w.shape[1]
    assert m % tm == 0 and n % tn == 0
    vmem = _vmem_limit(2 * _nbytes((tm, k), a.dtype), 2 * _nbytes((k, tn), w.dtype),
                       6 * _nbytes((tm, tn), jnp.float32))
    tile = pl.BlockSpec((tm, tn), lambda i, j: (i, j))
    out_shape = [jax.ShapeDtypeStruct((m, n), jnp.float32)]
    if with_bf16_copy:
        out_shape.append(jax.ShapeDtypeStruct((m, n), jnp.bfloat16))
    outs = pl.pallas_call(
        _res_matmul_kernel,
        out_shape=out_shape,
        grid=(m // tm, n // tn),
        in_specs=[
            pl.BlockSpec((tm, k), lambda i, j: (i, 0)),
            pl.BlockSpec((k, tn), lambda i, j: (0, j)),
            tile,
        ],
        out_specs=[tile] * len(out_shape),
        compiler_params=pltpu.CompilerParams(
            dimension_semantics=("parallel", "parallel"), vmem_limit_bytes=vmem),
        name=name,
    )(a, w, res)
    return outs if with_bf16_copy else outs[0]


def _ffn_up_kernel(x_ref, halo_ref, g_ref, wg_ref, wu_ref, cw_ref, cb_ref, o_ref, h_ref,
                   *, tm, blocks_per_seq):
    i = pl.program_id(0)
    j = pl.program_id(1)
    halo = V7X_BF16_ROWS_PER_TILE

    @pl.when(j == 0)
    def _():
        hh = _rms_rows(halo_ref[...].astype(jnp.float32), g_ref[...])
        hh = jnp.where(i % blocks_per_seq == 0, 0.0, hh)
        h_ref[0:halo, :] = hh.astype(h_ref.dtype)
        _norm_into(x_ref, g_ref, h_ref, rows=tm, row_chunk=16, h_offset=halo)

    g_all = jnp.dot(h_ref[...], wg_ref[...], preferred_element_type=jnp.float32)
    up = jnp.dot(h_ref[halo:, :], wu_ref[...], preferred_element_type=jnp.float32)
    rows = 128
    for r0 in range(0, tm, rows):
        for c0 in range(0, o_ref.shape[1], V7X_LANES):
            cs = slice(c0, c0 + V7X_LANES)
            g = g_all[r0:r0 + halo + rows, cs]
            t = cw_ref[1:2, cs] * g + pltpu.roll(cw_ref[0:1, cs] * g, 1, 0)
            gc = (cb_ref[:, cs] + cw_ref[2:3, cs] * g + pltpu.roll(t, 1, 0))[halo:, :]
            act = gc * (1.0 / (1.0 + jnp.exp(-gc))) * up[r0:r0 + rows, cs]
            o_ref[r0:r0 + rows, cs] = act.astype(o_ref.dtype)


def _ffn_up(x1, gain, wg, wu, conv_w, conv_b, *, seq, tm=1024, tf=F_TILE):
    m, k = x1.shape
    f = wg.shape[1]
    halo = V7X_BF16_ROWS_PER_TILE
    assert m % tm == 0 and seq % tm == 0
    kern = functools.partial(_ffn_up_kernel, tm=tm, blocks_per_seq=seq // tm)
    vmem = _vmem_limit(2 * _nbytes((tm, k), x1.dtype), _nbytes((tm + halo, k), jnp.bfloat16),
                       4 * _nbytes((k, tf), jnp.bfloat16), 2 * _nbytes((tm, tf), jnp.bfloat16),
                       3 * _nbytes((tm + halo, tf), jnp.float32))
    return pl.pallas_call(
        kern,
        out_shape=jax.ShapeDtypeStruct((m, f), jnp.bfloat16),
        grid=(m // tm, pl.cdiv(f, tf)),
        in_specs=[
            pl.BlockSpec((tm, k), lambda i, j: (i, 0)),
            pl.BlockSpec((halo, k), lambda i, j: (jnp.maximum(i * (tm // halo) - 1, 0), 0)),
            pl.BlockSpec((1, k), lambda i, j: (0, 0)),
            pl.BlockSpec((k, tf), lambda i, j: (0, j)),
            pl.BlockSpec((k, tf), lambda i, j: (0, j)),
            pl.BlockSpec((CONV_WIDTH, tf), lambda i, j: (0, j)),
            pl.BlockSpec((1, tf), lambda i, j: (0, j)),
        ],
        out_specs=pl.BlockSpec((tm, tf), lambda i, j: (i, j)),
        scratch_shapes=[pltpu.VMEM((tm + halo, k), jnp.bfloat16)],
        compiler_params=pltpu.CompilerParams(
            dimension_semantics=("arbitrary", "arbitrary"), vmem_limit_bytes=vmem),
        name="ffn_up",
    )(x1, x1, gain, wg, wu, conv_w, conv_b)


def kernel(x, attn_norm, w_in, q_gain, k_gain, lam_q1, lam_k1, lam_q2, lam_k2, sub_gain, ssm_log_dt, ssm_a_re, ssm_a_im, ssm_b_re, ssm_b_im, ssm_c_re, ssm_c_im, ssm_d, ssm_w_glu, ssm_b_glu, ssm_out_gain, w_out, ffn_norm, w_gate, w_up, conv_w, conv_b, w_down):
    b, s, d = x.shape
    depth = w_in.shape[0]
    bf16, f32 = jnp.bfloat16, jnp.float32
    xf = x.reshape(b * s, d)
    for l in range(depth):
        lam_init = 0.8 - 0.6 * math.exp(-0.3 * l)
        q_scale = HEAD_DIM ** -0.5 * math.log2(math.e)
        qg, kg = q_gain[l].astype(f32), k_gain[l].astype(f32)
        col_gain = jnp.concatenate([
            jnp.tile(qg * q_scale, ATTN_WIDTH // HEAD_DIM),
            jnp.tile(kg, ATTN_WIDTH // HEAD_DIM),
            jnp.ones((IN_WIDTH - 2 * ATTN_WIDTH,), f32)])[None, :]
        proj, (wo_bf, wg_bf, wu_bf, wd_bf) = _in_proj(
            xf, attn_norm[l].astype(f32)[None, :], w_in[l].astype(bf16), col_gain,
            (w_out, w_gate, w_up, w_down), l)

        logit_bound = 1.01 * HEAD_DIM * q_scale * jnp.max(jnp.abs(qg)) * jnp.max(jnp.abs(kg))
        small_logits = (logit_bound <= _UNSHIFTED_SOFTMAX_MAX_LOG2).astype(jnp.int32).reshape(1)
        lamv = jnp.stack([lam_q1[l], lam_k1[l], lam_q2[l], lam_k2[l]]).astype(f32)
        mixed = _attention(proj, small_logits, lamv, sub_gain[l].astype(f32)[None, :],
                           batch=b, seq=s, lam_init=lam_init)

        tables = _ssm_tables(ssm_log_dt[l], ssm_a_re[l], ssm_a_im[l], ssm_b_re[l], ssm_b_im[l],
                             ssm_c_re[l], ssm_c_im[l], SSM_CHUNK)
        mixed = _ssm(proj, mixed, tables, ssm_d[l].astype(f32).reshape(1, SSM_WIDTH),
                     ssm_w_glu[l].astype(bf16), ssm_b_glu[l].astype(f32)[None, :],
                     ssm_out_gain[l].astype(f32)[None, :], batch=b, seq=s)

        x1, x1_bf = _res_matmul(mixed, wo_bf, xf, tm=1024, tn=512, name="out_proj",
                                with_bf16_copy=True)

        act = _ffn_up(x1_bf, ffn_norm[l].astype(f32)[None, :], wg_bf, wu_bf, conv_w[l].astype(f32),
                      conv_b[l].astype(f32)[None, :], seq=s)
        xf = _res_matmul(act, wd_bf, x1, tm=512, tn=512, name="ffn_down")
    return xf.reshape(b, s, d)
```

```python
import functools
import math

import jax
import jax.numpy as jnp
from jax import lax
from jax.experimental import pallas as pl
from jax.experimental.pallas import tpu as pltpu

D_MODEL = 4096
ATTN_HEADS = 12
HEAD_DIM = 128
ATTN_WIDTH = ATTN_HEADS * 2 * HEAD_DIM
SSM_WIDTH = D_MODEL - ATTN_WIDTH
SSM_GROUP = 16
SSM_GROUPS = SSM_WIDTH // SSM_GROUP
SSM_STATE = 64
IN_WIDTH = 3 * ATTN_WIDTH + SSM_WIDTH
D_FF = 11008
CONV_WIDTH = 3
EPS = 1e-6

V7X_LANES = 128
V7X_SUBLANES = 8
V7X_BF16_ROWS_PER_TILE = 2 * V7X_SUBLANES
V7X_VMEM_BYTES = 64 * 1024 * 1024
V7X_VMEM_RESERVED_BYTES = 6 * 1024 * 1024

F_TILE = 512
SSM_CLUSTER = V7X_LANES // SSM_GROUP
SSM_CLUSTERS = SSM_GROUPS // SSM_CLUSTER
SSM_CL_STATE = SSM_CLUSTER * SSM_STATE
SSM_CL_COLS = 2 * SSM_CL_STATE
SSM_CHUNK = 256


def _vmem_limit(*nbytes):
    want = int(sum(nbytes)) + 12 * 1024 * 1024
    return min(want, V7X_VMEM_BYTES - V7X_VMEM_RESERVED_BYTES)


def _nbytes(shape, dtype):
    return math.prod(shape) * jnp.dtype(dtype).itemsize


def _rms_rows(x, gain):
    ms = jnp.mean(x * x, axis=-1, keepdims=True)
    return x * lax.rsqrt(ms + EPS) * gain


def _norm_into(x_ref, gain_ref, h_ref, *, rows, row_chunk, h_offset):
    def body(c, carry):
        r = pl.multiple_of(c * row_chunk, row_chunk)
        xb = x_ref[pl.ds(r, row_chunk), :].astype(jnp.float32)
        h_ref[pl.ds(h_offset + r, row_chunk), :] = _rms_rows(xb, gain_ref[...]).astype(h_ref.dtype)
        return carry
    lax.fori_loop(0, rows // row_chunk, body, 0, unroll=4)


def _in_proj_kernel(x_ref, g_ref, w_ref, cg_ref, *refs, tm, tn, n_norm_tiles, n_parts, n_side):
    side_in, o_ref, side_out, h_ref = (refs[:n_side], refs[n_side], refs[n_side + 1:2 * n_side + 1],
                                       refs[2 * n_side + 1])
    j = pl.program_id(1)
    h_ref = h_ref.at[pl.program_id(2)]

    @pl.when(j == 0)
    def _():
        _norm_into(x_ref, g_ref, h_ref, rows=tm, row_chunk=16, h_offset=0)

    is_qk = j < n_norm_tiles
    part = tn // n_parts
    for c0 in range(0, tn, part):
        acc = jnp.dot(h_ref[...], w_ref[:, c0:c0 + part], preferred_element_type=jnp.float32)
        for s0 in range(0, part, HEAD_DIM):
            raw = acc[:, s0:s0 + HEAD_DIM]
            sl = slice(c0 + s0, c0 + s0 + HEAD_DIM)
            o_ref[:, sl] = jnp.where(is_qk, _rms_rows(raw, cg_ref[:, sl]), raw).astype(o_ref.dtype)

    for src, dst in zip(side_in, side_out):
        dst[...] = src[...].astype(dst.dtype)


def _in_proj(x2d, gain, w_bf16, col_gain, side_f32, layer, *, tm=512, tn=1024, row_blocks=2):
    m, k = x2d.shape
    n = w_bf16.shape[1]
    n_norm_tiles = 2 * ATTN_WIDTH // tn
    n_col = n // tn
    n_steps = (m // tm) * n_col
    assert 2 * ATTN_WIDTH % tn == 0 and m % (tm * row_blocks) == 0 and n % tn == 0
    step = lambda i, j, r: (i * n_col + j) * row_blocks + r
    kern = functools.partial(_in_proj_kernel, tm=tm, tn=tn, n_norm_tiles=n_norm_tiles, n_parts=4,
                             n_side=len(side_f32))
    side_in_specs, side_out_specs, side_shapes, side_bytes = [], [], [], 0
    for a in side_f32:
        _, rows, cols = a.shape
        tile = V7X_BF16_ROWS_PER_TILE
        slab = next(s for s in range(tile, rows + 1, tile)
                    if rows % s == 0 and rows // s <= n_steps)
        last = rows // slab - 1
        side_in_specs.append(pl.BlockSpec(
            (None, slab, cols),
            lambda i, j, r, last=last: (layer, jnp.minimum(step(i, j, r), last), 0)))
        side_out_specs.append(pl.BlockSpec(
            (slab, cols), lambda i, j, r, last=last: (jnp.minimum(step(i, j, r), last), 0)))
        side_shapes.append(jax.ShapeDtypeStruct((rows, cols), jnp.bfloat16))
        side_bytes += 2 * _nbytes((slab, cols), jnp.float32) + 2 * _nbytes((slab, cols), jnp.bfloat16)
    vmem = _vmem_limit(2 * _nbytes((tm, k), jnp.float32),
                       row_blocks * _nbytes((tm, k), jnp.bfloat16),
                       2 * _nbytes((k, tn), jnp.bfloat16), 2 * _nbytes((tm, tn), jnp.bfloat16),
                       _nbytes((tm, tn), jnp.float32), side_bytes)
    outs = pl.pallas_call(
        kern,
        out_shape=[jax.ShapeDtypeStruct((m, n), jnp.bfloat16)] + side_shapes,
        grid=(m // (tm * row_blocks), n_col, row_blocks),
        in_specs=[
            pl.BlockSpec((tm, k), lambda i, j, r: (
                i * row_blocks + jnp.where(j == 0, r, row_blocks - 1), 0)),
            pl.BlockSpec((1, k), lambda i, j, r: (0, 0)),
            pl.BlockSpec((k, tn), lambda i, j, r: (0, j)),
            pl.BlockSpec((1, tn), lambda i, j, r: (0, j)),
        ] + side_in_specs,
        out_specs=[pl.BlockSpec((tm, tn), lambda i, j, r: (i * row_blocks + r, j))] + side_out_specs,
        scratch_shapes=[pltpu.VMEM((row_blocks, tm, k), jnp.bfloat16)],
        compiler_params=pltpu.CompilerParams(
            dimension_semantics=("arbitrary", "arbitrary", "arbitrary"), vmem_limit_bytes=vmem),
        name="in_proj",
    )(x2d, gain, w_bf16, col_gain, *side_f32)
    return outs[0], outs[1:]


_UNSHIFTED_SOFTMAX_MAX_LOG2 = 60.0


def _qk_scores(q, k, row_offset):
    s = lax.dot_general(q, k, (((1,), (1,)), ((), ())), preferred_element_type=jnp.float32)
    if row_offset is not None:
        row = lax.broadcasted_iota(jnp.int32, s.shape, 0)
        col = lax.broadcasted_iota(jnp.int32, s.shape, 1)
        s = jnp.where(col <= row + row_offset, s, -jnp.inf)
    return s


def _lane_partial_sums(p):
    ps = p[:, 0:V7X_LANES]
    for c in range(1, p.shape[1] // V7X_LANES):
        ps = ps + p[:, c * V7X_LANES:(c + 1) * V7X_LANES]
    return ps


def _lane_totals(x):
    ones = jnp.ones((V7X_LANES, V7X_LANES), jnp.bfloat16)
    hi = x.astype(jnp.bfloat16)
    lo = (x - hi.astype(jnp.float32)).astype(jnp.bfloat16)
    return (jnp.dot(hi, ones, preferred_element_type=jnp.float32)
            + jnp.dot(lo, ones, preferred_element_type=jnp.float32))


def _attn_kernel(small_ref, lamv_ref, q_ref, k_ref, v_ref, sg_ref, o_ref,
                 m_ref, l_ref, acc_ref, p_ref, *, blk, nq, lam_init):
    unshifted = small_ref[0] == 1
    lamv = lamv_ref[...]
    lam = (jnp.exp(jnp.sum(lamv[0:1] * lamv[1:2], axis=-1, keepdims=True))
           - jnp.exp(jnp.sum(lamv[2:3] * lamv[3:4], axis=-1, keepdims=True)) + lam_init)
    out_gain = sg_ref[...] * (1.0 - lam_init)

    def head(mi):
        return slice(mi * HEAD_DIM, (mi + 1) * HEAD_DIM)

    half = blk // 2

    def stacked(hf, mi):
        return slice((2 * hf + mi) * half, (2 * hf + mi + 1) * half)

    def q_block(qi, carry):
        q0 = pl.multiple_of(qi * blk, blk)

        def tile(hfs, start, width, mask_offset, shifted, first):
            ks = pl.ds(start, width)
            n_rows = len(hfs) * half
            qs = pl.ds(q0 + hfs[0] * half, n_rows)
            for mi in range(2):
                s = _qk_scores(q_ref[qs, head(mi)], k_ref[ks, head(mi)], mask_offset)
                if not shifted:
                    p = jnp.exp2(s)
                    ps = _lane_partial_sums(p)
                for n, hf in enumerate(hfs):
                    st = stacked(hf, mi)
                    local = slice(n * half, (n + 1) * half)
                    if shifted:
                        m_cur = jnp.max(s[local], axis=-1, keepdims=True)
                        m_new = m_cur if first else jnp.maximum(m_ref[st, :], m_cur)
                        ph = jnp.exp2(s[local] - m_new)
                        l_cur = jnp.sum(ph, axis=-1, keepdims=True)
                        if first:
                            l_ref[st, :] = jnp.zeros((half, V7X_LANES), jnp.float32)
                            l_ref[st, 0:1] = l_cur
                        else:
                            alpha = jnp.exp2(m_ref[st, :] - m_new)
                            l_ref[st, 0:1] = alpha * l_ref[st, 0:1] + l_cur
                            acc_ref[st, :] = alpha * acc_ref[st, :]
                        m_ref[st, :] = m_new
                        p_ref[st, 0:width] = ph.astype(p_ref.dtype)
                    else:
                        if first:
                            l_ref[st, :] = ps[local]
                        else:
                            l_ref[st, :] += ps[local]
                        p_ref[st, 0:width] = p[local].astype(p_ref.dtype)
            lo, hi = 2 * hfs[0] * half, 2 * (hfs[-1] + 1) * half
            for rows in (slice(lo, (lo + hi) // 2), slice((lo + hi) // 2, hi)):
                pv = jnp.dot(p_ref[rows, 0:width], v_ref[ks, :], preferred_element_type=jnp.float32)
                if first:
                    acc_ref[rows, :] = pv
                else:
                    acc_ref[rows, :] += pv

        def causal_tiles(shifted):
            tile((0,), q0, half, 0, shifted, True)
            tile((1,), q0, blk, half, shifted, True)

            def body(j, c):
                tile((0, 1), pl.multiple_of(j * blk, blk), blk, None, shifted, False)
                return c
            lax.fori_loop(0, qi, body, 0)

        @pl.when(unshifted)
        def _():
            causal_tiles(False)

        @pl.when(jnp.logical_not(unshifted))
        def _():
            causal_tiles(True)

        for hf in range(2):
            r1 = 1.0 / _lane_totals(l_ref[stacked(hf, 0), :])
            r2 = lam / _lane_totals(l_ref[stacked(hf, 1), :])
            o = (acc_ref[stacked(hf, 0), :] * jnp.concatenate([r1, r1], axis=1)
                 - acc_ref[stacked(hf, 1), :] * jnp.concatenate([r2, r2], axis=1))
            o_ref[pl.ds(q0 + hf * half, half), :] = _rms_rows(o, out_gain).astype(o_ref.dtype)
        return carry

    lax.fori_loop(0, nq, q_block, 0)


def _attention(proj, small_logits, lamv, sub_gain, *, batch, seq, lam_init, blk=1024):
    nq = seq // blk
    hw = 2 * HEAD_DIM
    kern = functools.partial(_attn_kernel, blk=blk, nq=nq, lam_init=lam_init)
    vmem = _vmem_limit(8 * _nbytes((seq, hw), jnp.bfloat16),
                       _nbytes((2 * blk, hw), jnp.float32), _nbytes((2 * blk, blk), jnp.bfloat16),
                       4 * _nbytes((blk, V7X_LANES), jnp.float32),
                       4 * _nbytes((blk, blk), jnp.float32))
    return pl.pallas_call(
        kern,
        out_shape=jax.ShapeDtypeStruct((batch * seq, D_MODEL), jnp.bfloat16),
        grid=(batch, ATTN_HEADS),
        in_specs=[
            pl.BlockSpec(memory_space=pltpu.SMEM),
            pl.BlockSpec((4, HEAD_DIM), lambda b, h: (0, 0)),
            pl.BlockSpec((seq, hw), lambda b, h: (b, h)),
            pl.BlockSpec((seq, hw), lambda b, h: (b, ATTN_HEADS + h)),
            pl.BlockSpec((seq, hw), lambda b, h: (b, 2 * ATTN_HEADS + h)),
            pl.BlockSpec((1, hw), lambda b, h: (0, 0)),
        ],
        out_specs=pl.BlockSpec((seq, hw), lambda b, h: (b, h)),
        scratch_shapes=[
            pltpu.VMEM((2 * blk, 1), jnp.float32),
            pltpu.VMEM((2 * blk, V7X_LANES), jnp.float32),
            pltpu.VMEM((2 * blk, hw), jnp.float32),
            pltpu.VMEM((2 * blk, blk), jnp.bfloat16),
        ],
        compiler_params=pltpu.CompilerParams(
            dimension_semantics=("arbitrary", "arbitrary"), vmem_limit_bytes=vmem),
        name="diff_attn",
    )(small_logits, lamv, proj, proj, proj, sub_gain)


_SSM_FACTORISED_MAX_LOG = 60.0


def _cmul(ar, ai, br, bi):
    return ar * br - ai * bi, ar * bi + ai * br


def _ssm_kernel(flag_ref, u_ref, mixed_ref, bmat_ref, cmat_ref, pinv_ref, pfwd_ref, pstep_ref, d_ref,
                wglu_ref, bglu_ref, og_ref, o_ref, h_ref, y_ref, bu_ref, hs_ref, *, chunk):
    del mixed_ref
    c = pl.program_id(1)

    @pl.when(c == 0)
    def _():
        h_ref[...] = jnp.zeros(h_ref.shape, jnp.float32)

    n = SSM_CL_STATE
    cluster_batch = SSM_CLUSTERS
    factorised = flag_ref[0] == 1

    def lanes_of(cl):
        return slice(cl * V7X_LANES, (cl + 1) * V7X_LANES)

    def finish():
        y = y_ref[...] + d_ref[...] * u_ref[...].astype(jnp.float32)
        y = jax.nn.gelu(y, approximate=True)
        z = jnp.dot(y.astype(jnp.bfloat16), wglu_ref[...], preferred_element_type=jnp.float32)
        z = z + bglu_ref[...]
        y = y * (1.0 / (1.0 + jnp.exp(-z)))
        o_ref[...] = _rms_rows(y, og_ref[...]).astype(o_ref.dtype)

    @pl.when(factorised)
    def _():
        row = lax.broadcasted_iota(jnp.int32, (chunk, chunk), 0)
        col = lax.broadcasted_iota(jnp.int32, (chunk, chunk), 1)
        tri = (col <= row).astype(jnp.bfloat16)
        for c_lo in range(0, SSM_CLUSTERS, cluster_batch):
            clusters = range(c_lo, c_lo + cluster_batch)
            bus = {cl: jnp.dot(u_ref[:, lanes_of(cl)], bmat_ref[cl],
                               preferred_element_type=jnp.float32) for cl in clusters}
            xs = {}
            for cl in clusters:
                bu = bus[cl]
                xr, xi = _cmul(pinv_ref[cl, :, :n], pinv_ref[cl, :, n:], bu[:, :n], bu[:, n:])
                xs[cl] = jnp.concatenate([xr, xi], axis=1).astype(jnp.bfloat16)
            css = {cl: jnp.dot(tri, xs[cl], preferred_element_type=jnp.float32) for cl in clusters}
            hs = {}
            for cl in clusters:
                cs = css[cl]
                cr, ci = _cmul(pstep_ref[1, cl:cl + 1, :n], pstep_ref[1, cl:cl + 1, n:],
                               h_ref[cl:cl + 1, :n], h_ref[cl:cl + 1, n:])
                hr, hi = _cmul(pfwd_ref[cl, :, :n], pfwd_ref[cl, :, n:],
                               cs[:, :n] + cr, cs[:, n:] + ci)
                h_ref[cl:cl + 1, :n] = hr[chunk - 1:chunk, :]
                h_ref[cl:cl + 1, n:] = hi[chunk - 1:chunk, :]
                hs[cl] = jnp.concatenate([hr, hi], axis=1).astype(jnp.bfloat16)
            for cl in clusters:
                y_ref[:, lanes_of(cl)] = jnp.dot(hs[cl], cmat_ref[cl],
                                                 preferred_element_type=jnp.float32)
        finish()

    @pl.when(jnp.logical_not(factorised))
    def _():
        for cl in range(SSM_CLUSTERS):
            bu_ref[...] = jnp.dot(u_ref[:, lanes_of(cl)], bmat_ref[cl],
                                  preferred_element_type=jnp.float32)
            lr, li = pstep_ref[0, cl:cl + 1, :n], pstep_ref[0, cl:cl + 1, n:]

            def step(t, carry):
                hr, hi = carry
                ar, ai = _cmul(lr, li, hr, hi)
                hr = ar + bu_ref[pl.ds(t, 1), :n]
                hi = ai + bu_ref[pl.ds(t, 1), n:]
                hs_ref[pl.ds(t, 1), :n] = hr
                hs_ref[pl.ds(t, 1), n:] = hi
                return hr, hi

            hr, hi = lax.fori_loop(0, chunk, step, (h_ref[cl:cl + 1, :n], h_ref[cl:cl + 1, n:]))
            h_ref[cl:cl + 1, :n] = hr
            h_ref[cl:cl + 1, n:] = hi
            y_ref[:, lanes_of(cl)] = jnp.dot(hs_ref[...].astype(jnp.bfloat16), cmat_ref[cl],
                                             preferred_element_type=jnp.float32)
        finish()


def _ssm_tables(log_dt, a_re, a_im, b_re, b_im, c_re, c_im, chunk):
    f32 = jnp.float32
    n, c = SSM_STATE, SSM_GROUP
    dt = jnp.exp(log_dt.astype(f32))[:, None]
    ar, ai = a_re.astype(f32), a_im.astype(f32)
    mag = jnp.exp(ar * dt)
    lb_re = mag * jnp.cos(ai * dt)
    lb_im = mag * jnp.sin(ai * dt)
    pr, pi_ = lb_re - 1.0, lb_im
    den = ar * ar + ai * ai
    z_re = ((pr * ar + pi_ * ai) / den)[..., None]
    z_im = ((pi_ * ar - pr * ai) / den)[..., None]
    br, bi = b_re.astype(f32), b_im.astype(f32)
    bb_re = z_re * br - z_im * bi
    bb_im = z_re * bi + z_im * br

    def pair_cols(re, im):
        return jnp.concatenate([re, im], axis=-1)

    log_re = (ar * dt).reshape(SSM_CLUSTERS, 1, SSM_CLUSTER * n)
    log_im = (ai * dt).reshape(SSM_CLUSTERS, 1, SSM_CLUSTER * n)

    def table(k):
        kk = k.astype(f32)[None, :, None]
        pm = jnp.exp(kk * log_re)
        ph = kk * log_im
        return pair_cols(pm * jnp.cos(ph), pm * jnp.sin(ph))

    c0 = chunk // 2
    t = jnp.arange(chunk)
    pfwd = table(t - c0)
    pinv = table(c0 - t)
    pstep = table(jnp.array([1, c0 + 1])).transpose(1, 0, 2)
    factorised = ((c0 + 1) * jnp.max(jnp.abs(log_re)) <= _SSM_FACTORISED_MAX_LOG)

    eye = jnp.eye(SSM_CLUSTER, dtype=f32)
    def bpart(bb):
        b4 = bb.reshape(SSM_CLUSTERS, SSM_CLUSTER, n, c)
        return jnp.einsum('kgnc,gh->kgchn', b4, eye).reshape(
            SSM_CLUSTERS, V7X_LANES, SSM_CLUSTER * n)
    bmat = pair_cols(bpart(bb_re), bpart(bb_im)).astype(jnp.bfloat16)
    def cpart(cc):
        c4 = cc.astype(f32).reshape(SSM_CLUSTERS, SSM_CLUSTER, c, n)
        return jnp.einsum('kgcn,gh->khcgn', c4, eye).reshape(
            SSM_CLUSTERS, V7X_LANES, SSM_CLUSTER * n)
    cmat = pair_cols(cpart(c_re), -cpart(c_im)).transpose(0, 2, 1).astype(jnp.bfloat16)
    return factorised.astype(jnp.int32).reshape(1), bmat, cmat, pinv, pfwd, pstep


def _ssm(proj, mixed, tables, d_skip, w_glu, b_glu, out_gain, *, batch, seq, chunk=SSM_CHUNK):
    factorised, bmat, cmat, pinv, pfwd, pstep = tables
    nc = seq // chunk
    u_col = 3 * ATTN_WIDTH // SSM_WIDTH
    o_col = ATTN_WIDTH // SSM_WIDTH
    kern = functools.partial(_ssm_kernel, chunk=chunk)
    const3 = lambda b, c: (0, 0, 0)
    const2 = lambda b, c: (0, 0)
    vmem = _vmem_limit(2 * (bmat.nbytes + cmat.nbytes + pinv.nbytes + pfwd.nbytes + pstep.nbytes),
                       4 * _nbytes((SSM_WIDTH, SSM_WIDTH), jnp.bfloat16),
                       10 * _nbytes((chunk, SSM_WIDTH), jnp.float32))
    return pl.pallas_call(
        kern,
        out_shape=jax.ShapeDtypeStruct(mixed.shape, mixed.dtype),
        grid=(batch, nc),
        in_specs=[
            pl.BlockSpec(memory_space=pltpu.SMEM),
            pl.BlockSpec((chunk, SSM_WIDTH), lambda b, c: (b * nc + c, u_col)),
            pl.BlockSpec(memory_space=pl.ANY),
            pl.BlockSpec(bmat.shape, const3),
            pl.BlockSpec(cmat.shape, const3),
            pl.BlockSpec(pinv.shape, const3),
            pl.BlockSpec(pfwd.shape, const3),
            pl.BlockSpec(pstep.shape, const3),
            pl.BlockSpec((1, SSM_WIDTH), const2),
            pl.BlockSpec((SSM_WIDTH, SSM_WIDTH), const2),
            pl.BlockSpec((1, SSM_WIDTH), const2),
            pl.BlockSpec((1, SSM_WIDTH), const2),
        ],
        out_specs=pl.BlockSpec((chunk, SSM_WIDTH), lambda b, c: (b * nc + c, o_col)),
        scratch_shapes=[
            pltpu.VMEM((SSM_CLUSTERS, SSM_CL_COLS), jnp.float32),
            pltpu.VMEM((chunk, SSM_WIDTH), jnp.float32),
            pltpu.VMEM((chunk, SSM_CL_COLS), jnp.float32),
            pltpu.VMEM((chunk, SSM_CL_COLS), jnp.float32),
        ],
        input_output_aliases={2: 0},
        compiler_params=pltpu.CompilerParams(
            dimension_semantics=("arbitrary", "arbitrary"), vmem_limit_bytes=vmem),
        name="s5_ssm",
    )(factorised, proj, mixed, bmat, cmat, pinv, pfwd, pstep, d_skip, w_glu, b_glu, out_gain)


def _res_matmul_kernel(a_ref, w_ref, r_ref, o_ref, *maybe_bf16_ref):
    out = r_ref[...] + jnp.dot(a_ref[...], w_ref[...], preferred_element_type=jnp.float32)
    o_ref[...] = out
    for ob_ref in maybe_bf16_ref:
        ob_ref[...] = out.astype(ob_ref.dtype)


def _res_matmul(a, w, res, *, tm, tn, name, with_bf16_copy=False):
    m, k = a.shape
    n = w.shape[1]
    assert m % tm == 0 and n % tn == 0
    vmem = _vmem_limit(2 * _nbytes((tm, k), a.dtype), 2 * _nbytes((k, tn), w.dtype),
                       6 * _nbytes((tm, tn), jnp.float32))
    tile = pl.BlockSpec((tm, tn), lambda i, j: (i, j))
    out_shape = [jax.ShapeDtypeStruct((m, n), jnp.float32)]
    if with_bf16_copy:
        out_shape.append(jax.ShapeDtypeStruct((m, n), jnp.bfloat16))
    outs = pl.pallas_call(
        _res_matmul_kernel,
        out_shape=out_shape,
        grid=(m // tm, n // tn),
        in_specs=[
            pl.BlockSpec((tm, k), lambda i, j: (i, 0)),
            pl.BlockSpec((k, tn), lambda i, j: (0, j)),
            tile,
        ],
        out_specs=[tile] * len(out_shape),
        compiler_params=pltpu.CompilerParams(
            dimension_semantics=("parallel", "parallel"), vmem_limit_bytes=vmem),
        name=name,
    )(a, w, res)
    return outs if with_bf16_copy else outs[0]


def _ffn_up_kernel(x_ref, halo_ref, g_ref, wg_ref, wu_ref, cw_ref, cb_ref, o_ref, h_ref,
                   *, tm, blocks_per_seq):
    i = pl.program_id(0)
    j = pl.program_id(1)
    halo = V7X_BF16_ROWS_PER_TILE

    @pl.when(j == 0)
    def _():
        hh = _rms_rows(halo_ref[...].astype(jnp.float32), g_ref[...])
        hh = jnp.where(i % blocks_per_seq == 0, 0.0, hh)
        h_ref[0:halo, :] = hh.astype(h_ref.dtype)
        _norm_into(x_ref, g_ref, h_ref, rows=tm, row_chunk=16, h_offset=halo)

    g_all = jnp.dot(h_ref[...], wg_ref[...], preferred_element_type=jnp.float32)
    up = jnp.dot(h_ref[halo:, :], wu_ref[...], preferred_element_type=jnp.float32)
    rows = 128
    for r0 in range(0, tm, rows):
        for c0 in range(0, o_ref.shape[1], V7X_LANES):
            cs = slice(c0, c0 + V7X_LANES)
            g = g_all[r0:r0 + halo + rows, cs]
            t = cw_ref[1:2, cs] * g + pltpu.roll(cw_ref[0:1, cs] * g, 1, 0)
            gc = (cb_ref[:, cs] + cw_ref[2:3, cs] * g + pltpu.roll(t, 1, 0))[halo:, :]
            act = gc * (1.0 / (1.0 + jnp.exp(-gc))) * up[r0:r0 + rows, cs]
            o_ref[r0:r0 + rows, cs] = act.astype(o_ref.dtype)


def _ffn_up(x1, gain, wg, wu, conv_w, conv_b, *, seq, tm=1024, tf=F_TILE):
    m, k = x1.shape
    f = wg.shape[1]
    halo = V7X_BF16_ROWS_PER_TILE
    assert m % tm == 0 and seq % tm == 0
    kern = functools.partial(_ffn_up_kernel, tm=tm, blocks_per_seq=seq // tm)
    vmem = _vmem_limit(2 * _nbytes((tm, k), x1.dtype), _nbytes((tm + halo, k), jnp.bfloat16),
                       4 * _nbytes((k, tf), jnp.bfloat16), 2 * _nbytes((tm, tf), jnp.bfloat16),
                       3 * _nbytes((tm + halo, tf), jnp.float32))
    return pl.pallas_call(
        kern,
        out_shape=jax.ShapeDtypeStruct((m, f), jnp.bfloat16),
        grid=(m // tm, pl.cdiv(f, tf)),
        in_specs=[
            pl.BlockSpec((tm, k), lambda i, j: (i, 0)),
            pl.BlockSpec((halo, k), lambda i, j: (jnp.maximum(i * (tm // halo) - 1, 0), 0)),
            pl.BlockSpec((1, k), lambda i, j: (0, 0)),
            pl.BlockSpec((k, tf), lambda i, j: (0, j)),
            pl.BlockSpec((k, tf), lambda i, j: (0, j)),
            pl.BlockSpec((CONV_WIDTH, tf), lambda i, j: (0, j)),
            pl.BlockSpec((1, tf), lambda i, j: (0, j)),
        ],
        out_specs=pl.BlockSpec((tm, tf), lambda i, j: (i, j)),
        scratch_shapes=[pltpu.VMEM((tm + halo, k), jnp.bfloat16)],
        compiler_params=pltpu.CompilerParams(
            dimension_semantics=("arbitrary", "arbitrary"), vmem_limit_bytes=vmem),
        name="ffn_up",
    )(x1, x1, gain, wg, wu, conv_w, conv_b)


def kernel(x, attn_norm, w_in, q_gain, k_gain, lam_q1, lam_k1, lam_q2, lam_k2, sub_gain, ssm_log_dt, ssm_a_re, ssm_a_im, ssm_b_re, ssm_b_im, ssm_c_re, ssm_c_im, ssm_d, ssm_w_glu, ssm_b_glu, ssm_out_gain, w_out, ffn_norm, w_gate, w_up, conv_w, conv_b, w_down):
    b, s, d = x.shape
    depth = w_in.shape[0]
    bf16, f32 = jnp.bfloat16, jnp.float32
    xf = x.reshape(b * s, d)
    for l in range(depth):
        lam_init = 0.8 - 0.6 * math.exp(-0.3 * l)
        q_scale = HEAD_DIM ** -0.5 * math.log2(math.e)
        qg, kg = q_gain[l].astype(f32), k_gain[l].astype(f32)
        col_gain = jnp.concatenate([
            jnp.tile(qg * q_scale, ATTN_WIDTH // HEAD_DIM),
            jnp.tile(kg, ATTN_WIDTH // HEAD_DIM),
            jnp.ones((IN_WIDTH - 2 * ATTN_WIDTH,), f32)])[None, :]
        proj, (wo_bf, wg_bf, wu_bf, wd_bf) = _in_proj(
            xf, attn_norm[l].astype(f32)[None, :], w_in[l].astype(bf16), col_gain,
            (w_out, w_gate, w_up, w_down), l)

        logit_bound = 1.01 * HEAD_DIM * q_scale * jnp.max(jnp.abs(qg)) * jnp.max(jnp.abs(kg))
        small_logits = (logit_bound <= _UNSHIFTED_SOFTMAX_MAX_LOG2).astype(jnp.int32).reshape(1)
        lamv = jnp.stack([lam_q1[l], lam_k1[l], lam_q2[l], lam_k2[l]]).astype(f32)
        mixed = _attention(proj, small_logits, lamv, sub_gain[l].astype(f32)[None, :],
                           batch=b, seq=s, lam_init=lam_init)

        tables = _ssm_tables(ssm_log_dt[l], ssm_a_re[l], ssm_a_im[l], ssm_b_re[l], ssm_b_im[l],
                             ssm_c_re[l], ssm_c_im[l], SSM_CHUNK)
        mixed = _ssm(proj, mixed, tables, ssm_d[l].astype(f32).reshape(1, SSM_WIDTH),
                     ssm_w_glu[l].astype(bf16), ssm_b_glu[l].astype(f32)[None, :],
                     ssm_out_gain[l].astype(f32)[None, :], batch=b, seq=s)

        x1, x1_bf = _res_matmul(mixed, wo_bf, xf, tm=1024, tn=512, name="out_proj",
                                with_bf16_copy=True)

        act = _ffn_up(x1_bf, ffn_norm[l].astype(f32)[None, :], wg_bf, wu_bf, conv_w[l].astype(f32),
                      conv_b[l].astype(f32)[None, :], seq=s)
        xf = _res_matmul(act, wd_bf, x1, tm=512, tn=512, name="ffn_down")
    return xf.reshape(b, s, d)
```

```python
import functools
import math

import jax
import jax.numpy as jnp
from jax import lax
from jax.experimental import pallas as pl
from jax.experimental.pallas import tpu as pltpu

D_MODEL = 4096
ATTN_HEADS = 12
HEAD_DIM = 128
ATTN_WIDTH = ATTN_HEADS * 2 * HEAD_DIM
SSM_WIDTH = D_MODEL - ATTN_WIDTH
SSM_GROUP = 16
SSM_GROUPS = SSM_WIDTH // SSM_GROUP
SSM_STATE = 64
IN_WIDTH = 3 * ATTN_WIDTH + SSM_WIDTH
D_FF = 11008
CONV_WIDTH = 3
EPS = 1e-6

V7X_LANES = 128
V7X_SUBLANES = 8
V7X_BF16_ROWS_PER_TILE = 2 * V7X_SUBLANES
V7X_VMEM_BYTES = 64 * 1024 * 1024
V7X_VMEM_RESERVED_BYTES = 6 * 1024 * 1024

F_TILE = 512
_DOWN_TILE = 512
SSM_CLUSTER = V7X_LANES // SSM_GROUP
SSM_CLUSTERS = SSM_GROUPS // SSM_CLUSTER
SSM_CL_STATE = SSM_CLUSTER * SSM_STATE
SSM_CL_COLS = 2 * SSM_CL_STATE
SSM_CHUNK = 256


def _vmem_limit(*nbytes):
    want = int(sum(nbytes)) + 12 * 1024 * 1024
    return min(want, V7X_VMEM_BYTES - V7X_VMEM_RESERVED_BYTES)


def _nbytes(shape, dtype):
    return math.prod(shape) * jnp.dtype(dtype).itemsize


def _rms_rows(x, gain):
    ms = jnp.mean(x * x, axis=-1, keepdims=True)
    return x * lax.rsqrt(ms + EPS) * gain


def _norm_into(x_ref, gain_ref, h_ref, *, rows, row_chunk, h_offset):
    def body(c, carry):
        r = pl.multiple_of(c * row_chunk, row_chunk)
        xb = x_ref[pl.ds(r, row_chunk), :].astype(jnp.float32)
        h_ref[pl.ds(h_offset + r, row_chunk), :] = _rms_rows(xb, gain_ref[...]).astype(h_ref.dtype)
        return carry
    lax.fori_loop(0, rows // row_chunk, body, 0, unroll=4)


def _in_proj_kernel(x_ref, g_ref, w_ref, cg_ref, *refs, tm, tn, n_norm_tiles, n_parts, n_side):
    side_in, o_ref, side_out, h_ref = (refs[:n_side], refs[n_side], refs[n_side + 1:2 * n_side + 1],
                                       refs[2 * n_side + 1])
    j = pl.program_id(1)
    h_ref = h_ref.at[pl.program_id(2)]

    @pl.when(j == 0)
    def _():
        _norm_into(x_ref, g_ref, h_ref, rows=tm, row_chunk=16, h_offset=0)

    is_qk = j < n_norm_tiles
    part = tn // n_parts

    def tile(normed):
        for c0 in range(0, tn, part):
            acc = jnp.dot(h_ref[...], w_ref[:, c0:c0 + part], preferred_element_type=jnp.float32)
            for s0 in range(0, part, HEAD_DIM):
                raw = acc[:, s0:s0 + HEAD_DIM]
                sl = slice(c0 + s0, c0 + s0 + HEAD_DIM)
                out = _rms_rows(raw, cg_ref[:, sl]) if normed else raw
                o_ref[:, sl] = out.astype(o_ref.dtype)
        for src, dst in zip(side_in, side_out):
            if len(dst.shape) == 3:
                width = dst.shape[2]
                for t in range(dst.shape[0]):
                    dst[t] = src[:, t * width:(t + 1) * width].astype(dst.dtype)
            else:
                dst[...] = src[...].astype(dst.dtype)

    pl.when(is_qk)(functools.partial(tile, True))
    pl.when(jnp.logical_not(is_qk))(functools.partial(tile, False))


def _in_proj(x2d, gain, w_bf16, col_gain, side_f32, side_tile_widths, layer, *, tm=512, tn=1024,
             row_blocks=2):
    m, k = x2d.shape
    n = w_bf16.shape[1]
    n_norm_tiles = 2 * ATTN_WIDTH // tn
    n_col = n // tn
    n_steps = (m // tm) * n_col
    assert 2 * ATTN_WIDTH % tn == 0 and m % (tm * row_blocks) == 0 and n % tn == 0
    step = lambda i, j, r: (i * n_col + j) * row_blocks + r
    kern = functools.partial(_in_proj_kernel, tm=tm, tn=tn, n_norm_tiles=n_norm_tiles, n_parts=4,
                             n_side=len(side_f32))
    side_in_specs, side_out_specs, side_shapes, side_bytes = [], [], [], 0
    for a, width in zip(side_f32, side_tile_widths):
        _, rows, cols = a.shape
        tile = V7X_BF16_ROWS_PER_TILE
        slab = next(s for s in range(tile, rows + 1, tile)
                    if rows % s == 0 and rows // s <= n_steps)
        last = rows // slab - 1
        side_in_specs.append(pl.BlockSpec(
            (None, slab, cols),
            lambda i, j, r, last=last: (layer, jnp.minimum(step(i, j, r), last), 0)))
        if width is None:
            side_out_specs.append(pl.BlockSpec(
                (slab, cols), lambda i, j, r, last=last: (jnp.minimum(step(i, j, r), last), 0)))
            side_shapes.append(jax.ShapeDtypeStruct((rows, cols), jnp.bfloat16))
        else:
            side_out_specs.append(pl.BlockSpec(
                (cols // width, slab, width),
                lambda i, j, r, last=last: (0, jnp.minimum(step(i, j, r), last), 0)))
            side_shapes.append(jax.ShapeDtypeStruct((cols // width, rows, width), jnp.bfloat16))
        side_bytes += 2 * _nbytes((slab, cols), jnp.float32) + 2 * _nbytes((slab, cols), jnp.bfloat16)
    vmem = _vmem_limit(2 * _nbytes((tm, k), jnp.float32),
                       row_blocks * _nbytes((tm, k), jnp.bfloat16),
                       2 * _nbytes((k, tn), jnp.bfloat16), 2 * _nbytes((tm, tn), jnp.bfloat16),
                       _nbytes((tm, tn), jnp.float32), side_bytes)
    outs = pl.pallas_call(
        kern,
        out_shape=[jax.ShapeDtypeStruct((m, n), jnp.bfloat16)] + side_shapes,
        grid=(m // (tm * row_blocks), n_col, row_blocks),
        in_specs=[
            pl.BlockSpec((tm, k), lambda i, j, r: (
                i * row_blocks + jnp.where(j == 0, r, row_blocks - 1), 0)),
            pl.BlockSpec((1, k), lambda i, j, r: (0, 0)),
            pl.BlockSpec((k, tn), lambda i, j, r: (0, j)),
            pl.BlockSpec((1, tn), lambda i, j, r: (0, j)),
        ] + side_in_specs,
        out_specs=[pl.BlockSpec((tm, tn), lambda i, j, r: (i * row_blocks + r, j))] + side_out_specs,
        scratch_shapes=[pltpu.VMEM((row_blocks, tm, k), jnp.bfloat16)],
        compiler_params=pltpu.CompilerParams(
            dimension_semantics=("arbitrary", "arbitrary", "arbitrary"), vmem_limit_bytes=vmem),
        name="in_proj",
    )(x2d, gain, w_bf16, col_gain, *side_f32)
    return outs[0], outs[1:]


_UNSHIFTED_SOFTMAX_MAX_LOG2 = 60.0


def _qk_scores(q, k, row_offset):
    s = lax.dot_general(q, k, (((1,), (1,)), ((), ())), preferred_element_type=jnp.float32)
    if row_offset is not None:
        row = lax.broadcasted_iota(jnp.int32, s.shape, 0)
        col = lax.broadcasted_iota(jnp.int32, s.shape, 1)
        s = jnp.where(col <= row + row_offset, s, -jnp.inf)
    return s


def _lane_partial_sums(p):
    ps = p[:, 0:V7X_LANES]
    for c in range(1, p.shape[1] // V7X_LANES):
        ps = ps + p[:, c * V7X_LANES:(c + 1) * V7X_LANES]
    return ps


def _lane_totals(x):
    ones = jnp.ones((V7X_LANES, V7X_LANES), jnp.bfloat16)
    hi = x.astype(jnp.bfloat16)
    lo = (x - hi.astype(jnp.float32)).astype(jnp.bfloat16)
    return (jnp.dot(hi, ones, preferred_element_type=jnp.float32)
            + jnp.dot(lo, ones, preferred_element_type=jnp.float32))


def _attn_kernel(small_ref, lamv_ref, q_ref, k_ref, v_ref, sg_ref, o_ref,
                 m_ref, l_ref, acc_ref, p_ref, *, blk, nq, lam_init):
    unshifted = small_ref[0] == 1
    lamv = lamv_ref[...]
    lam = (jnp.exp(jnp.sum(lamv[0:1] * lamv[1:2], axis=-1, keepdims=True))
           - jnp.exp(jnp.sum(lamv[2:3] * lamv[3:4], axis=-1, keepdims=True)) + lam_init)
    out_gain = sg_ref[...] * (1.0 - lam_init)

    def head(mi):
        return slice(mi * HEAD_DIM, (mi + 1) * HEAD_DIM)

    half = blk // 2

    def stacked(hf, mi):
        return slice((2 * hf + mi) * half, (2 * hf + mi + 1) * half)

    def q_block(qi, carry):
        q0 = pl.multiple_of(qi * blk, blk)

        def tile(hfs, start, width, mask_offset, shifted, first):
            ks = pl.ds(start, width)
            n_rows = len(hfs) * half
            qs = pl.ds(q0 + hfs[0] * half, n_rows)
            for mi in range(2):
                s = _qk_scores(q_ref[qs, head(mi)], k_ref[ks, head(mi)], mask_offset)
                if not shifted:
                    p = jnp.exp2(s)
                    ps = _lane_partial_sums(p)
                for n, hf in enumerate(hfs):
                    st = stacked(hf, mi)
                    local = slice(n * half, (n + 1) * half)
                    if shifted:
                        m_cur = jnp.max(s[local], axis=-1, keepdims=True)
                        m_new = m_cur if first else jnp.maximum(m_ref[st, :], m_cur)
                        ph = jnp.exp2(s[local] - m_new)
                        l_cur = jnp.sum(ph, axis=-1, keepdims=True)
                        if first:
                            l_ref[st, :] = jnp.zeros((half, V7X_LANES), jnp.float32)
                            l_ref[st, 0:1] = l_cur
                        else:
                            alpha = jnp.exp2(m_ref[st, :] - m_new)
                            l_ref[st, 0:1] = alpha * l_ref[st, 0:1] + l_cur
                            acc_ref[st, :] = alpha * acc_ref[st, :]
                        m_ref[st, :] = m_new
                        p_ref[st, 0:width] = ph.astype(p_ref.dtype)
                    else:
                        if first:
                            l_ref[st, :] = ps[local]
                        else:
                            l_ref[st, :] += ps[local]
                        p_ref[st, 0:width] = p[local].astype(p_ref.dtype)
            lo, hi = 2 * hfs[0] * half, 2 * (hfs[-1] + 1) * half
            for rows in (slice(lo, (lo + hi) // 2), slice((lo + hi) // 2, hi)):
                pv = jnp.dot(p_ref[rows, 0:width], v_ref[ks, :], preferred_element_type=jnp.float32)
                if first:
                    acc_ref[rows, :] = pv
                else:
                    acc_ref[rows, :] += pv

        def causal_tiles(shifted):
            tile((0,), q0, half, 0, shifted, True)
            tile((1,), q0, blk, half, shifted, True)

            def body(j, c):
                tile((0, 1), pl.multiple_of(j * blk, blk), blk, None, shifted, False)
                return c
            lax.fori_loop(0, qi, body, 0)

        @pl.when(unshifted)
        def _():
            causal_tiles(False)

        @pl.when(jnp.logical_not(unshifted))
        def _():
            causal_tiles(True)

        for hf in range(2):
            r1 = 1.0 / _lane_totals(l_ref[stacked(hf, 0), :])
            r2 = lam / _lane_totals(l_ref[stacked(hf, 1), :])
            o = (acc_ref[stacked(hf, 0), :] * jnp.concatenate([r1, r1], axis=1)
                 - acc_ref[stacked(hf, 1), :] * jnp.concatenate([r2, r2], axis=1))
            o_ref[pl.ds(q0 + hf * half, half), :] = _rms_rows(o, out_gain).astype(o_ref.dtype)
        return carry

    lax.fori_loop(0, nq, q_block, 0)


def _attention(proj, small_logits, lamv, sub_gain, *, batch, seq, lam_init, blk=1024):
    nq = seq // blk
    hw = 2 * HEAD_DIM
    kern = functools.partial(_attn_kernel, blk=blk, nq=nq, lam_init=lam_init)
    vmem = _vmem_limit(8 * _nbytes((seq, hw), jnp.bfloat16),
                       _nbytes((2 * blk, hw), jnp.float32), _nbytes((2 * blk, blk), jnp.bfloat16),
                       4 * _nbytes((blk, V7X_LANES), jnp.float32),
                       4 * _nbytes((blk, blk), jnp.float32))
    return pl.pallas_call(
        kern,
        out_shape=jax.ShapeDtypeStruct((batch * seq, D_MODEL), jnp.bfloat16),
        grid=(batch, ATTN_HEADS),
        in_specs=[
            pl.BlockSpec(memory_space=pltpu.SMEM),
            pl.BlockSpec((4, HEAD_DIM), lambda b, h: (0, 0)),
            pl.BlockSpec((seq, hw), lambda b, h: (b, h)),
            pl.BlockSpec((seq, hw), lambda b, h: (b, ATTN_HEADS + h)),
            pl.BlockSpec((seq, hw), lambda b, h: (b, 2 * ATTN_HEADS + h)),
            pl.BlockSpec((1, hw), lambda b, h: (0, 0)),
        ],
        out_specs=pl.BlockSpec((seq, hw), lambda b, h: (b, h)),
        scratch_shapes=[
            pltpu.VMEM((2 * blk, 1), jnp.float32),
            pltpu.VMEM((2 * blk, V7X_LANES), jnp.float32),
            pltpu.VMEM((2 * blk, hw), jnp.float32),
            pltpu.VMEM((2 * blk, blk), jnp.bfloat16),
        ],
        compiler_params=pltpu.CompilerParams(
            dimension_semantics=("arbitrary", "arbitrary"), vmem_limit_bytes=vmem),
        name="diff_attn",
    )(small_logits, lamv, proj, proj, proj, sub_gain)


_SSM_FACTORISED_MAX_LOG = 60.0


def _cmul(ar, ai, br, bi):
    return ar * br - ai * bi, ar * bi + ai * br


def _ssm_kernel(flag_ref, u_ref, mixed_ref, bmat_ref, cmat_ref, pinv_ref, pfwd_ref, pstep_ref, d_ref,
                wglu_ref, bglu_ref, og_ref, o_ref, h_ref, y_ref, bu_ref, hs_ref, *, chunk):
    del mixed_ref
    c = pl.program_id(1)

    @pl.when(c == 0)
    def _():
        h_ref[...] = jnp.zeros(h_ref.shape, jnp.float32)

    n = SSM_CL_STATE
    cluster_batch = SSM_CLUSTERS
    factorised = flag_ref[0] == 1

    def lanes_of(cl):
        return slice(cl * V7X_LANES, (cl + 1) * V7X_LANES)

    def finish():
        y = y_ref[...] + d_ref[...] * u_ref[...].astype(jnp.float32)
        y = jax.nn.gelu(y, approximate=True)
        z = jnp.dot(y.astype(jnp.bfloat16), wglu_ref[...], preferred_element_type=jnp.float32)
        z = z + bglu_ref[...]
        y = y * (1.0 / (1.0 + jnp.exp(-z)))
        o_ref[...] = _rms_rows(y, og_ref[...]).astype(o_ref.dtype)

    @pl.when(factorised)
    def _():
        row = lax.broadcasted_iota(jnp.int32, (chunk, chunk), 0)
        col = lax.broadcasted_iota(jnp.int32, (chunk, chunk), 1)
        tri = (col <= row).astype(jnp.bfloat16)
        for c_lo in range(0, SSM_CLUSTERS, cluster_batch):
            clusters = range(c_lo, c_lo + cluster_batch)
            bus = {cl: jnp.dot(u_ref[:, lanes_of(cl)], bmat_ref[cl],
                               preferred_element_type=jnp.float32) for cl in clusters}
            xs = {}
            for cl in clusters:
                bu = bus[cl]
                xr, xi = _cmul(pinv_ref[cl, :, :n], pinv_ref[cl, :, n:], bu[:, :n], bu[:, n:])
                xs[cl] = jnp.concatenate([xr, xi], axis=1).astype(jnp.bfloat16)
            css = {cl: jnp.dot(tri, xs[cl], preferred_element_type=jnp.float32) for cl in clusters}
            hs = {}
            for cl in clusters:
                cs = css[cl]
                cr, ci = _cmul(pstep_ref[1, cl:cl + 1, :n], pstep_ref[1, cl:cl + 1, n:],
                               h_ref[cl:cl + 1, :n], h_ref[cl:cl + 1, n:])
                hr, hi = _cmul(pfwd_ref[cl, :, :n], pfwd_ref[cl, :, n:],
                               cs[:, :n] + cr, cs[:, n:] + ci)
                h_ref[cl:cl + 1, :n] = hr[chunk - 1:chunk, :]
                h_ref[cl:cl + 1, n:] = hi[chunk - 1:chunk, :]
                hs[cl] = jnp.concatenate([hr, hi], axis=1).astype(jnp.bfloat16)
            for cl in clusters:
                y_ref[:, lanes_of(cl)] = jnp.dot(hs[cl], cmat_ref[cl],
                                                 preferred_element_type=jnp.float32)
        finish()

    @pl.when(jnp.logical_not(factorised))
    def _():
        for cl in range(SSM_CLUSTERS):
            bu_ref[...] = jnp.dot(u_ref[:, lanes_of(cl)], bmat_ref[cl],
                                  preferred_element_type=jnp.float32)
            lr, li = pstep_ref[0, cl:cl + 1, :n], pstep_ref[0, cl:cl + 1, n:]

            def step(t, carry):
                hr, hi = carry
                ar, ai = _cmul(lr, li, hr, hi)
                hr = ar + bu_ref[pl.ds(t, 1), :n]
                hi = ai + bu_ref[pl.ds(t, 1), n:]
                hs_ref[pl.ds(t, 1), :n] = hr
                hs_ref[pl.ds(t, 1), n:] = hi
                return hr, hi

            hr, hi = lax.fori_loop(0, chunk, step, (h_ref[cl:cl + 1, :n], h_ref[cl:cl + 1, n:]))
            h_ref[cl:cl + 1, :n] = hr
            h_ref[cl:cl + 1, n:] = hi
            y_ref[:, lanes_of(cl)] = jnp.dot(hs_ref[...].astype(jnp.bfloat16), cmat_ref[cl],
                                             preferred_element_type=jnp.float32)
        finish()


def _ssm_tables(log_dt, a_re, a_im, b_re, b_im, c_re, c_im, chunk):
    f32 = jnp.float32
    n, c = SSM_STATE, SSM_GROUP
    dt = jnp.exp(log_dt.astype(f32))[:, None]
    ar, ai = a_re.astype(f32), a_im.astype(f32)
    mag = jnp.exp(ar * dt)
    lb_re = mag * jnp.cos(ai * dt)
    lb_im = mag * jnp.sin(ai * dt)
    pr, pi_ = lb_re - 1.0, lb_im
    den = ar * ar + ai * ai
    z_re = ((pr * ar + pi_ * ai) / den)[..., None]
    z_im = ((pi_ * ar - pr * ai) / den)[..., None]
    br, bi = b_re.astype(f32), b_im.astype(f32)
    bb_re = z_re * br - z_im * bi
    bb_im = z_re * bi + z_im * br

    def pair_cols(re, im):
        return jnp.concatenate([re, im], axis=-1)

    log_re = (ar * dt).reshape(SSM_CLUSTERS, 1, SSM_CLUSTER * n)
    log_im = (ai * dt).reshape(SSM_CLUSTERS, 1, SSM_CLUSTER * n)

    def table(k):
        kk = k.astype(f32)[None, :, None]
        pm = jnp.exp(kk * log_re)
        ph = kk * log_im
        return pair_cols(pm * jnp.cos(ph), pm * jnp.sin(ph))

    c0 = chunk // 2
    t = jnp.arange(chunk)
    pfwd = table(t - c0)
    pinv = table(c0 - t)
    pstep = table(jnp.array([1, c0 + 1])).transpose(1, 0, 2)
    factorised = ((c0 + 1) * jnp.max(jnp.abs(log_re)) <= _SSM_FACTORISED_MAX_LOG)

    eye = jnp.eye(SSM_CLUSTER, dtype=f32)
    def bpart(bb):
        b4 = bb.reshape(SSM_CLUSTERS, SSM_CLUSTER, n, c)
        return jnp.einsum('kgnc,gh->kgchn', b4, eye).reshape(
            SSM_CLUSTERS, V7X_LANES, SSM_CLUSTER * n)
    bmat = pair_cols(bpart(bb_re), bpart(bb_im)).astype(jnp.bfloat16)
    def cpart(cc):
        c4 = cc.astype(f32).reshape(SSM_CLUSTERS, SSM_CLUSTER, c, n)
        return jnp.einsum('kgcn,gh->khcgn', c4, eye).reshape(
            SSM_CLUSTERS, V7X_LANES, SSM_CLUSTER * n)
    cmat = pair_cols(cpart(c_re), -cpart(c_im)).transpose(0, 2, 1).astype(jnp.bfloat16)
    return factorised.astype(jnp.int32).reshape(1), bmat, cmat, pinv, pfwd, pstep


def _ssm(proj, mixed, tables, d_skip, w_glu, b_glu, out_gain, *, batch, seq, chunk=SSM_CHUNK):
    factorised, bmat, cmat, pinv, pfwd, pstep = tables
    nc = seq // chunk
    u_col = 3 * ATTN_WIDTH // SSM_WIDTH
    o_col = ATTN_WIDTH // SSM_WIDTH
    kern = functools.partial(_ssm_kernel, chunk=chunk)
    const3 = lambda b, c: (0, 0, 0)
    const2 = lambda b, c: (0, 0)
    vmem = _vmem_limit(2 * (bmat.nbytes + cmat.nbytes + pinv.nbytes + pfwd.nbytes + pstep.nbytes),
                       4 * _nbytes((SSM_WIDTH, SSM_WIDTH), jnp.bfloat16),
                       10 * _nbytes((chunk, SSM_WIDTH), jnp.float32))
    return pl.pallas_call(
        kern,
        out_shape=jax.ShapeDtypeStruct(mixed.shape, mixed.dtype),
        grid=(batch, nc),
        in_specs=[
            pl.BlockSpec(memory_space=pltpu.SMEM),
            pl.BlockSpec((chunk, SSM_WIDTH), lambda b, c: (b * nc + c, u_col)),
            pl.BlockSpec(memory_space=pl.ANY),
            pl.BlockSpec(bmat.shape, const3),
            pl.BlockSpec(cmat.shape, const3),
            pl.BlockSpec(pinv.shape, const3),
            pl.BlockSpec(pfwd.shape, const3),
            pl.BlockSpec(pstep.shape, const3),
            pl.BlockSpec((1, SSM_WIDTH), const2),
            pl.BlockSpec((SSM_WIDTH, SSM_WIDTH), const2),
            pl.BlockSpec((1, SSM_WIDTH), const2),
            pl.BlockSpec((1, SSM_WIDTH), const2),
        ],
        out_specs=pl.BlockSpec((chunk, SSM_WIDTH), lambda b, c: (b * nc + c, o_col)),
        scratch_shapes=[
            pltpu.VMEM((SSM_CLUSTERS, SSM_CL_COLS), jnp.float32),
            pltpu.VMEM((chunk, SSM_WIDTH), jnp.float32),
            pltpu.VMEM((chunk, SSM_CL_COLS), jnp.float32),
            pltpu.VMEM((chunk, SSM_CL_COLS), jnp.float32),
        ],
        input_output_aliases={2: 0},
        compiler_params=pltpu.CompilerParams(
            dimension_semantics=("arbitrary", "arbitrary"), vmem_limit_bytes=vmem),
        name="s5_ssm",
    )(factorised, proj, mixed, bmat, cmat, pinv, pfwd, pstep, d_skip, w_glu, b_glu, out_gain)


def _res_matmul_kernel(a_ref, w_ref, r_ref, o_ref, *maybe_bf16_ref):
    out = r_ref[...] + jnp.dot(a_ref[...], w_ref[...], preferred_element_type=jnp.float32)
    o_ref[...] = out
    for ob_ref in maybe_bf16_ref:
        ob_ref[...] = out.astype(ob_ref.dtype)


def _res_matmul(a, w, res, *, tm, tn, name, with_bf16_copy=False):
    m, k = a.shape
    if w.ndim == 3:
        assert w.shape[1:] == (k, tn)
        n = w.shape[0] * tn
        w_spec = pl.BlockSpec((None, k, tn), lambda i, j: (j, 0, 0))
    else:
        n = w.shape[1]
        w_spec = pl.BlockSpec((k, tn), lambda i, j: (0, j))
    assert m % tm == 0 and n % tn == 0
    vmem = _vmem_limit(2 * _nbytes((tm, k), a.dtype), 2 * _nbytes((k, tn), w.dtype),
                       6 * _nbytes((tm, tn), jnp.float32))
    tile = pl.BlockSpec((tm, tn), lambda i, j: (i, j))
    out_shape = [jax.ShapeDtypeStruct((m, n), jnp.float32)]
    if with_bf16_copy:
        out_shape.append(jax.ShapeDtypeStruct((m, n), jnp.bfloat16))
    outs = pl.pallas_call(
        _res_matmul_kernel,
        out_shape=out_shape,
        grid=(m // tm, n // tn),
        in_specs=[
            pl.BlockSpec((tm, k), lambda i, j: (i, 0)),
            w_spec,
            tile,
        ],
        out_specs=[tile] * len(out_shape),
        compiler_params=pltpu.CompilerParams(
            dimension_semantics=("parallel", "parallel"), vmem_limit_bytes=vmem),
        name=name,
    )(a, w, res)
    return outs if with_bf16_copy else outs[0]


def _ffn_up_kernel(x_ref, halo_ref, g_ref, wg_ref, wu_ref, cw_ref, cb_ref, o_ref, h_ref,
                   *, tm, blocks_per_seq):
    i = pl.program_id(0)
    j = pl.program_id(1)
    halo = V7X_BF16_ROWS_PER_TILE

    @pl.when(j == 0)
    def _():
        hh = _rms_rows(halo_ref[...].astype(jnp.float32), g_ref[...])
        hh = jnp.where(i % blocks_per_seq == 0, 0.0, hh)
        h_ref[0:halo, :] = hh.astype(h_ref.dtype)
        _norm_into(x_ref, g_ref, h_ref, rows=tm, row_chunk=16, h_offset=halo)

    g_all = jnp.dot(h_ref[...], wg_ref[...], preferred_element_type=jnp.float32)
    up = jnp.dot(h_ref[halo:, :], wu_ref[...], preferred_element_type=jnp.float32)
    rows = 128
    for r0 in range(0, tm, rows):
        for c0 in range(0, o_ref.shape[1], V7X_LANES):
            cs = slice(c0, c0 + V7X_LANES)
            g = g_all[r0:r0 + halo + rows, cs]
            t = cw_ref[1:2, cs] * g + pltpu.roll(cw_ref[0:1, cs] * g, 1, 0)
            gc = (cb_ref[:, cs] + cw_ref[2:3, cs] * g + pltpu.roll(t, 1, 0))[halo:, :]
            act = gc * (1.0 / (1.0 + jnp.exp(-gc))) * up[r0:r0 + rows, cs]
            o_ref[r0:r0 + rows, cs] = act.astype(o_ref.dtype)


def _ffn_up(x1, gain, wg, wu, conv_w, conv_b, *, seq, tm=1024, tf=F_TILE):
    m, k = x1.shape
    f = wg.shape[1]
    halo = V7X_BF16_ROWS_PER_TILE
    assert m % tm == 0 and seq % tm == 0
    kern = functools.partial(_ffn_up_kernel, tm=tm, blocks_per_seq=seq // tm)
    vmem = _vmem_limit(2 * _nbytes((tm, k), x1.dtype), _nbytes((tm + halo, k), jnp.bfloat16),
                       4 * _nbytes((k, tf), jnp.bfloat16), 2 * _nbytes((tm, tf), jnp.bfloat16),
                       3 * _nbytes((tm + halo, tf), jnp.float32))
    return pl.pallas_call(
        kern,
        out_shape=jax.ShapeDtypeStruct((m, f), jnp.bfloat16),
        grid=(m // tm, pl.cdiv(f, tf)),
        in_specs=[
            pl.BlockSpec((tm, k), lambda i, j: (i, 0)),
            pl.BlockSpec((halo, k), lambda i, j: (jnp.maximum(i * (tm // halo) - 1, 0), 0)),
            pl.BlockSpec((1, k), lambda i, j: (0, 0)),
            pl.BlockSpec((k, tf), lambda i, j: (0, j)),
            pl.BlockSpec((k, tf), lambda i, j: (0, j)),
            pl.BlockSpec((CONV_WIDTH, tf), lambda i, j: (0, j)),
            pl.BlockSpec((1, tf), lambda i, j: (0, j)),
        ],
        out_specs=pl.BlockSpec((tm, tf), lambda i, j: (i, j)),
        scratch_shapes=[pltpu.VMEM((tm + halo, k), jnp.bfloat16)],
        compiler_params=pltpu.CompilerParams(
            dimension_semantics=("arbitrary", "arbitrary"), vmem_limit_bytes=vmem),
        name="ffn_up",
    )(x1, x1, gain, wg, wu, conv_w, conv_b)


def kernel(x, attn_norm, w_in, q_gain, k_gain, lam_q1, lam_k1, lam_q2, lam_k2, sub_gain, ssm_log_dt, ssm_a_re, ssm_a_im, ssm_b_re, ssm_b_im, ssm_c_re, ssm_c_im, ssm_d, ssm_w_glu, ssm_b_glu, ssm_out_gain, w_out, ffn_norm, w_gate, w_up, conv_w, conv_b, w_down):
    b, s, d = x.shape
    depth = w_in.shape[0]
    bf16, f32 = jnp.bfloat16, jnp.float32
    xf = x.reshape(b * s, d)
    for l in range(depth):
        lam_init = 0.8 - 0.6 * math.exp(-0.3 * l)
        q_scale = HEAD_DIM ** -0.5 * math.log2(math.e)
        qg, kg = q_gain[l].astype(f32), k_gain[l].astype(f32)
        col_gain = jnp.concatenate([
            jnp.tile(qg * q_scale, ATTN_WIDTH // HEAD_DIM),
            jnp.tile(kg, ATTN_WIDTH // HEAD_DIM),
            jnp.ones((IN_WIDTH - 2 * ATTN_WIDTH,), f32)])[None, :]
        proj, (wo_bf, wg_bf, wu_bf, wd_bf) = _in_proj(
            xf, attn_norm[l].astype(f32)[None, :], w_in[l].astype(bf16), col_gain,
            (w_out, w_gate, w_up, w_down), (None, None, None, _DOWN_TILE), l)

        logit_bound = 1.01 * HEAD_DIM * q_scale * jnp.max(jnp.abs(qg)) * jnp.max(jnp.abs(kg))
        small_logits = (logit_bound <= _UNSHIFTED_SOFTMAX_MAX_LOG2).astype(jnp.int32).reshape(1)
        lamv = jnp.stack([lam_q1[l], lam_k1[l], lam_q2[l], lam_k2[l]]).astype(f32)
        mixed = _attention(proj, small_logits, lamv, sub_gain[l].astype(f32)[None, :],
                           batch=b, seq=s, lam_init=lam_init)

        tables = _ssm_tables(ssm_log_dt[l], ssm_a_re[l], ssm_a_im[l], ssm_b_re[l], ssm_b_im[l],
                             ssm_c_re[l], ssm_c_im[l], SSM_CHUNK)
        mixed = _ssm(proj, mixed, tables, ssm_d[l].astype(f32).reshape(1, SSM_WIDTH),
                     ssm_w_glu[l].astype(bf16), ssm_b_glu[l].astype(f32)[None, :],
                     ssm_out_gain[l].astype(f32)[None, :], batch=b, seq=s)

        x1, x1_bf = _res_matmul(mixed, wo_bf, xf, tm=1024, tn=512, name="out_proj",
                                with_bf16_copy=True)

        act = _ffn_up(x1_bf, ffn_norm[l].astype(f32)[None, :], wg_bf, wu_bf, conv_w[l].astype(f32),
                      conv_b[l].astype(f32)[None, :], seq=s)
        xf = _res_matmul(act, wd_bf, x1, tm=512, tn=_DOWN_TILE, name="ffn_down")
    return xf.reshape(b, s, d)
```

```python
import functools
import math

import jax
import jax.numpy as jnp
from jax import lax
from jax.experimental import pallas as pl
from jax.experimental.pallas import tpu as pltpu

D_MODEL = 4096
ATTN_HEADS = 12
HEAD_DIM = 128
ATTN_WIDTH = ATTN_HEADS * 2 * HEAD_DIM
SSM_WIDTH = D_MODEL - ATTN_WIDTH
SSM_GROUP = 16
SSM_GROUPS = SSM_WIDTH // SSM_GROUP
SSM_STATE = 64
IN_WIDTH = 3 * ATTN_WIDTH + SSM_WIDTH
D_FF = 11008
CONV_WIDTH = 3
EPS = 1e-6

V7X_LANES = 128
V7X_SUBLANES = 8
V7X_BF16_ROWS_PER_TILE = 2 * V7X_SUBLANES
V7X_VMEM_BYTES = 64 * 1024 * 1024
V7X_VMEM_RESERVED_BYTES = 6 * 1024 * 1024

F_TILE = 512
SSM_CLUSTER = V7X_LANES // SSM_GROUP
SSM_CLUSTERS = SSM_GROUPS // SSM_CLUSTER
SSM_CL_STATE = SSM_CLUSTER * SSM_STATE
SSM_CL_COLS = 2 * SSM_CL_STATE
SSM_CHUNK = 256


def _vmem_limit(*nbytes):
    want = int(sum(nbytes)) + 12 * 1024 * 1024
    return min(want, V7X_VMEM_BYTES - V7X_VMEM_RESERVED_BYTES)


def _nbytes(shape, dtype):
    return math.prod(shape) * jnp.dtype(dtype).itemsize


def _rms_rows(x, gain):
    ms = jnp.mean(x * x, axis=-1, keepdims=True)
    return x * lax.rsqrt(ms + EPS) * gain


def _norm_into(x_ref, gain_ref, h_ref, *, rows, row_chunk, h_offset):
    def body(c, carry):
        r = pl.multiple_of(c * row_chunk, row_chunk)
        xb = x_ref[pl.ds(r, row_chunk), :].astype(jnp.float32)
        h_ref[pl.ds(h_offset + r, row_chunk), :] = _rms_rows(xb, gain_ref[...]).astype(h_ref.dtype)
        return carry
    lax.fori_loop(0, rows // row_chunk, body, 0, unroll=4)


def _in_proj_kernel(x_ref, g_ref, w_ref, cg_ref, *refs, tm, tn, n_norm_tiles, n_parts, n_side):
    side_in, o_ref, side_out, h_ref = (refs[:n_side], refs[n_side], refs[n_side + 1:2 * n_side + 1],
                                       refs[2 * n_side + 1])
    j = pl.program_id(1)
    h_ref = h_ref.at[pl.program_id(2)]

    @pl.when(j == 0)
    def _():
        _norm_into(x_ref, g_ref, h_ref, rows=tm, row_chunk=16, h_offset=0)

    is_qk = j < n_norm_tiles
    part = tn // n_parts

    def tile(normed):
        for c0 in range(0, tn, part):
            acc = jnp.dot(h_ref[...], w_ref[:, c0:c0 + part], preferred_element_type=jnp.float32)
            for s0 in range(0, part, HEAD_DIM):
                raw = acc[:, s0:s0 + HEAD_DIM]
                sl = slice(c0 + s0, c0 + s0 + HEAD_DIM)
                out = _rms_rows(raw, cg_ref[:, sl]) if normed else raw
                o_ref[:, sl] = out.astype(o_ref.dtype)
        for src, dst in zip(side_in, side_out):
            dst[...] = src[...].astype(dst.dtype)

    pl.when(is_qk)(functools.partial(tile, True))
    pl.when(jnp.logical_not(is_qk))(functools.partial(tile, False))


def _in_proj(x2d, gain, w_bf16, col_gain, side_f32, layer, *, tm=512, tn=1024, row_blocks=2):
    m, k = x2d.shape
    n = w_bf16.shape[1]
    n_norm_tiles = 2 * ATTN_WIDTH // tn
    n_col = n // tn
    n_steps = (m // tm) * n_col
    assert 2 * ATTN_WIDTH % tn == 0 and m % (tm * row_blocks) == 0 and n % tn == 0
    step = lambda i, j, r: (i * n_col + j) * row_blocks + r
    kern = functools.partial(_in_proj_kernel, tm=tm, tn=tn, n_norm_tiles=n_norm_tiles, n_parts=4,
                             n_side=len(side_f32))
    side_in_specs, side_out_specs, side_shapes, side_bytes = [], [], [], 0
    for a in side_f32:
        _, rows, cols = a.shape
        tile = V7X_BF16_ROWS_PER_TILE
        slab = next(s for s in range(tile, rows + 1, tile)
                    if rows % s == 0 and rows // s <= n_steps)
        last = rows // slab - 1
        side_in_specs.append(pl.BlockSpec(
            (None, slab, cols),
            lambda i, j, r, last=last: (layer, jnp.minimum(step(i, j, r), last), 0)))
        side_out_specs.append(pl.BlockSpec(
            (slab, cols), lambda i, j, r, last=last: (jnp.minimum(step(i, j, r), last), 0)))
        side_shapes.append(jax.ShapeDtypeStruct((rows, cols), jnp.bfloat16))
        side_bytes += 2 * _nbytes((slab, cols), jnp.float32) + 2 * _nbytes((slab, cols), jnp.bfloat16)
    vmem = _vmem_limit(2 * _nbytes((tm, k), jnp.float32),
                       row_blocks * _nbytes((tm, k), jnp.bfloat16),
                       2 * _nbytes((k, tn), jnp.bfloat16), 2 * _nbytes((tm, tn), jnp.bfloat16),
                       _nbytes((tm, tn), jnp.float32), side_bytes)
    outs = pl.pallas_call(
        kern,
        out_shape=[jax.ShapeDtypeStruct((m, n), jnp.bfloat16)] + side_shapes,
        grid=(m // (tm * row_blocks), n_col, row_blocks),
        in_specs=[
            pl.BlockSpec((tm, k), lambda i, j, r: (
                i * row_blocks + jnp.where(j == 0, r, row_blocks - 1), 0)),
            pl.BlockSpec((1, k), lambda i, j, r: (0, 0)),
            pl.BlockSpec((k, tn), lambda i, j, r: (0, j)),
            pl.BlockSpec((1, tn), lambda i, j, r: (0, j)),
        ] + side_in_specs,
        out_specs=[pl.BlockSpec((tm, tn), lambda i, j, r: (i * row_blocks + r, j))] + side_out_specs,
        scratch_shapes=[pltpu.VMEM((row_blocks, tm, k), jnp.bfloat16)],
        compiler_params=pltpu.CompilerParams(
            dimension_semantics=("arbitrary", "arbitrary", "arbitrary"), vmem_limit_bytes=vmem),
        name="in_proj",
    )(x2d, gain, w_bf16, col_gain, *side_f32)
    return outs[0], outs[1:]


_UNSHIFTED_SOFTMAX_MAX_LOG2 = 60.0


def _qk_scores(q, k, row_offset):
    s = lax.dot_general(q, k, (((1,), (1,)), ((), ())), preferred_element_type=jnp.float32)
    if row_offset is not None:
        row = lax.broadcasted_iota(jnp.int32, s.shape, 0)
        col = lax.broadcasted_iota(jnp.int32, s.shape, 1)
        s = jnp.where(col <= row + row_offset, s, -jnp.inf)
    return s


def _lane_partial_sums(p):
    ps = p[:, 0:V7X_LANES]
    for c in range(1, p.shape[1] // V7X_LANES):
        ps = ps + p[:, c * V7X_LANES:(c + 1) * V7X_LANES]
    return ps


def _lane_totals(x):
    ones = jnp.ones((V7X_LANES, V7X_LANES), jnp.bfloat16)
    hi = x.astype(jnp.bfloat16)
    lo = (x - hi.astype(jnp.float32)).astype(jnp.bfloat16)
    return (jnp.dot(hi, ones, preferred_element_type=jnp.float32)
            + jnp.dot(lo, ones, preferred_element_type=jnp.float32))


def _attn_kernel(small_ref, lamv_ref, q_ref, k_ref, v_ref, sg_ref, o_ref,
                 m_ref, l_ref, acc_ref, p_ref, *, blk, nq, lam_init):
    unshifted = small_ref[0] == 1
    lamv = lamv_ref[...]
    lam = (jnp.exp(jnp.sum(lamv[0:1] * lamv[1:2], axis=-1, keepdims=True))
           - jnp.exp(jnp.sum(lamv[2:3] * lamv[3:4], axis=-1, keepdims=True)) + lam_init)
    out_gain = sg_ref[...] * (1.0 - lam_init)

    def head(mi):
        return slice(mi * HEAD_DIM, (mi + 1) * HEAD_DIM)

    half = blk // 2

    def stacked(hf, mi):
        return slice((2 * hf + mi) * half, (2 * hf + mi + 1) * half)

    def q_block(qi, carry):
        q0 = pl.multiple_of(qi * blk, blk)

        def tile(hfs, start, width, mask_offset, shifted, first):
            ks = pl.ds(start, width)
            n_rows = len(hfs) * half
            qs = pl.ds(q0 + hfs[0] * half, n_rows)
            for mi in range(2):
                s = _qk_scores(q_ref[qs, head(mi)], k_ref[ks, head(mi)], mask_offset)
                if not shifted:
                    p = jnp.exp2(s)
                    ps = _lane_partial_sums(p)
                for n, hf in enumerate(hfs):
                    st = stacked(hf, mi)
                    local = slice(n * half, (n + 1) * half)
                    if shifted:
                        m_cur = jnp.max(s[local], axis=-1, keepdims=True)
                        m_new = m_cur if first else jnp.maximum(m_ref[st, :], m_cur)
                        ph = jnp.exp2(s[local] - m_new)
                        l_cur = jnp.sum(ph, axis=-1, keepdims=True)
                        if first:
                            l_ref[st, :] = jnp.zeros((half, V7X_LANES), jnp.float32)
                            l_ref[st, 0:1] = l_cur
                        else:
                            alpha = jnp.exp2(m_ref[st, :] - m_new)
                            l_ref[st, 0:1] = alpha * l_ref[st, 0:1] + l_cur
                            acc_ref[st, :] = alpha * acc_ref[st, :]
                        m_ref[st, :] = m_new
                        p_ref[st, 0:width] = ph.astype(p_ref.dtype)
                    else:
                        if first:
                            l_ref[st, :] = ps[local]
                        else:
                            l_ref[st, :] += ps[local]
                        p_ref[st, 0:width] = p[local].astype(p_ref.dtype)
            lo, hi = 2 * hfs[0] * half, 2 * (hfs[-1] + 1) * half
            for rows in (slice(lo, (lo + hi) // 2), slice((lo + hi) // 2, hi)):
                pv = jnp.dot(p_ref[rows, 0:width], v_ref[ks, :], preferred_element_type=jnp.float32)
                if first:
                    acc_ref[rows, :] = pv
                else:
                    acc_ref[rows, :] += pv

        def causal_tiles(shifted):
            tile((0,), q0, half, 0, shifted, True)
            tile((1,), q0, blk, half, shifted, True)

            def body(j, c):
                tile((0, 1), pl.multiple_of(j * blk, blk), blk, None, shifted, False)
                return c
            lax.fori_loop(0, qi, body, 0)

        @pl.when(unshifted)
        def _():
            causal_tiles(False)

        @pl.when(jnp.logical_not(unshifted))
        def _():
            causal_tiles(True)

        for hf in range(2):
            r1 = 1.0 / _lane_totals(l_ref[stacked(hf, 0), :])
            r2 = lam / _lane_totals(l_ref[stacked(hf, 1), :])
            o = (acc_ref[stacked(hf, 0), :] * jnp.concatenate([r1, r1], axis=1)
                 - acc_ref[stacked(hf, 1), :] * jnp.concatenate([r2, r2], axis=1))
            o_ref[pl.ds(q0 + hf * half, half), :] = _rms_rows(o, out_gain).astype(o_ref.dtype)
        return carry

    lax.fori_loop(0, nq, q_block, 0)


def _attention(proj, small_logits, lamv, sub_gain, *, batch, seq, lam_init, blk=1024):
    nq = seq // blk
    hw = 2 * HEAD_DIM
    kern = functools.partial(_attn_kernel, blk=blk, nq=nq, lam_init=lam_init)
    vmem = _vmem_limit(8 * _nbytes((seq, hw), jnp.bfloat16),
                       _nbytes((2 * blk, hw), jnp.float32), _nbytes((2 * blk, blk), jnp.bfloat16),
                       4 * _nbytes((blk, V7X_LANES), jnp.float32),
                       4 * _nbytes((blk, blk), jnp.float32))
    return pl.pallas_call(
        kern,
        out_shape=jax.ShapeDtypeStruct((batch * seq, D_MODEL), jnp.bfloat16),
        grid=(batch, ATTN_HEADS),
        in_specs=[
            pl.BlockSpec(memory_space=pltpu.SMEM),
            pl.BlockSpec((4, HEAD_DIM), lambda b, h: (0, 0)),
            pl.BlockSpec((seq, hw), lambda b, h: (b, h)),
            pl.BlockSpec((seq, hw), lambda b, h: (b, ATTN_HEADS + h)),
            pl.BlockSpec((seq, hw), lambda b, h: (b, 2 * ATTN_HEADS + h)),
            pl.BlockSpec((1, hw), lambda b, h: (0, 0)),
        ],
        out_specs=pl.BlockSpec((seq, hw), lambda b, h: (b, h)),
        scratch_shapes=[
            pltpu.VMEM((2 * blk, 1), jnp.float32),
            pltpu.VMEM((2 * blk, V7X_LANES), jnp.float32),
            pltpu.VMEM((2 * blk, hw), jnp.float32),
            pltpu.VMEM((2 * blk, blk), jnp.bfloat16),
        ],
        compiler_params=pltpu.CompilerParams(
            dimension_semantics=("arbitrary", "arbitrary"), vmem_limit_bytes=vmem),
        name="diff_attn",
    )(small_logits, lamv, proj, proj, proj, sub_gain)


_SSM_FACTORISED_MAX_LOG = 60.0


def _cmul(ar, ai, br, bi):
    return ar * br - ai * bi, ar * bi + ai * br


def _ssm_kernel(flag_ref, u_ref, mixed_ref, bmat_ref, cmat_ref, pinv_ref, pfwd_ref, pstep_ref, d_ref,
                wglu_ref, bglu_ref, og_ref, *refs, chunk, n_side):
    side_in, o_ref, side_out = refs[:n_side], refs[n_side], refs[n_side + 1:2 * n_side + 1]
    h_ref, y_ref, bu_ref, hs_ref = refs[2 * n_side + 1:]
    del mixed_ref
    c = pl.program_id(1)

    @pl.when(c == 0)
    def _():
        h_ref[...] = jnp.zeros(h_ref.shape, jnp.float32)

    n = SSM_CL_STATE
    cluster_batch = SSM_CLUSTERS
    factorised = flag_ref[0] == 1

    def lanes_of(cl):
        return slice(cl * V7X_LANES, (cl + 1) * V7X_LANES)

    def finish():
        y = y_ref[...] + d_ref[...] * u_ref[...].astype(jnp.float32)
        y = jax.nn.gelu(y, approximate=True)
        z = jnp.dot(y.astype(jnp.bfloat16), wglu_ref[...], preferred_element_type=jnp.float32)
        z = z + bglu_ref[...]
        y = y * (1.0 / (1.0 + jnp.exp(-z)))
        o_ref[...] = _rms_rows(y, og_ref[...]).astype(o_ref.dtype)
        for src, dst in zip(side_in, side_out):
            dst[...] = src[...].astype(dst.dtype)

    @pl.when(factorised)
    def _():
        row = lax.broadcasted_iota(jnp.int32, (chunk, chunk), 0)
        col = lax.broadcasted_iota(jnp.int32, (chunk, chunk), 1)
        tri = (col <= row).astype(jnp.bfloat16)
        for c_lo in range(0, SSM_CLUSTERS, cluster_batch):
            clusters = range(c_lo, c_lo + cluster_batch)
            bus = {cl: jnp.dot(u_ref[:, lanes_of(cl)], bmat_ref[cl],
                               preferred_element_type=jnp.float32) for cl in clusters}
            xs = {}
            for cl in clusters:
                bu = bus[cl]
                xr, xi = _cmul(pinv_ref[cl, :, :n], pinv_ref[cl, :, n:], bu[:, :n], bu[:, n:])
                xs[cl] = jnp.concatenate([xr, xi], axis=1).astype(jnp.bfloat16)
            css = {cl: jnp.dot(tri, xs[cl], preferred_element_type=jnp.float32) for cl in clusters}
            hs = {}
            for cl in clusters:
                cs = css[cl]
                cr, ci = _cmul(pstep_ref[1, cl:cl + 1, :n], pstep_ref[1, cl:cl + 1, n:],
                               h_ref[cl:cl + 1, :n], h_ref[cl:cl + 1, n:])
                hr, hi = _cmul(pfwd_ref[cl, :, :n], pfwd_ref[cl, :, n:],
                               cs[:, :n] + cr, cs[:, n:] + ci)
                h_ref[cl:cl + 1, :n] = hr[chunk - 1:chunk, :]
                h_ref[cl:cl + 1, n:] = hi[chunk - 1:chunk, :]
                hs[cl] = jnp.concatenate([hr, hi], axis=1).astype(jnp.bfloat16)
            for cl in clusters:
                y_ref[:, lanes_of(cl)] = jnp.dot(hs[cl], cmat_ref[cl],
                                                 preferred_element_type=jnp.float32)
        finish()

    @pl.when(jnp.logical_not(factorised))
    def _():
        for cl in range(SSM_CLUSTERS):
            bu_ref[...] = jnp.dot(u_ref[:, lanes_of(cl)], bmat_ref[cl],
                                  preferred_element_type=jnp.float32)
            lr, li = pstep_ref[0, cl:cl + 1, :n], pstep_ref[0, cl:cl + 1, n:]

            def step(t, carry):
                hr, hi = carry
                ar, ai = _cmul(lr, li, hr, hi)
                hr = ar + bu_ref[pl.ds(t, 1), :n]
                hi = ai + bu_ref[pl.ds(t, 1), n:]
                hs_ref[pl.ds(t, 1), :n] = hr
                hs_ref[pl.ds(t, 1), n:] = hi
                return hr, hi

            hr, hi = lax.fori_loop(0, chunk, step, (h_ref[cl:cl + 1, :n], h_ref[cl:cl + 1, n:]))
            h_ref[cl:cl + 1, :n] = hr
            h_ref[cl:cl + 1, n:] = hi
            y_ref[:, lanes_of(cl)] = jnp.dot(hs_ref[...].astype(jnp.bfloat16), cmat_ref[cl],
                                             preferred_element_type=jnp.float32)
        finish()


def _ssm_tables(log_dt, a_re, a_im, b_re, b_im, c_re, c_im, chunk):
    f32 = jnp.float32
    n, c = SSM_STATE, SSM_GROUP
    dt = jnp.exp(log_dt.astype(f32))[:, None]
    ar, ai = a_re.astype(f32), a_im.astype(f32)
    mag = jnp.exp(ar * dt)
    lb_re = mag * jnp.cos(ai * dt)
    lb_im = mag * jnp.sin(ai * dt)
    pr, pi_ = lb_re - 1.0, lb_im
    den = ar * ar + ai * ai
    z_re = ((pr * ar + pi_ * ai) / den)[..., None]
    z_im = ((pi_ * ar - pr * ai) / den)[..., None]
    br, bi = b_re.astype(f32), b_im.astype(f32)
    bb_re = z_re * br - z_im * bi
    bb_im = z_re * bi + z_im * br

    def pair_cols(re, im):
        return jnp.concatenate([re, im], axis=-1)

    log_re = (ar * dt).reshape(SSM_CLUSTERS, 1, SSM_CLUSTER * n)
    log_im = (ai * dt).reshape(SSM_CLUSTERS, 1, SSM_CLUSTER * n)

    def table(k):
        kk = k.astype(f32)[None, :, None]
        pm = jnp.exp(kk * log_re)
        ph = kk * log_im
        return pair_cols(pm * jnp.cos(ph), pm * jnp.sin(ph))

    c0 = chunk // 2
    t = jnp.arange(chunk)
    pfwd = table(t - c0)
    pinv = table(c0 - t)
    pstep = table(jnp.array([1, c0 + 1])).transpose(1, 0, 2)
    factorised = ((c0 + 1) * jnp.max(jnp.abs(log_re)) <= _SSM_FACTORISED_MAX_LOG)

    eye = jnp.eye(SSM_CLUSTER, dtype=f32)
    def bpart(bb):
        b4 = bb.reshape(SSM_CLUSTERS, SSM_CLUSTER, n, c)
        return jnp.einsum('kgnc,gh->kgchn', b4, eye).reshape(
            SSM_CLUSTERS, V7X_LANES, SSM_CLUSTER * n)
    bmat = pair_cols(bpart(bb_re), bpart(bb_im)).astype(jnp.bfloat16)
    def cpart(cc):
        c4 = cc.astype(f32).reshape(SSM_CLUSTERS, SSM_CLUSTER, c, n)
        return jnp.einsum('kgcn,gh->khcgn', c4, eye).reshape(
            SSM_CLUSTERS, V7X_LANES, SSM_CLUSTER * n)
    cmat = pair_cols(cpart(c_re), -cpart(c_im)).transpose(0, 2, 1).astype(jnp.bfloat16)
    return factorised.astype(jnp.int32).reshape(1), bmat, cmat, pinv, pfwd, pstep


def _ssm(proj, mixed, tables, d_skip, w_glu, b_glu, out_gain, side_f32, layer, *, batch, seq,
         chunk=SSM_CHUNK):
    factorised, bmat, cmat, pinv, pfwd, pstep = tables
    nc = seq // chunk
    u_col = 3 * ATTN_WIDTH // SSM_WIDTH
    o_col = ATTN_WIDTH // SSM_WIDTH
    kern = functools.partial(_ssm_kernel, chunk=chunk, n_side=len(side_f32))
    const3 = lambda b, c: (0, 0, 0)
    const2 = lambda b, c: (0, 0)
    once = pl.Buffered(1)
    side_in_specs, side_out_specs, side_shapes, side_bytes = [], [], [], 0
    for a in side_f32:
        _, rows, cols = a.shape
        slab = rows // (batch * nc)
        assert rows % (batch * nc) == 0 and slab % V7X_BF16_ROWS_PER_TILE == 0
        side_in_specs.append(pl.BlockSpec((None, slab, cols), lambda b, c: (layer, b * nc + c, 0)))
        side_out_specs.append(pl.BlockSpec((slab, cols), lambda b, c: (b * nc + c, 0)))
        side_shapes.append(jax.ShapeDtypeStruct((rows, cols), jnp.bfloat16))
        side_bytes += 2 * _nbytes((slab, cols), jnp.float32) + 2 * _nbytes((slab, cols), jnp.bfloat16)
    vmem = _vmem_limit(bmat.nbytes + cmat.nbytes + pinv.nbytes + pfwd.nbytes + pstep.nbytes,
                       _nbytes((SSM_WIDTH, SSM_WIDTH), jnp.bfloat16),
                       18 * _nbytes((chunk, SSM_WIDTH), jnp.float32), side_bytes)
    outs = pl.pallas_call(
        kern,
        out_shape=[jax.ShapeDtypeStruct(mixed.shape, mixed.dtype)] + side_shapes,
        grid=(batch, nc),
        in_specs=[
            pl.BlockSpec(memory_space=pltpu.SMEM),
            pl.BlockSpec((chunk, SSM_WIDTH), lambda b, c: (b * nc + c, u_col)),
            pl.BlockSpec(memory_space=pl.ANY),
            pl.BlockSpec(bmat.shape, const3, pipeline_mode=once),
            pl.BlockSpec(cmat.shape, const3, pipeline_mode=once),
            pl.BlockSpec(pinv.shape, const3, pipeline_mode=once),
            pl.BlockSpec(pfwd.shape, const3, pipeline_mode=once),
            pl.BlockSpec(pstep.shape, const3, pipeline_mode=once),
            pl.BlockSpec((1, SSM_WIDTH), const2),
            pl.BlockSpec((SSM_WIDTH, SSM_WIDTH), const2, pipeline_mode=once),
            pl.BlockSpec((1, SSM_WIDTH), const2),
            pl.BlockSpec((1, SSM_WIDTH), const2),
        ] + side_in_specs,
        out_specs=[pl.BlockSpec((chunk, SSM_WIDTH), lambda b, c: (b * nc + c, o_col))]
        + side_out_specs,
        scratch_shapes=[
            pltpu.VMEM((SSM_CLUSTERS, SSM_CL_COLS), jnp.float32),
            pltpu.VMEM((chunk, SSM_WIDTH), jnp.float32),
            pltpu.VMEM((chunk, SSM_CL_COLS), jnp.float32),
            pltpu.VMEM((chunk, SSM_CL_COLS), jnp.float32),
        ],
        input_output_aliases={2: 0},
        compiler_params=pltpu.CompilerParams(
            dimension_semantics=("arbitrary", "arbitrary"), vmem_limit_bytes=vmem),
        name="s5_ssm",
    )(factorised, proj, mixed, bmat, cmat, pinv, pfwd, pstep, d_skip, w_glu, b_glu, out_gain,
      *side_f32)
    return outs[0], outs[1:]


def _res_matmul_kernel(a_ref, w_ref, r_ref, o_ref, *maybe_bf16_ref):
    out = r_ref[...] + jnp.dot(a_ref[...], w_ref[...], preferred_element_type=jnp.float32)
    o_ref[...] = out
    for ob_ref in maybe_bf16_ref:
        ob_ref[...] = out.astype(ob_ref.dtype)


def _res_matmul(a, w, res, *, tm, tn, name, with_bf16_copy=False):
    m, k = a.shape
    n = w.shape[1]
    assert m % tm == 0 and n % tn == 0
    vmem = _vmem_limit(2 * _nbytes((tm, k), a.dtype), 2 * _nbytes((k, tn), w.dtype),
                       6 * _nbytes((tm, tn), jnp.float32))
    tile = pl.BlockSpec((tm, tn), lambda i, j: (i, j))
    out_shape = [jax.ShapeDtypeStruct((m, n), jnp.float32)]
    if with_bf16_copy:
        out_shape.append(jax.ShapeDtypeStruct((m, n), jnp.bfloat16))
    outs = pl.pallas_call(
        _res_matmul_kernel,
        out_shape=out_shape,
        grid=(m // tm, n // tn),
        in_specs=[
            pl.BlockSpec((tm, k), lambda i, j: (i, 0)),
            pl.BlockSpec((k, tn), lambda i, j: (0, j)),
            tile,
        ],
        out_specs=[tile] * len(out_shape),
        compiler_params=pltpu.CompilerParams(
            dimension_semantics=("parallel", "parallel"), vmem_limit_bytes=vmem),
        name=name,
    )(a, w, res)
    return outs if with_bf16_copy else outs[0]


def _ffn_up_kernel(x_ref, halo_ref, g_ref, wg_ref, wu_ref, cw_ref, cb_ref, o_ref, h_ref,
                   *, tm, blocks_per_seq):
    i = pl.program_id(0)
    j = pl.program_id(1)
    halo = V7X_BF16_ROWS_PER_TILE

    @pl.when(j == 0)
    def _():
        hh = _rms_rows(halo_ref[...].astype(jnp.float32), g_ref[...])
        hh = jnp.where(i % blocks_per_seq == 0, 0.0, hh)
        h_ref[0:halo, :] = hh.astype(h_ref.dtype)
        _norm_into(x_ref, g_ref, h_ref, rows=tm, row_chunk=16, h_offset=halo)

    g_all = jnp.dot(h_ref[...], wg_ref[...], preferred_element_type=jnp.float32)
    up = jnp.dot(h_ref[halo:, :], wu_ref[...], preferred_element_type=jnp.float32)
    rows = 128
    for r0 in range(0, tm, rows):
        for c0 in range(0, o_ref.shape[1], V7X_LANES):
            cs = slice(c0, c0 + V7X_LANES)
            g = g_all[r0:r0 + halo + rows, cs]
            t = cw_ref[1:2, cs] * g + pltpu.roll(cw_ref[0:1, cs] * g, 1, 0)
            gc = (cb_ref[:, cs] + cw_ref[2:3, cs] * g + pltpu.roll(t, 1, 0))[halo:, :]
            act = gc * (1.0 / (1.0 + jnp.exp(-gc))) * up[r0:r0 + rows, cs]
            o_ref[r0:r0 + rows, cs] = act.astype(o_ref.dtype)


def _ffn_up(x1, gain, wg, wu, conv_w, conv_b, *, seq, tm=1024, tf=F_TILE):
    m, k = x1.shape
    f = wg.shape[1]
    halo = V7X_BF16_ROWS_PER_TILE
    assert m % tm == 0 and seq % tm == 0
    kern = functools.partial(_ffn_up_kernel, tm=tm, blocks_per_seq=seq // tm)
    vmem = _vmem_limit(2 * _nbytes((tm, k), x1.dtype), _nbytes((tm + halo, k), jnp.bfloat16),
                       4 * _nbytes((k, tf), jnp.bfloat16), 2 * _nbytes((tm, tf), jnp.bfloat16),
                       3 * _nbytes((tm + halo, tf), jnp.float32))
    return pl.pallas_call(
        kern,
        out_shape=jax.ShapeDtypeStruct((m, f), jnp.bfloat16),
        grid=(m // tm, pl.cdiv(f, tf)),
        in_specs=[
            pl.BlockSpec((tm, k), lambda i, j: (i, 0)),
            pl.BlockSpec((halo, k), lambda i, j: (jnp.maximum(i * (tm // halo) - 1, 0), 0)),
            pl.BlockSpec((1, k), lambda i, j: (0, 0)),
            pl.BlockSpec((k, tf), lambda i, j: (0, j)),
            pl.BlockSpec((k, tf), lambda i, j: (0, j)),
            pl.BlockSpec((CONV_WIDTH, tf), lambda i, j: (0, j)),
            pl.BlockSpec((1, tf), lambda i, j: (0, j)),
        ],
        out_specs=pl.BlockSpec((tm, tf), lambda i, j: (i, j)),
        scratch_shapes=[pltpu.VMEM((tm + halo, k), jnp.bfloat16)],
        compiler_params=pltpu.CompilerParams(
            dimension_semantics=("arbitrary", "arbitrary"), vmem_limit_bytes=vmem),
        name="ffn_up",
    )(x1, x1, gain, wg, wu, conv_w, conv_b)


def kernel(x, attn_norm, w_in, q_gain, k_gain, lam_q1, lam_k1, lam_q2, lam_k2, sub_gain, ssm_log_dt, ssm_a_re, ssm_a_im, ssm_b_re, ssm_b_im, ssm_c_re, ssm_c_im, ssm_d, ssm_w_glu, ssm_b_glu, ssm_out_gain, w_out, ffn_norm, w_gate, w_up, conv_w, conv_b, w_down):
    b, s, d = x.shape
    depth = w_in.shape[0]
    bf16, f32 = jnp.bfloat16, jnp.float32
    xf = x.reshape(b * s, d)
    for l in range(depth):
        lam_init = 0.8 - 0.6 * math.exp(-0.3 * l)
        q_scale = HEAD_DIM ** -0.5 * math.log2(math.e)
        qg, kg = q_gain[l].astype(f32), k_gain[l].astype(f32)
        col_gain = jnp.concatenate([
            jnp.tile(qg * q_scale, ATTN_WIDTH // HEAD_DIM),
            jnp.tile(kg, ATTN_WIDTH // HEAD_DIM),
            jnp.ones((IN_WIDTH - 2 * ATTN_WIDTH,), f32)])[None, :]
        proj, (wo_bf, wd_bf) = _in_proj(
            xf, attn_norm[l].astype(f32)[None, :], w_in[l].astype(bf16), col_gain,
            (w_out, w_down), l)

        logit_bound = 1.01 * HEAD_DIM * q_scale * jnp.max(jnp.abs(qg)) * jnp.max(jnp.abs(kg))
        small_logits = (logit_bound <= _UNSHIFTED_SOFTMAX_MAX_LOG2).astype(jnp.int32).reshape(1)
        lamv = jnp.stack([lam_q1[l], lam_k1[l], lam_q2[l], lam_k2[l]]).astype(f32)
        mixed = _attention(proj, small_logits, lamv, sub_gain[l].astype(f32)[None, :],
                           batch=b, seq=s, lam_init=lam_init)

        tables = _ssm_tables(ssm_log_dt[l], ssm_a_re[l], ssm_a_im[l], ssm_b_re[l], ssm_b_im[l],
                             ssm_c_re[l], ssm_c_im[l], SSM_CHUNK)
        mixed, (wg_bf, wu_bf) = _ssm(
            proj, mixed, tables, ssm_d[l].astype(f32).reshape(1, SSM_WIDTH),
            ssm_w_glu[l].astype(bf16), ssm_b_glu[l].astype(f32)[None, :],
            ssm_out_gain[l].astype(f32)[None, :], (w_gate, w_up), l, batch=b, seq=s)

        x1, x1_bf = _res_matmul(mixed, wo_bf, xf, tm=1024, tn=512, name="out_proj",
                                with_bf16_copy=True)

        act = _ffn_up(x1_bf, ffn_norm[l].astype(f32)[None, :], wg_bf, wu_bf, conv_w[l].astype(f32),
                      conv_b[l].astype(f32)[None, :], seq=s)
        xf = _res_matmul(act, wd_bf, x1, tm=512, tn=512, name="ffn_down")
    return xf.reshape(b, s, d)
```
